```python
import jax, jax.numpy as jnp
from jax import lax
import numpy as np

D_MODEL = 1024
BATCH = 8
SEQ = 4096
DEPTH = 4

GRID_W = 64
CTX_LEN = 256
D_FF = 2752
N_MOD = 9
EPS = 1e-6
D_FOURIER = D_MODEL // 2
FOURIER_CH = 128
FOURIER_GROUPS = D_FOURIER // FOURIER_CH
D_SGU = D_MODEL // 2
SGU_CH = 128
SGU_GROUPS = D_SGU // SGU_CH
CHUNK = 128
HEAD_DIM = 128
N_HEADS = D_MODEL // HEAD_DIM
N_KV_HEADS = 2
GROUP = N_HEADS // N_KV_HEADS
ROPE_AXIS_DIM = HEAD_DIM // 2
ROPE_THETA = 10000.0
Q_BLOCK = 128

kernel_name = "hybrid_fnet_gmlp_gqa_dit_prefix"


def rms_norm(x, g):
    xf = x.astype(jnp.float32)
    y = xf * lax.rsqrt(jnp.mean(jnp.square(xf), axis=-1, keepdims=True) + EPS)
    return (y * g.astype(jnp.float32)).astype(x.dtype)


def adaln(cvec, w_mod, b_mod):
    m = jax.nn.silu(cvec) @ w_mod + b_mod
    m = m.reshape(m.shape[0], 1, N_MOD, D_MODEL)
    return [m[:, :, i] for i in range(N_MOD)]


def modulated_norm(x, g, shift, scale):
    return rms_norm(x, g) * (1 + scale) + shift


def swiglu(h, w_gu, w_down):
    gate, up = jnp.split(h @ w_gu, 2, axis=-1)
    return (jax.nn.silu(gate) * up) @ w_down


def ffn_half_step(x, mods, g, w_gu, w_down):
    shift, scale, gate = mods
    return x + 0.5 * gate * swiglu(modulated_norm(x, g, shift, scale), w_gu, w_down)


def fourier_mix(a):
    b_, n, _ = a.shape
    a4 = a.reshape(b_, n, FOURIER_GROUPS, FOURIER_CH).astype(jnp.float32)
    f = jnp.fft.fftn(a4, axes=(1, 3), norm="ortho").real
    return f.reshape(b_, n, D_FOURIER).astype(a.dtype)


def chunk_sgu(u, v, g_v, w_s, b_s):
    b_, n, _ = v.shape
    vh = rms_norm(v.reshape(b_, n, SGU_GROUPS, SGU_CH), g_v)
    vc = vh.reshape(b_, n // CHUNK, CHUNK, SGU_GROUPS, SGU_CH)
    mixed = jnp.einsum('hpq,bcqhd->bcphd', w_s, vc) + b_s.T[None, None, :, :, None]
    return u * mixed.reshape(b_, n, D_SGU)


def fourier_sgu_mixer(h, w_in, g_v, w_s, b_s, w_out):
    a, uv = jnp.split(h @ w_in, [D_FOURIER], axis=-1)
    u, v = jnp.split(jax.nn.gelu(uv), 2, axis=-1)
    out = jnp.concatenate([fourier_mix(a), chunk_sgu(u, v, g_v, w_s, b_s)], axis=-1)
    return out @ w_out


def axial_angles(rows):
    row = jnp.repeat(jnp.arange(rows, dtype=jnp.float32), GRID_W)
    col = jnp.tile(jnp.arange(GRID_W, dtype=jnp.float32), rows)
    inv_freq = ROPE_THETA ** (-jnp.arange(0, ROPE_AXIS_DIM, 2, dtype=jnp.float32) / ROPE_AXIS_DIM)
    return row[:, None] * inv_freq, col[:, None] * inv_freq


def rotate(x, ang):
    full = jnp.concatenate([ang, ang], axis=-1)[:, None, :]
    cos = jnp.cos(full).astype(x.dtype)
    sin = jnp.sin(full).astype(x.dtype)
    x1, x2 = jnp.split(x, 2, axis=-1)
    return x * cos + jnp.concatenate([-x2, x1], axis=-1) * sin


def axial_rope(x, ang_row, ang_col):
    xr, xc = jnp.split(x, 2, axis=-1)
    return jnp.concatenate([rotate(xr, ang_row), rotate(xc, ang_col)], axis=-1)


def project_kv(kv, g_k):
    b_, n, _ = kv.shape
    k, v = jnp.split(kv, 2, axis=-1)
    k = rms_norm(k.reshape(b_, n, N_KV_HEADS, HEAD_DIM), g_k)
    return k, v.reshape(b_, n, N_KV_HEADS, HEAD_DIM)


def project_q(q, g_q):
    b_, n, _ = q.shape
    return rms_norm(q.reshape(b_, n, N_HEADS, HEAD_DIM), g_q)


def gqa_attend(q, k, v):
    b_, nq = q.shape[:2]
    qg = q.reshape(b_, nq, N_KV_HEADS, GROUP, HEAD_DIM)
    s = jnp.einsum('bqkgd,bskd->bkgqs', qg, k, preferred_element_type=jnp.float32) * (HEAD_DIM ** -0.5)
    p = jax.nn.softmax(s, axis=-1).astype(v.dtype)
    o = jnp.einsum('bkgqs,bskd->bqkgd', p, v)
    return o.reshape(b_, nq, N_HEADS * HEAD_DIM)


def blocked_attention(q, k_all, v_all):
    b_, n = q.shape[:2]
    n_blk = n // Q_BLOCK
    qb = jnp.moveaxis(q.reshape(b_, n_blk, Q_BLOCK, N_HEADS, HEAD_DIM), 1, 0)
    o = lax.map(lambda qblk: gqa_attend(qblk, k_all, v_all), qb)
    return jnp.moveaxis(o, 0, 1).reshape(b_, n, N_HEADS * HEAD_DIM)


def setup_inputs(seed: int = 0) -> dict:
    key = jax.random.key(seed)
    ks = jax.random.split(key, 24)
    n_even = (DEPTH + 1) // 2
    n_odd = DEPTH // 2

    def nrm(k, shape, fan_in, s=1.0):
        return s * jax.random.normal(k, shape, jnp.float32) * (fan_in ** -0.5)

    def gain(k, shape):
        return 1.0 + 0.05 * jax.random.normal(k, shape, jnp.float32)

    return {
        "x": jax.random.normal(ks[0], (BATCH, SEQ, D_MODEL), jnp.float32),
        "c": jax.random.normal(ks[1], (BATCH, D_MODEL), jnp.float32),
        "ctx": jax.random.normal(ks[2], (BATCH, CTX_LEN, D_MODEL), jnp.float32),
        "c_ctx": jax.random.normal(ks[3], (D_MODEL,), jnp.float32),
        "w_mod": nrm(ks[4], (DEPTH, D_MODEL, N_MOD * D_MODEL), D_MODEL, 0.5),
        "b_mod": 0.02 * jax.random.normal(ks[5], (DEPTH, N_MOD * D_MODEL), jnp.float32),
        "g_ffn1": gain(ks[6], (DEPTH, D_MODEL)),
        "w_ffn1_gu": nrm(ks[7], (DEPTH, D_MODEL, 2 * D_FF), D_MODEL),
        "w_ffn1_down": nrm(ks[8], (DEPTH, D_FF, D_MODEL), D_FF),
        "g_mix": gain(ks[9], (DEPTH, D_MODEL)),
        "g_ffn2": gain(ks[10], (DEPTH, D_MODEL)),
        "w_ffn2_gu": nrm(ks[11], (DEPTH, D_MODEL, 2 * D_FF), D_MODEL),
        "w_ffn2_down": nrm(ks[12], (DEPTH, D_FF, D_MODEL), D_FF),
        "g_final": gain(ks[13], (D_MODEL,)),
        "w_in_ab": nrm(ks[14], (n_even, D_MODEL, D_FOURIER + 2 * D_SGU), D_MODEL),
        "g_v": gain(ks[15], (n_even, SGU_GROUPS, SGU_CH)),
        "w_s": nrm(ks[16], (n_even, SGU_GROUPS, CHUNK, CHUNK), CHUNK, 0.5),
        "b_s": 1.0 + 0.02 * jax.random.normal(ks[17], (n_even, SGU_GROUPS, CHUNK), jnp.float32),
        "w_out_ab": nrm(ks[18], (n_even, D_FOURIER + D_SGU, D_MODEL), D_FOURIER + D_SGU),
        "w_qkv": nrm(ks[19], (n_odd, D_MODEL, (N_HEADS + 2 * N_KV_HEADS) * HEAD_DIM), D_MODEL),
        "g_q": gain(ks[20], (n_odd, HEAD_DIM)),
        "g_k": gain(ks[21], (n_odd, HEAD_DIM)),
        "w_o": nrm(ks[22], (n_odd, N_HEADS * HEAD_DIM, D_MODEL), N_HEADS * HEAD_DIM),
    }


def reference(x, c, ctx, c_ctx, w_mod, b_mod, g_ffn1, w_ffn1_gu, w_ffn1_down, g_mix,
              g_ffn2, w_ffn2_gu, w_ffn2_down, g_final, w_in_ab, g_v, w_s, b_s, w_out_ab,
              w_qkv, g_q, g_k, w_o):
    n_tok = x.shape[1]
    ROWS = n_tok // GRID_W
    ang_row, ang_col = axial_angles(ROWS)
    q_cols = N_HEADS * HEAD_DIM

    for l in range(DEPTH):
        last = l == DEPTH - 1
        even = l % 2 == 0
        j = l // 2
        mx = adaln(c, w_mod[l], b_mod[l])
        mc = adaln(c_ctx[None], w_mod[l], b_mod[l])
        ctx_live = (not last) or (not even)

        x = ffn_half_step(x, mx[0:3], g_ffn1[l], w_ffn1_gu[l], w_ffn1_down[l])
        if ctx_live:
            ctx = ffn_half_step(ctx, mc[0:3], g_ffn1[l], w_ffn1_gu[l], w_ffn1_down[l])

        hx = modulated_norm(x, g_mix[l], mx[3], mx[4])
        if even:
            out_x = fourier_sgu_mixer(hx, w_in_ab[j], g_v[j], w_s[j], b_s[j], w_out_ab[j])
            if not last:
                hc = modulated_norm(ctx, g_mix[l], mc[3], mc[4])
                out_c = fourier_sgu_mixer(hc, w_in_ab[j], g_v[j], w_s[j], b_s[j], w_out_ab[j])
        else:
            hc = modulated_norm(ctx, g_mix[l], mc[3], mc[4])
            qkv_x = hx @ w_qkv[j]
            q_x = axial_rope(project_q(qkv_x[..., :q_cols], g_q[j]), ang_row, ang_col)
            k_x, v_x = project_kv(qkv_x[..., q_cols:], g_k[j])
            k_x = axial_rope(k_x, ang_row, ang_col)
            if last:
                k_c, v_c = project_kv(hc @ w_qkv[j][:, q_cols:], g_k[j])
            else:
                qkv_c = hc @ w_qkv[j]
                k_c, v_c = project_kv(qkv_c[..., q_cols:], g_k[j])
                q_c = project_q(qkv_c[..., :q_cols], g_q[j])
                out_c = gqa_attend(q_c, k_c, v_c) @ w_o[j]
            k_all = jnp.concatenate([k_x, k_c], axis=1)
            v_all = jnp.concatenate([v_x, v_c], axis=1)
            out_x = blocked_attention(q_x, k_all, v_all) @ w_o[j]

        x = x + mx[5] * out_x
        x = ffn_half_step(x, mx[6:9], g_ffn2[l], w_ffn2_gu[l], w_ffn2_down[l])
        if not last:
            ctx = ctx + mc[5] * out_c
            ctx = ffn_half_step(ctx, mc[6:9], g_ffn2[l], w_ffn2_gu[l], w_ffn2_down[l])

    return rms_norm(x, g_final)
```

```python
import functools
import math

import jax
import jax.numpy as jnp
from jax import lax
from jax.experimental import pallas as pl
from jax.experimental.pallas import tpu as pltpu

D = 1024
SEQ = 4096
CTX = 256
ROWS = SEQ + CTX
GRID_W = 64
D_FF = 2752
N_MOD = 9
EPS = 1e-6
HEAD_DIM = 128
N_HEADS = 8
N_KV = 2
GROUP = N_HEADS // N_KV
ROPE_THETA = 10000.0
D_FOURIER = 512
D_SGU = 512
GROUP_CH = 128
N_GROUPS = 4
CHUNK = 128

LANE = 128
MXU_DIM = 256
TM = 256
N_LAT_TILES = SEQ // TM
N_TILES = ROWS // TM
D_FF_PAD = -(-D_FF // MXU_DIM) * MXU_DIM
TK = 256
VMEM_LIMIT = 48 * 1024 * 1024


def _cparams(sem):
    return pltpu.CompilerParams(dimension_semantics=sem, vmem_limit_bytes=VMEM_LIMIT)


def _resident(block_shape, index_map):
    return pl.BlockSpec(block_shape, index_map, pipeline_mode=pl.Buffered(1))


def _mod_index(b, i):
    return (jnp.where(i == N_LAT_TILES, pl.num_programs(0), b), 0, 0)

def _modnorm(x, g, shift, scale):
    ms = jnp.mean(x * x, axis=-1, keepdims=True)
    y = x * lax.rsqrt(ms + EPS) * g
    return y * (1.0 + scale) + shift


def _mod_kernel(cc_ref, w_ref, b_ref, o_ref):
    s = cc_ref[...]
    s = s * jax.nn.sigmoid(s)
    s_hi = s.astype(jnp.bfloat16)
    s_lo = (s - s_hi.astype(jnp.float32)).astype(jnp.bfloat16)
    w = w_ref[...]
    w_hi = w.astype(jnp.bfloat16)
    w_lo = (w - w_hi.astype(jnp.float32)).astype(jnp.bfloat16)
    rows = s.shape[0]
    both = jnp.dot(jnp.concatenate([s_hi, s_lo], axis=0), w_hi,
                   preferred_element_type=jnp.float32)
    cross = jnp.dot(s_hi, w_lo, preferred_element_type=jnp.float32)
    o_ref[...] = both[:rows] + both[rows:] + cross + b_ref[...]


def _modulation(cc, w_mod, b_mod):
    depth = w_mod.shape[0]
    rows = cc.shape[0]
    tn = 1024
    return pl.pallas_call(
        _mod_kernel,
        grid=(depth, N_MOD * D // tn),
        in_specs=[
            pl.BlockSpec((rows, D), lambda l, j: (0, 0)),
            pl.BlockSpec((None, D, tn), lambda l, j: (l, 0, j)),
            pl.BlockSpec((None, 1, tn), lambda l, j: (l, 0, j)),
        ],
        out_specs=pl.BlockSpec((None, rows, tn), lambda l, j: (l, 0, j)),
        out_shape=jax.ShapeDtypeStruct((depth, rows, N_MOD * D), jnp.float32),
        compiler_params=_cparams(("arbitrary", "arbitrary")),
        name="modulation",
    )(cc, w_mod, b_mod.reshape(depth, 1, N_MOD * D))


def _ffn_kernel(x_ref, mod_ref, g_ref, wg_ref, wu_ref, wd_ref, *rest, row0, final):
    if final:
        gf_ref, o_ref = rest
    else:
        (o_ref,) = rest
    x = x_ref[...]
    shift = mod_ref[row0:row0 + 1, :]
    scale = mod_ref[row0 + 1:row0 + 2, :]
    gate = mod_ref[row0 + 2:row0 + 3, :]
    h = _modnorm(x, g_ref[...], shift, scale).astype(jnp.bfloat16)
    gt = jnp.dot(h, wg_ref[...], preferred_element_type=jnp.float32)
    up = jnp.dot(h, wu_ref[...], preferred_element_type=jnp.float32)
    a = (gt * jax.nn.sigmoid(gt) * up).astype(jnp.bfloat16)
    y = jnp.dot(a, wd_ref[...], preferred_element_type=jnp.float32)
    out = x + 0.5 * gate * y
    if final:
        ms = jnp.mean(out * out, axis=-1, keepdims=True)
        out = out * lax.rsqrt(ms + EPS) * gf_ref[...]
    o_ref[...] = out


def _ffn(xc, mods, g, wg, wu, wd, layer, row0, n_tiles, g_final=None):
    batch = xc.shape[0]
    final = g_final is not None
    in_specs = [
        pl.BlockSpec((None, TM, D), lambda b, i: (b, i, 0)),
        pl.BlockSpec((None, N_MOD, D), _mod_index),
        pl.BlockSpec((None, 1, D), lambda b, i: (layer, 0, 0)),
        _resident((None, D, D_FF_PAD), lambda b, i: (layer, 0, 0)),
        _resident((None, D, D_FF_PAD), lambda b, i: (layer, 0, 0)),
        _resident((None, D_FF_PAD, D), lambda b, i: (layer, 0, 0)),
    ]
    args = [xc, mods, g, wg, wu, wd]
    if final:
        in_specs.append(pl.BlockSpec((1, D), lambda b, i: (0, 0)))
        args.append(g_final)
    return pl.pallas_call(
        functools.partial(_ffn_kernel, row0=row0, final=final),
        grid=(batch, n_tiles),
        in_specs=in_specs,
        out_specs=pl.BlockSpec((None, TM, D), lambda b, i: (b, i, 0)),
        out_shape=jax.ShapeDtypeStruct((batch, n_tiles * TM, D), jnp.float32),
        compiler_params=_cparams(("arbitrary", "arbitrary")),
        name="ffn_final" if final else "ffn",
    )(*args)


def _even_in_kernel(x_ref, mod_ref, g_ref, win_ref, ccs_ref, gv_ref, ws_ref, bs_ref,
                    pq_ref, sgu_ref):
    x = x_ref[...]
    h = _modnorm(x, g_ref[...], mod_ref[3:4, :], mod_ref[4:5, :]).astype(jnp.bfloat16)
    p = jnp.dot(h, win_ref[...], preferred_element_type=jnp.float32)
    a = p[:, :D_FOURIER].astype(jnp.bfloat16)
    uv = jax.nn.gelu(p[:, D_FOURIER:], approximate=True)
    u = uv[:, :D_SGU]
    v = uv[:, D_SGU:]
    ccs = ccs_ref[...]
    ps, qs, gated = [], [], []
    n_chunks = TM // CHUNK
    for grp in range(N_GROUPS):
        lo, hi = grp * GROUP_CH, (grp + 1) * GROUP_CH
        t = jnp.dot(a[:, lo:hi], ccs, preferred_element_type=jnp.float32)
        ps.append(t[:, :GROUP_CH])
        qs.append(t[:, GROUP_CH:])
        vg = v[:, lo:hi]
        ms = jnp.mean(vg * vg, axis=-1, keepdims=True)
        vh = (vg * lax.rsqrt(ms + EPS) * gv_ref[:, lo:hi]).astype(jnp.bfloat16)
        rhs = jnp.concatenate([vh[c * CHUNK:(c + 1) * CHUNK, :] for c in range(n_chunks)], axis=1)
        mixed = jnp.dot(ws_ref[grp], rhs, preferred_element_type=jnp.float32)
        mixed = jnp.concatenate(
            [mixed[:, c * GROUP_CH:(c + 1) * GROUP_CH] for c in range(n_chunks)], axis=0)
        bias = jnp.concatenate([bs_ref[:, lo:hi]] * n_chunks, axis=0)
        gated.append(u[:, lo:hi] * (mixed + bias))
    pq_ref[...] = jnp.concatenate(ps + qs, axis=1).astype(jnp.bfloat16)
    sgu_ref[...] = jnp.concatenate(gated, axis=1).astype(jnp.bfloat16)


def _even_in(xc, mods, g_mix, w_in, ccs, gv, ws, bs_full, layer, j, n_tiles):
    batch = xc.shape[0]
    rows = n_tiles * TM
    return pl.pallas_call(
        _even_in_kernel,
        grid=(batch, n_tiles),
        in_specs=[
            pl.BlockSpec((None, TM, D), lambda b, i: (b, i, 0)),
            pl.BlockSpec((None, N_MOD, D), _mod_index),
            pl.BlockSpec((None, 1, D), lambda b, i: (layer, 0, 0)),
            _resident((None, D, D_FOURIER + 2 * D_SGU), lambda b, i: (j, 0, 0)),
            _resident((GROUP_CH, 2 * GROUP_CH), lambda b, i: (0, 0)),
            pl.BlockSpec((None, 1, D_SGU), lambda b, i: (j, 0, 0)),
            _resident((None, N_GROUPS, CHUNK, CHUNK), lambda b, i: (j, 0, 0, 0)),
            pl.BlockSpec((None, CHUNK, D_SGU), lambda b, i: (j, 0, 0)),
        ],
        out_specs=[
            pl.BlockSpec((None, TM, 2 * D_FOURIER), lambda b, i: (b, i, 0)),
            pl.BlockSpec((None, TM, D_SGU), lambda b, i: (b, i, 0)),
        ],
        out_shape=[
            jax.ShapeDtypeStruct((batch, rows, 2 * D_FOURIER), jnp.bfloat16),
            jax.ShapeDtypeStruct((batch, rows, D_SGU), jnp.bfloat16),
        ],
        compiler_params=_cparams(("arbitrary", "arbitrary")),
        name="even_in",
    )(xc, mods, g_mix, w_in, ccs, gv, ws, bs_full)


def _dft_kernel(c_ref, s_ref, pq_ref, o_ref):
    p = pq_ref[:, :D_FOURIER]
    q = pq_ref[:, D_FOURIER:]
    acc = jnp.dot(c_ref[...], p, preferred_element_type=jnp.float32)
    acc = acc + jnp.dot(s_ref[...], q, preferred_element_type=jnp.float32)
    o_ref[...] = acc.astype(jnp.bfloat16)


def _dft(cmat, smat, pq):
    batch, rows, _ = pq.shape
    n_tiles = rows // TM
    return pl.pallas_call(
        _dft_kernel,
        grid=(batch, n_tiles),
        in_specs=[
            pl.BlockSpec((TM, rows), lambda b, i: (i, 0)),
            pl.BlockSpec((TM, rows), lambda b, i: (i, 0)),
            pl.BlockSpec((None, rows, 2 * D_FOURIER), lambda b, i: (b, 0, 0)),
        ],
        out_specs=pl.BlockSpec((None, TM, D_FOURIER), lambda b, i: (b, i, 0)),
        out_shape=jax.ShapeDtypeStruct((batch, rows, D_FOURIER), jnp.bfloat16),
        compiler_params=_cparams(("arbitrary", "arbitrary")),
        name="dft",
    )(cmat, smat, pq)


def _out_proj_kernel(x_ref, mod_ref, w_ref, *rest):
    *y_refs, o_ref = rest
    y = jnp.concatenate([r[...] for r in y_refs], axis=1) if len(y_refs) > 1 else y_refs[0][...]
    out = jnp.dot(y, w_ref[...], preferred_element_type=jnp.float32)
    o_ref[...] = x_ref[...] + mod_ref[5:6, :] * out


def _out_proj(xc, mods, w, j, ys, n_tiles):
    batch = xc.shape[0]
    in_specs = [
        pl.BlockSpec((None, TM, D), lambda b, i: (b, i, 0)),
        pl.BlockSpec((None, N_MOD, D), _mod_index),
        _resident((None, D, D), lambda b, i: (j, 0, 0)),
    ]
    for y in ys:
        in_specs.append(pl.BlockSpec((None, TM, y.shape[-1]), lambda b, i: (b, i, 0)))
    return pl.pallas_call(
        _out_proj_kernel,
        grid=(batch, n_tiles),
        in_specs=in_specs,
        out_specs=pl.BlockSpec((None, TM, D), lambda b, i: (b, i, 0)),
        out_shape=jax.ShapeDtypeStruct((batch, n_tiles * TM, D), jnp.float32),
        compiler_params=_cparams(("arbitrary", "arbitrary")),
        name="out_proj",
    )(xc, mods, w, *ys)


def _rope(t, cos, sin_signed, first_half):
    partner = jnp.where(first_half, pltpu.roll(t, 3 * HEAD_DIM // 4, 1),
                        pltpu.roll(t, HEAD_DIM // 4, 1))
    return t * cos + partner * sin_signed


def _qkv_kernel(x_ref, mod_ref, g_ref, w_ref, gq_ref, gk_ref, cos_ref, sin_ref,
                q_ref, k_ref, v_ref):
    x = x_ref[...]
    h = _modnorm(x, g_ref[...], mod_ref[3:4, :], mod_ref[4:5, :]).astype(jnp.bfloat16)
    qkv = jnp.dot(h, w_ref[...], preferred_element_type=jnp.float32)
    cos = cos_ref[...]
    sin = sin_ref[...]
    lane = lax.broadcasted_iota(jnp.int32, (TM, HEAD_DIM), 1)
    first_half = (lane % (HEAD_DIM // 2)) < (HEAD_DIM // 4)
    q_scale = (HEAD_DIM ** -0.5) * math.log2(math.e)

    def head(col, gain):
        t = qkv[:, col:col + HEAD_DIM]
        ms = jnp.mean(t * t, axis=-1, keepdims=True)
        return _rope(t * lax.rsqrt(ms + EPS) * gain, cos, sin, first_half)

    qs = [head(hd * HEAD_DIM, gq_ref[...]) * q_scale for hd in range(N_HEADS)]
    ks = [head((N_HEADS + hd) * HEAD_DIM, gk_ref[...]) for hd in range(N_KV)]
    q_ref[...] = jnp.concatenate(qs, axis=1).astype(jnp.bfloat16)
    k_ref[...] = jnp.concatenate(ks, axis=1).astype(jnp.bfloat16)
    v_ref[...] = qkv[:, (N_HEADS + N_KV) * HEAD_DIM:].astype(jnp.bfloat16)


def _qkv(xc, mods, g_mix, w_qkv, gq, gk, cos_t, sin_t, layer, j):
    batch = xc.shape[0]
    kv_w = N_KV * HEAD_DIM
    return pl.pallas_call(
        _qkv_kernel,
        grid=(batch, N_TILES),
        in_specs=[
            pl.BlockSpec((None, TM, D), lambda b, i: (b, i, 0)),
            pl.BlockSpec((None, N_MOD, D), _mod_index),
            pl.BlockSpec((None, 1, D), lambda b, i: (layer, 0, 0)),
            _resident((None, D, D + 2 * kv_w), lambda b, i: (j, 0, 0)),
            pl.BlockSpec((None, 1, HEAD_DIM), lambda b, i: (j, 0, 0)),
            pl.BlockSpec((None, 1, HEAD_DIM), lambda b, i: (j, 0, 0)),
            pl.BlockSpec((TM, HEAD_DIM), lambda b, i: (i, 0)),
            pl.BlockSpec((TM, HEAD_DIM), lambda b, i: (i, 0)),
        ],
        out_specs=[
            pl.BlockSpec((None, TM, D), lambda b, i: (b, i, 0)),
            pl.BlockSpec((None, TM, kv_w), lambda b, i: (b, i, 0)),
            pl.BlockSpec((None, TM, kv_w), lambda b, i: (b, i, 0)),
        ],
        out_shape=[
            jax.ShapeDtypeStruct((batch, ROWS, D), jnp.bfloat16),
            jax.ShapeDtypeStruct((batch, ROWS, kv_w), jnp.bfloat16),
            jax.ShapeDtypeStruct((batch, ROWS, kv_w), jnp.bfloat16),
        ],
        compiler_params=_cparams(("arbitrary", "arbitrary")),
        name="qkv",
    )(xc, mods, g_mix, w_qkv, gq, gk, cos_t, sin_t)


def _attn_kernel(q_ref, k_ref, v_ref, o_ref, m_ref, acc_ref):
    i = pl.program_id(2)
    rows = GROUP * TM
    q = jnp.concatenate([q_ref[:, hd * HEAD_DIM:(hd + 1) * HEAD_DIM] for hd in range(GROUP)],
                        axis=0)
    m_ref[...] = jnp.full((rows, 1), -jnp.inf, jnp.float32)
    acc_ref[...] = jnp.zeros((rows, 2 * HEAD_DIM), jnp.float32)
    ones = jnp.ones((TK, HEAD_DIM), jnp.bfloat16)

    def step(c, carry):
        start = pl.multiple_of(c * TK, TK)
        k = k_ref[pl.ds(start, TK), :]
        v = v_ref[pl.ds(start, TK), :]
        s = lax.dot_general(q, k, (((1,), (1,)), ((), ())),
                            preferred_element_type=jnp.float32)
        m_old = m_ref[...]
        m_new = jnp.maximum(m_old, jnp.max(s, axis=-1, keepdims=True))
        p = jnp.exp2(s - m_new).astype(jnp.bfloat16)
        pv = jnp.dot(p, jnp.concatenate([v, ones], axis=1),
                     preferred_element_type=jnp.float32)
        acc_ref[...] = acc_ref[...] * jnp.exp2(m_old - m_new) + pv
        m_ref[...] = m_new
        return carry

    first = jnp.where(i == N_LAT_TILES, SEQ // TK, 0)
    lax.fori_loop(first, ROWS // TK, step, 0)
    acc = acc_ref[...]
    out = acc[:, :HEAD_DIM] / acc[:, HEAD_DIM:]
    o_ref[...] = jnp.concatenate([out[hd * TM:(hd + 1) * TM, :] for hd in range(GROUP)],
                                 axis=1).astype(jnp.bfloat16)


def _attention(q, k, v, n_tiles):
    batch = q.shape[0]
    gw = GROUP * HEAD_DIM
    return pl.pallas_call(
        _attn_kernel,
        grid=(batch, N_KV, n_tiles),
        in_specs=[
            pl.BlockSpec((None, TM, gw), lambda b, g, i: (b, i, g)),
            pl.BlockSpec((None, ROWS, HEAD_DIM), lambda b, g, i: (b, 0, g)),
            pl.BlockSpec((None, ROWS, HEAD_DIM), lambda b, g, i: (b, 0, g)),
        ],
        out_specs=pl.BlockSpec((None, TM, gw), lambda b, g, i: (b, i, g)),
        out_shape=jax.ShapeDtypeStruct((batch, n_tiles * TM, D), jnp.bfloat16),
        scratch_shapes=[
            pltpu.VMEM((GROUP * TM, 1), jnp.float32),
            pltpu.VMEM((GROUP * TM, 2 * HEAD_DIM), jnp.float32),
        ],
        compiler_params=_cparams(("arbitrary", "arbitrary", "arbitrary")),
        name="attention",
    )(q, k, v)


def _dft_matrices():
    def block(n):
        idx = jnp.arange(n, dtype=jnp.int32)
        r = (idx[:, None] * idx[None, :]) % n
        ang = r.astype(jnp.float32) * (2.0 * math.pi / n)
        s = n ** -0.5
        return jnp.cos(ang) * s, jnp.sin(ang) * s

    def diag(a, b):
        top = jnp.concatenate([a, jnp.zeros((SEQ, CTX), jnp.float32)], axis=1)
        bot = jnp.concatenate([jnp.zeros((CTX, SEQ), jnp.float32), b], axis=1)
        return jnp.concatenate([top, bot], axis=0).astype(jnp.bfloat16)

    cl, sl = block(SEQ)
    cc, sc = block(CTX)
    return diag(cl, cc), diag(sl, sc)


def _channel_dft():
    idx = jnp.arange(GROUP_CH, dtype=jnp.int32)
    r = (idx[:, None] * idx[None, :]) % GROUP_CH
    ang = r.astype(jnp.float32) * (2.0 * math.pi / GROUP_CH)
    s = GROUP_CH ** -0.5
    return jnp.concatenate([jnp.cos(ang) * s, -jnp.sin(ang) * s], axis=1).astype(jnp.bfloat16)


def _rope_tables():
    half = HEAD_DIM // 2
    rows = SEQ // GRID_W
    row = jnp.repeat(jnp.arange(rows, dtype=jnp.float32), GRID_W)
    col = jnp.tile(jnp.arange(GRID_W, dtype=jnp.float32), rows)
    inv_freq = ROPE_THETA ** (-jnp.arange(0, half, 2, dtype=jnp.float32) / half)
    ang_row = row[:, None] * inv_freq
    ang_col = col[:, None] * inv_freq
    cos = jnp.concatenate([jnp.cos(ang_row)] * 2 + [jnp.cos(ang_col)] * 2, axis=1)
    sin = jnp.concatenate([-jnp.sin(ang_row), jnp.sin(ang_row),
                           -jnp.sin(ang_col), jnp.sin(ang_col)], axis=1)
    cos = jnp.concatenate([cos, jnp.ones((CTX, HEAD_DIM), jnp.float32)], axis=0)
    sin = jnp.concatenate([sin, jnp.zeros((CTX, HEAD_DIM), jnp.float32)], axis=0)
    return cos, sin


def _split_pad_ffn(w_gu, w_down):
    pad = D_FF_PAD - D_FF
    wg = jnp.pad(w_gu[:, :, :D_FF], ((0, 0), (0, 0), (0, pad))).astype(jnp.bfloat16)
    wu = jnp.pad(w_gu[:, :, D_FF:], ((0, 0), (0, 0), (0, pad))).astype(jnp.bfloat16)
    wd = jnp.pad(w_down, ((0, 0), (0, pad), (0, 0))).astype(jnp.bfloat16)
    return wg, wu, wd


def kernel(x, c, ctx, c_ctx, w_mod, b_mod, g_ffn1, w_ffn1_gu, w_ffn1_down, g_mix, g_ffn2, w_ffn2_gu, w_ffn2_down, g_final, w_in_ab, g_v, w_s, b_s, w_out_ab, w_qkv, g_q, g_k, w_o):
    batch = x.shape[0]
    depth = w_mod.shape[0]
    assert x.shape == (batch, SEQ, D) and ctx.shape == (batch, CTX, D)
    assert depth % 2 == 0

    mod_rows = -(-(batch + 1) // 8) * 8
    cc = jnp.concatenate([c, c_ctx[None], jnp.zeros((mod_rows - batch - 1, D), jnp.float32)], axis=0)
    m = _modulation(cc, w_mod, b_mod)
    m = m[:, :batch + 1].reshape(depth, batch + 1, N_MOD, D)

    wg1, wu1, wd1 = _split_pad_ffn(w_ffn1_gu, w_ffn1_down)
    wg2, wu2, wd2 = _split_pad_ffn(w_ffn2_gu, w_ffn2_down)
    g1 = g_ffn1.reshape(depth, 1, D)
    g2 = g_ffn2.reshape(depth, 1, D)
    gm = g_mix.reshape(depth, 1, D)
    w_in = w_in_ab.astype(jnp.bfloat16)
    w_out = w_out_ab.astype(jnp.bfloat16)
    ws = w_s.astype(jnp.bfloat16)
    gv = g_v.reshape(-1, 1, D_SGU)
    bs_full = jnp.repeat(jnp.swapaxes(b_s, 1, 2), GROUP_CH, axis=2)
    wqkv = w_qkv.astype(jnp.bfloat16)
    wo = w_o.astype(jnp.bfloat16)
    gq = g_q.reshape(-1, 1, HEAD_DIM)
    gk = g_k.reshape(-1, 1, HEAD_DIM)
    cmat, smat = _dft_matrices()
    ccs = _channel_dft()
    cos_t, sin_t = _rope_tables()

    xc = jnp.concatenate([x, ctx], axis=1)
    for l in range(depth):
        last = l == depth - 1
        j = l // 2
        mods = m[l]
        xc = _ffn(xc, mods, g1, wg1, wu1, wd1, l, 0, N_TILES)
        n_out = N_LAT_TILES if last else N_TILES
        if l % 2 == 0:
            pq, sgu = _even_in(xc, mods, gm, w_in, ccs, gv, ws, bs_full, l, j, N_TILES)
            fo = _dft(cmat, smat, pq)
            xc = _out_proj(xc, mods, w_out, j, [fo, sgu], n_out)
        else:
            q, k, v = _qkv(xc, mods, gm, wqkv, gq, gk, cos_t, sin_t, l, j)
            o = _attention(q, k, v, n_out)
            xc = _out_proj(xc, mods, wo, j, [o], n_out)
        xc = _ffn(xc, mods, g2, wg2, wu2, wd2, l, 6, n_out,
                  g_final.reshape(1, D) if last else None)
    return xc
```

```python
import functools
import math

import jax
import jax.numpy as jnp
from jax import lax
from jax.experimental import pallas as pl
from jax.experimental.pallas import tpu as pltpu

D = 1024
SEQ = 4096
CTX = 256
ROWS = SEQ + CTX
GRID_W = 64
D_FF = 2752
N_MOD = 9
EPS = 1e-6
HEAD_DIM = 128
N_HEADS = 8
N_KV = 2
GROUP = N_HEADS // N_KV
ROPE_THETA = 10000.0
D_FOURIER = 512
D_SGU = 512
GROUP_CH = 128
N_GROUPS = 4
CHUNK = 128

LANE = 128
MXU_DIM = 256
TM = 256
N_LAT_TILES = SEQ // TM
N_TILES = ROWS // TM
D_FF_PAD = -(-D_FF // MXU_DIM) * MXU_DIM
VMEM_LIMIT = 48 * 1024 * 1024


def _cparams(sem):
    return pltpu.CompilerParams(dimension_semantics=sem, vmem_limit_bytes=VMEM_LIMIT)


def _resident(block_shape, index_map):
    return pl.BlockSpec(block_shape, index_map, pipeline_mode=pl.Buffered(1))


def _mod_index(b, i):
    return (jnp.where(i == N_LAT_TILES, pl.num_programs(0), b), 0, 0)

def _modnorm(x, g, shift, scale):
    ms = jnp.mean(x * x, axis=-1, keepdims=True)
    y = x * lax.rsqrt(ms + EPS) * g
    return y * (1.0 + scale) + shift


def _mod_kernel(cc_ref, w_ref, b_ref, o_ref):
    s = cc_ref[...]
    s = s * jax.nn.sigmoid(s)
    s_hi = s.astype(jnp.bfloat16)
    s_lo = (s - s_hi.astype(jnp.float32)).astype(jnp.bfloat16)
    w = w_ref[...]
    w_hi = w.astype(jnp.bfloat16)
    w_lo = (w - w_hi.astype(jnp.float32)).astype(jnp.bfloat16)
    rows = s.shape[0]
    both = jnp.dot(jnp.concatenate([s_hi, s_lo], axis=0), w_hi,
                   preferred_element_type=jnp.float32)
    cross = jnp.dot(s_hi, w_lo, preferred_element_type=jnp.float32)
    o_ref[...] = both[:rows] + both[rows:] + cross + b_ref[...]


def _modulation(cc, w_mod, b_mod):
    depth = w_mod.shape[0]
    rows = cc.shape[0]
    tn = 1024
    return pl.pallas_call(
        _mod_kernel,
        grid=(depth, N_MOD * D // tn),
        in_specs=[
            pl.BlockSpec((rows, D), lambda l, j: (0, 0)),
            pl.BlockSpec((None, D, tn), lambda l, j: (l, 0, j)),
            pl.BlockSpec((None, 1, tn), lambda l, j: (l, 0, j)),
        ],
        out_specs=pl.BlockSpec((None, rows, tn), lambda l, j: (l, 0, j)),
        out_shape=jax.ShapeDtypeStruct((depth, rows, N_MOD * D), jnp.float32),
        compiler_params=_cparams(("arbitrary", "arbitrary")),
        name="modulation",
    )(cc, w_mod, b_mod.reshape(depth, 1, N_MOD * D))


def _ffn_kernel(x_ref, mod_ref, g_ref, wg_ref, wu_ref, wd_ref, *rest, row0, final):
    if final:
        gf_ref, o_ref = rest
    else:
        (o_ref,) = rest
    x = x_ref[...]
    shift = mod_ref[row0:row0 + 1, :]
    scale = mod_ref[row0 + 1:row0 + 2, :]
    gate = mod_ref[row0 + 2:row0 + 3, :]
    h = _modnorm(x, g_ref[...], shift, scale).astype(jnp.bfloat16)
    gt = jnp.dot(h, wg_ref[...], preferred_element_type=jnp.float32)
    up = jnp.dot(h, wu_ref[...], preferred_element_type=jnp.float32)
    a = (gt * jax.nn.sigmoid(gt) * up).astype(jnp.bfloat16)
    y = jnp.dot(a, wd_ref[...], preferred_element_type=jnp.float32)
    out = x + 0.5 * gate * y
    if final:
        ms = jnp.mean(out * out, axis=-1, keepdims=True)
        out = out * lax.rsqrt(ms + EPS) * gf_ref[...]
    o_ref[...] = out


def _ffn(xc, mods, g, wg, wu, wd, layer, row0, n_tiles, g_final=None):
    batch = xc.shape[0]
    final = g_final is not None
    in_specs = [
        pl.BlockSpec((None, TM, D), lambda b, i: (b, i, 0)),
        pl.BlockSpec((None, N_MOD, D), _mod_index),
        pl.BlockSpec((None, 1, D), lambda b, i: (layer, 0, 0)),
        _resident((None, D, D_FF_PAD), lambda b, i: (layer, 0, 0)),
        _resident((None, D, D_FF_PAD), lambda b, i: (layer, 0, 0)),
        _resident((None, D_FF_PAD, D), lambda b, i: (layer, 0, 0)),
    ]
    args = [xc, mods, g, wg, wu, wd]
    if final:
        in_specs.append(pl.BlockSpec((1, D), lambda b, i: (0, 0)))
        args.append(g_final)
    return pl.pallas_call(
        functools.partial(_ffn_kernel, row0=row0, final=final),
        grid=(batch, n_tiles),
        in_specs=in_specs,
        out_specs=pl.BlockSpec((None, TM, D), lambda b, i: (b, i, 0)),
        out_shape=jax.ShapeDtypeStruct((batch, n_tiles * TM, D), jnp.float32),
        compiler_params=_cparams(("arbitrary", "arbitrary")),
        name="ffn_final" if final else "ffn",
    )(*args)


def _even_in_kernel(x_ref, mod_ref, g_ref, win_ref, ccs_ref, gv_ref, ws_ref, bs_ref,
                    pq_ref, sgu_ref):
    x = x_ref[...]
    h = _modnorm(x, g_ref[...], mod_ref[3:4, :], mod_ref[4:5, :]).astype(jnp.bfloat16)
    p = jnp.dot(h, win_ref[...], preferred_element_type=jnp.float32)
    a = p[:, :D_FOURIER].astype(jnp.bfloat16)
    uv = jax.nn.gelu(p[:, D_FOURIER:], approximate=True)
    u = uv[:, :D_SGU]
    v = uv[:, D_SGU:]
    ccs = ccs_ref[...]
    ps, qs, gated = [], [], []
    n_chunks = TM // CHUNK
    for grp in range(N_GROUPS):
        lo, hi = grp * GROUP_CH, (grp + 1) * GROUP_CH
        t = jnp.dot(a[:, lo:hi], ccs, preferred_element_type=jnp.float32)
        ps.append(t[:, :GROUP_CH])
        qs.append(t[:, GROUP_CH:])
        vg = v[:, lo:hi]
        ms = jnp.mean(vg * vg, axis=-1, keepdims=True)
        vh = (vg * lax.rsqrt(ms + EPS) * gv_ref[:, lo:hi]).astype(jnp.bfloat16)
        rhs = jnp.concatenate([vh[c * CHUNK:(c + 1) * CHUNK, :] for c in range(n_chunks)], axis=1)
        mixed = jnp.dot(ws_ref[grp], rhs, preferred_element_type=jnp.float32)
        mixed = jnp.concatenate(
            [mixed[:, c * GROUP_CH:(c + 1) * GROUP_CH] for c in range(n_chunks)], axis=0)
        bias = jnp.concatenate([bs_ref[:, lo:hi]] * n_chunks, axis=0)
        gated.append(u[:, lo:hi] * (mixed + bias))
    pq_ref[...] = jnp.concatenate(ps + qs, axis=1).astype(jnp.bfloat16)
    sgu_ref[...] = jnp.concatenate(gated, axis=1).astype(jnp.bfloat16)


def _even_in(xc, mods, g_mix, w_in, ccs, gv, ws, bs_full, layer, j, n_tiles):
    batch = xc.shape[0]
    rows = n_tiles * TM
    return pl.pallas_call(
        _even_in_kernel,
        grid=(batch, n_tiles),
        in_specs=[
            pl.BlockSpec((None, TM, D), lambda b, i: (b, i, 0)),
            pl.BlockSpec((None, N_MOD, D), _mod_index),
            pl.BlockSpec((None, 1, D), lambda b, i: (layer, 0, 0)),
            _resident((None, D, D_FOURIER + 2 * D_SGU), lambda b, i: (j, 0, 0)),
            _resident((GROUP_CH, 2 * GROUP_CH), lambda b, i: (0, 0)),
            pl.BlockSpec((None, 1, D_SGU), lambda b, i: (j, 0, 0)),
            _resident((None, N_GROUPS, CHUNK, CHUNK), lambda b, i: (j, 0, 0, 0)),
            pl.BlockSpec((None, CHUNK, D_SGU), lambda b, i: (j, 0, 0)),
        ],
        out_specs=[
            pl.BlockSpec((None, TM, 2 * D_FOURIER), lambda b, i: (b, i, 0)),
            pl.BlockSpec((None, TM, D_SGU), lambda b, i: (b, i, 0)),
        ],
        out_shape=[
            jax.ShapeDtypeStruct((batch, rows, 2 * D_FOURIER), jnp.bfloat16),
            jax.ShapeDtypeStruct((batch, rows, D_SGU), jnp.bfloat16),
        ],
        compiler_params=_cparams(("arbitrary", "arbitrary")),
        name="even_in",
    )(xc, mods, g_mix, w_in, ccs, gv, ws, bs_full)


def _dft_kernel(c_ref, s_ref, pq_ref, o_ref):
    p = pq_ref[:, :D_FOURIER]
    q = pq_ref[:, D_FOURIER:]
    acc = jnp.dot(c_ref[...], p, preferred_element_type=jnp.float32)
    acc = acc + jnp.dot(s_ref[...], q, preferred_element_type=jnp.float32)
    o_ref[...] = acc.astype(jnp.bfloat16)


def _dft(cmat, smat, pq):
    batch, rows, _ = pq.shape
    n_tiles = rows // TM
    return pl.pallas_call(
        _dft_kernel,
        grid=(batch, n_tiles),
        in_specs=[
            pl.BlockSpec((TM, rows), lambda b, i: (i, 0)),
            pl.BlockSpec((TM, rows), lambda b, i: (i, 0)),
            pl.BlockSpec((None, rows, 2 * D_FOURIER), lambda b, i: (b, 0, 0)),
        ],
        out_specs=pl.BlockSpec((None, TM, D_FOURIER), lambda b, i: (b, i, 0)),
        out_shape=jax.ShapeDtypeStruct((batch, rows, D_FOURIER), jnp.bfloat16),
        compiler_params=_cparams(("arbitrary", "arbitrary")),
        name="dft",
    )(cmat, smat, pq)


def _out_proj_kernel(x_ref, mod_ref, w_ref, *rest):
    *y_refs, o_ref = rest
    y = jnp.concatenate([r[...] for r in y_refs], axis=1) if len(y_refs) > 1 else y_refs[0][...]
    out = jnp.dot(y, w_ref[...], preferred_element_type=jnp.float32)
    o_ref[...] = x_ref[...] + mod_ref[5:6, :] * out


def _out_proj(xc, mods, w, j, ys, n_tiles):
    batch = xc.shape[0]
    in_specs = [
        pl.BlockSpec((None, TM, D), lambda b, i: (b, i, 0)),
        pl.BlockSpec((None, N_MOD, D), _mod_index),
        _resident((None, D, D), lambda b, i: (j, 0, 0)),
    ]
    for y in ys:
        in_specs.append(pl.BlockSpec((None, TM, y.shape[-1]), lambda b, i: (b, i, 0)))
    return pl.pallas_call(
        _out_proj_kernel,
        grid=(batch, n_tiles),
        in_specs=in_specs,
        out_specs=pl.BlockSpec((None, TM, D), lambda b, i: (b, i, 0)),
        out_shape=jax.ShapeDtypeStruct((batch, n_tiles * TM, D), jnp.float32),
        compiler_params=_cparams(("arbitrary", "arbitrary")),
        name="out_proj",
    )(xc, mods, w, *ys)


def _rope(t, cos, sin_signed, first_half):
    partner = jnp.where(first_half, pltpu.roll(t, 3 * HEAD_DIM // 4, 1),
                        pltpu.roll(t, HEAD_DIM // 4, 1))
    return t * cos + partner * sin_signed


def _qkv_kernel(x_ref, mod_ref, g_ref, w_ref, gq_ref, gk_ref, cos_ref, sin_ref,
                q_ref, kt_ref, v_ref):
    x = x_ref[...]
    h = _modnorm(x, g_ref[...], mod_ref[3:4, :], mod_ref[4:5, :]).astype(jnp.bfloat16)
    qkv = jnp.dot(h, w_ref[...], preferred_element_type=jnp.float32)
    cos = cos_ref[...]
    sin = sin_ref[...]
    lane = lax.broadcasted_iota(jnp.int32, (TM, HEAD_DIM), 1)
    first_half = (lane % (HEAD_DIM // 2)) < (HEAD_DIM // 4)
    q_scale = (HEAD_DIM ** -0.5) * math.log2(math.e)

    def head(col, gain):
        t = qkv[:, col:col + HEAD_DIM]
        ms = jnp.mean(t * t, axis=-1, keepdims=True)
        return _rope(t * lax.rsqrt(ms + EPS) * gain, cos, sin, first_half)

    qs = [head(hd * HEAD_DIM, gq_ref[...]) * q_scale for hd in range(N_HEADS)]
    q_ref[...] = jnp.concatenate(qs, axis=1).astype(jnp.bfloat16)
    for hd in range(N_KV):
        kt_ref[hd] = head((N_HEADS + hd) * HEAD_DIM, gk_ref[...]).T.astype(jnp.bfloat16)
    v_ref[...] = qkv[:, (N_HEADS + N_KV) * HEAD_DIM:].astype(jnp.bfloat16)


def _qkv(xc, mods, g_mix, w_qkv, gq, gk, cos_t, sin_t, layer, j):
    batch = xc.shape[0]
    kv_w = N_KV * HEAD_DIM
    return pl.pallas_call(
        _qkv_kernel,
        grid=(batch, N_TILES),
        in_specs=[
            pl.BlockSpec((None, TM, D), lambda b, i: (b, i, 0)),
            pl.BlockSpec((None, N_MOD, D), _mod_index),
            pl.BlockSpec((None, 1, D), lambda b, i: (layer, 0, 0)),
            _resident((None, D, D + 2 * kv_w), lambda b, i: (j, 0, 0)),
            pl.BlockSpec((None, 1, HEAD_DIM), lambda b, i: (j, 0, 0)),
            pl.BlockSpec((None, 1, HEAD_DIM), lambda b, i: (j, 0, 0)),
            pl.BlockSpec((TM, HEAD_DIM), lambda b, i: (i, 0)),
            pl.BlockSpec((TM, HEAD_DIM), lambda b, i: (i, 0)),
        ],
        out_specs=[
            pl.BlockSpec((None, TM, D), lambda b, i: (b, i, 0)),
            pl.BlockSpec((None, N_KV, HEAD_DIM, TM), lambda b, i: (b, 0, 0, i)),
            pl.BlockSpec((None, TM, kv_w), lambda b, i: (b, i, 0)),
        ],
        out_shape=[
            jax.ShapeDtypeStruct((batch, ROWS, D), jnp.bfloat16),
            jax.ShapeDtypeStruct((batch, N_KV, HEAD_DIM, ROWS), jnp.bfloat16),
            jax.ShapeDtypeStruct((batch, ROWS, kv_w), jnp.bfloat16),
        ],
        compiler_params=_cparams(("arbitrary", "arbitrary")),
        name="qkv",
    )(xc, mods, g_mix, w_qkv, gq, gk, cos_t, sin_t)


def _attend(q, kt, v):
    s = jnp.dot(q, kt, preferred_element_type=jnp.float32)
    p = jnp.exp2(s - jnp.max(s, axis=-1, keepdims=True)).astype(jnp.bfloat16)
    v1 = jnp.concatenate([v, jnp.ones_like(v)], axis=1)
    pv = jnp.dot(p, v1, preferred_element_type=jnp.float32)
    return (pv[:, :HEAD_DIM] / pv[:, HEAD_DIM:]).astype(jnp.bfloat16)


def _attn_kernel(q_ref, kt_ref, v_ref, o_ref):
    i = pl.program_id(2)

    def heads(kt, v):
        outs = [_attend(q_ref[:, hd * HEAD_DIM:(hd + 1) * HEAD_DIM], kt, v)
                for hd in range(GROUP)]
        o_ref[...] = jnp.concatenate(outs, axis=1)

    @pl.when(i < N_LAT_TILES)
    def _():
        heads(kt_ref[...], v_ref[...])

    @pl.when(i == N_LAT_TILES)
    def _():
        heads(kt_ref[:, SEQ:], v_ref[SEQ:, :])


def _attention(q, kt, v, n_tiles):
    batch = q.shape[0]
    gw = GROUP * HEAD_DIM
    return pl.pallas_call(
        _attn_kernel,
        grid=(batch, N_KV, n_tiles),
        in_specs=[
            pl.BlockSpec((None, TM, gw), lambda b, g, i: (b, i, g)),
            pl.BlockSpec((None, None, HEAD_DIM, ROWS), lambda b, g, i: (b, g, 0, 0)),
            pl.BlockSpec((None, ROWS, HEAD_DIM), lambda b, g, i: (b, 0, g)),
        ],
        out_specs=pl.BlockSpec((None, TM, gw), lambda b, g, i: (b, i, g)),
        out_shape=jax.ShapeDtypeStruct((batch, n_tiles * TM, D), jnp.bfloat16),
        compiler_params=_cparams(("arbitrary", "arbitrary", "arbitrary")),
        name="attention",
    )(q, kt, v)


def _dft_matrices():
    def block(n):
        idx = jnp.arange(n, dtype=jnp.int32)
        r = (idx[:, None] * idx[None, :]) % n
        ang = r.astype(jnp.float32) * (2.0 * math.pi / n)
        s = n ** -0.5
        return jnp.cos(ang) * s, jnp.sin(ang) * s

    def diag(a, b):
        top = jnp.concatenate([a, jnp.zeros((SEQ, CTX), jnp.float32)], axis=1)
        bot = jnp.concatenate([jnp.zeros((CTX, SEQ), jnp.float32), b], axis=1)
        return jnp.concatenate([top, bot], axis=0).astype(jnp.bfloat16)

    cl, sl = block(SEQ)
    cc, sc = block(CTX)
    return diag(cl, cc), diag(sl, sc)


def _channel_dft():
    idx = jnp.arange(GROUP_CH, dtype=jnp.int32)
    r = (idx[:, None] * idx[None, :]) % GROUP_CH
    ang = r.astype(jnp.float32) * (2.0 * math.pi / GROUP_CH)
    s = GROUP_CH ** -0.5
    return jnp.concatenate([jnp.cos(ang) * s, -jnp.sin(ang) * s], axis=1).astype(jnp.bfloat16)


def _rope_tables():
    half = HEAD_DIM // 2
    rows = SEQ // GRID_W
    row = jnp.repeat(jnp.arange(rows, dtype=jnp.float32), GRID_W)
    col = jnp.tile(jnp.arange(GRID_W, dtype=jnp.float32), rows)
    inv_freq = ROPE_THETA ** (-jnp.arange(0, half, 2, dtype=jnp.float32) / half)
    ang_row = row[:, None] * inv_freq
    ang_col = col[:, None] * inv_freq
    cos = jnp.concatenate([jnp.cos(ang_row)] * 2 + [jnp.cos(ang_col)] * 2, axis=1)
    sin = jnp.concatenate([-jnp.sin(ang_row), jnp.sin(ang_row),
                           -jnp.sin(ang_col), jnp.sin(ang_col)], axis=1)
    cos = jnp.concatenate([cos, jnp.ones((CTX, HEAD_DIM), jnp.float32)], axis=0)
    sin = jnp.concatenate([sin, jnp.zeros((CTX, HEAD_DIM), jnp.float32)], axis=0)
    return cos, sin


def _split_pad_ffn(w_gu, w_down):
    pad = D_FF_PAD - D_FF
    wg = jnp.pad(w_gu[:, :, :D_FF], ((0, 0), (0, 0), (0, pad))).astype(jnp.bfloat16)
    wu = jnp.pad(w_gu[:, :, D_FF:], ((0, 0), (0, 0), (0, pad))).astype(jnp.bfloat16)
    wd = jnp.pad(w_down, ((0, 0), (0, pad), (0, 0))).astype(jnp.bfloat16)
    return wg, wu, wd


def kernel(x, c, ctx, c_ctx, w_mod, b_mod, g_ffn1, w_ffn1_gu, w_ffn1_down, g_mix, g_ffn2, w_ffn2_gu, w_ffn2_down, g_final, w_in_ab, g_v, w_s, b_s, w_out_ab, w_qkv, g_q, g_k, w_o):
    batch = x.shape[0]
    depth = w_mod.shape[0]
    assert x.shape == (batch, SEQ, D) and ctx.shape == (batch, CTX, D)
    assert depth % 2 == 0

    mod_rows = -(-(batch + 1) // 8) * 8
    cc = jnp.concatenate([c, c_ctx[None], jnp.zeros((mod_rows - batch - 1, D), jnp.float32)], axis=0)
    m = _modulation(cc, w_mod, b_mod)
    m = m[:, :batch + 1].reshape(depth, batch + 1, N_MOD, D)

    wg1, wu1, wd1 = _split_pad_ffn(w_ffn1_gu, w_ffn1_down)
    wg2, wu2, wd2 = _split_pad_ffn(w_ffn2_gu, w_ffn2_down)
    g1 = g_ffn1.reshape(depth, 1, D)
    g2 = g_ffn2.reshape(depth, 1, D)
    gm = g_mix.reshape(depth, 1, D)
    w_in = w_in_ab.astype(jnp.bfloat16)
    w_out = w_out_ab.astype(jnp.bfloat16)
    ws = w_s.astype(jnp.bfloat16)
    gv = g_v.reshape(-1, 1, D_SGU)
    bs_full = jnp.repeat(jnp.swapaxes(b_s, 1, 2), GROUP_CH, axis=2)
    wqkv = w_qkv.astype(jnp.bfloat16)
    wo = w_o.astype(jnp.bfloat16)
    gq = g_q.reshape(-1, 1, HEAD_DIM)
    gk = g_k.reshape(-1, 1, HEAD_DIM)
    cmat, smat = _dft_matrices()
    ccs = _channel_dft()
    cos_t, sin_t = _rope_tables()

    xc = jnp.concatenate([x, ctx], axis=1)
    for l in range(depth):
        last = l == depth - 1
        j = l // 2
        mods = m[l]
        xc = _ffn(xc, mods, g1, wg1, wu1, wd1, l, 0, N_TILES)
        n_out = N_LAT_TILES if last else N_TILES
        if l % 2 == 0:
            pq, sgu = _even_in(xc, mods, gm, w_in, ccs, gv, ws, bs_full, l, j, N_TILES)
            fo = _dft(cmat, smat, pq)
            xc = _out_proj(xc, mods, w_out, j, [fo, sgu], n_out)
        else:
            q, k, v = _qkv(xc, mods, gm, wqkv, gq, gk, cos_t, sin_t, l, j)
            o = _attention(q, k, v, n_out)
            xc = _out_proj(xc, mods, wo, j, [o], n_out)
        xc = _ffn(xc, mods, g2, wg2, wu2, wd2, l, 6, n_out,
                  g_final.reshape(1, D) if last else None)
    return xc
```

```python
import functools
import math

import jax
import jax.numpy as jnp
from jax import lax
from jax.experimental import pallas as pl
from jax.experimental.pallas import tpu as pltpu

D = 1024
SEQ = 4096
CTX = 256
ROWS = SEQ + CTX
GRID_W = 64
D_FF = 2752
N_MOD = 9
EPS = 1e-6
HEAD_DIM = 128
N_HEADS = 8
N_KV = 2
GROUP = N_HEADS // N_KV
KV_W = N_KV * HEAD_DIM
ROPE_THETA = 10000.0
D_FOURIER = 512
D_SGU = 512
GROUP_CH = 128
N_GROUPS = 4
CHUNK = 128

MXU_DIM = 256
TM = 256
N_LAT_TILES = SEQ // TM
N_TILES = ROWS // TM
D_FF_PAD = -(-D_FF // MXU_DIM) * MXU_DIM
VMEM_LIMIT = 48 * 1024 * 1024

assert TM == CTX and SEQ % TM == 0


def _cparams(n_axes):
    return pltpu.CompilerParams(dimension_semantics=("arbitrary",) * n_axes,
                                vmem_limit_bytes=VMEM_LIMIT)


def _resident(block_shape, index_map):
    return pl.BlockSpec(block_shape, index_map, pipeline_mode=pl.Buffered(1))


def _tile_spec(width):
    return pl.BlockSpec((None, TM, width), lambda b, i: (b, i, 0))


def _mod_spec():
    return pl.BlockSpec(
        (None, N_MOD, D),
        lambda b, i: (jnp.where(i == N_LAT_TILES, pl.num_programs(0), b), 0, 0))


def _modnorm(x, g, shift, scale):
    ms = jnp.mean(x * x, axis=-1, keepdims=True)
    y = x * lax.rsqrt(ms + EPS) * g
    return y * (1.0 + scale) + shift


def _ffn_half_step(x, mod_ref, row0, g, wg_ref, wu_ref, wd_ref):
    h = _modnorm(x, g, mod_ref[row0:row0 + 1, :], mod_ref[row0 + 1:row0 + 2, :])
    h = h.astype(jnp.bfloat16)
    gt = jnp.dot(h, wg_ref[...], preferred_element_type=jnp.float32)
    up = jnp.dot(h, wu_ref[...], preferred_element_type=jnp.float32)
    a = (gt * jax.nn.sigmoid(gt) * up).astype(jnp.bfloat16)
    y = jnp.dot(a, wd_ref[...], preferred_element_type=jnp.float32)
    return x + 0.5 * mod_ref[row0 + 2:row0 + 3, :] * y


def _mod_kernel(cc_ref, w_ref, b_ref, o_ref):
    s = cc_ref[...]
    s = s * jax.nn.sigmoid(s)
    s_hi = s.astype(jnp.bfloat16)
    s_lo = (s - s_hi.astype(jnp.float32)).astype(jnp.bfloat16)
    w = w_ref[...]
    w_hi = w.astype(jnp.bfloat16)
    w_lo = (w - w_hi.astype(jnp.float32)).astype(jnp.bfloat16)
    rows = s.shape[0]
    both = jnp.dot(jnp.concatenate([s_hi, s_lo], axis=0), w_hi,
                   preferred_element_type=jnp.float32)
    cross = jnp.dot(s_hi, w_lo, preferred_element_type=jnp.float32)
    o_ref[...] = both[:rows] + both[rows:] + cross + b_ref[...]


def _modulation(cc, w_mod, b_mod):
    depth = w_mod.shape[0]
    rows = cc.shape[0]
    tn = 1024
    return pl.pallas_call(
        _mod_kernel,
        grid=(depth, N_MOD * D // tn),
        in_specs=[
            pl.BlockSpec((rows, D), lambda l, j: (0, 0)),
            pl.BlockSpec((None, D, tn), lambda l, j: (l, 0, j)),
            pl.BlockSpec((None, 1, tn), lambda l, j: (l, 0, j)),
        ],
        out_specs=pl.BlockSpec((None, rows, tn), lambda l, j: (l, 0, j)),
        out_shape=jax.ShapeDtypeStruct((depth, rows, N_MOD * D), jnp.float32),
        compiler_params=_cparams(2),
        name="modulation",
    )(cc, w_mod, b_mod.reshape(depth, 1, N_MOD * D))


def _even_in_stage(h, win_ref, ccs_ref, gv_ref, ws_ref, bs_ref, pq_ref, sgu_ref):
    p = jnp.dot(h, win_ref[...], preferred_element_type=jnp.float32)
    a = p[:, :D_FOURIER].astype(jnp.bfloat16)
    uv = jax.nn.gelu(p[:, D_FOURIER:], approximate=True)
    u = uv[:, :D_SGU]
    v = uv[:, D_SGU:]
    ccs = ccs_ref[...]
    ps, qs, gated = [], [], []
    n_chunks = TM // CHUNK
    for grp in range(N_GROUPS):
        lo, hi = grp * GROUP_CH, (grp + 1) * GROUP_CH
        t = jnp.dot(a[:, lo:hi], ccs, preferred_element_type=jnp.float32)
        ps.append(t[:, :GROUP_CH])
        qs.append(t[:, GROUP_CH:])
        vg = v[:, lo:hi]
        ms = jnp.mean(vg * vg, axis=-1, keepdims=True)
        vh = (vg * lax.rsqrt(ms + EPS) * gv_ref[:, lo:hi]).astype(jnp.bfloat16)
        rhs = jnp.concatenate([vh[c * CHUNK:(c + 1) * CHUNK, :] for c in range(n_chunks)], axis=1)
        mixed = jnp.dot(ws_ref[grp], rhs, preferred_element_type=jnp.float32)
        mixed = jnp.concatenate(
            [mixed[:, c * GROUP_CH:(c + 1) * GROUP_CH] for c in range(n_chunks)], axis=0)
        bias = jnp.concatenate([bs_ref[:, lo:hi]] * n_chunks, axis=0)
        gated.append(u[:, lo:hi] * (mixed + bias))
    pq_ref[...] = jnp.concatenate(ps + qs, axis=1).astype(jnp.bfloat16)
    sgu_ref[...] = jnp.concatenate(gated, axis=1).astype(jnp.bfloat16)


def _rope(t, cos, sin_signed, first_half):
    partner = jnp.where(first_half, pltpu.roll(t, 3 * HEAD_DIM // 4, 1),
                        pltpu.roll(t, HEAD_DIM // 4, 1))
    return t * cos + partner * sin_signed


def _qkv_stage(h, w_ref, gq_ref, gk_ref, cos_ref, sin_ref, q_ref, kt_ref, v_ref):
    qkv = jnp.dot(h, w_ref[...], preferred_element_type=jnp.float32)
    cos = cos_ref[...]
    sin = sin_ref[...]
    lane = lax.broadcasted_iota(jnp.int32, (TM, HEAD_DIM), 1)
    first_half = (lane % (HEAD_DIM // 2)) < (HEAD_DIM // 4)
    q_scale = (HEAD_DIM ** -0.5) * math.log2(math.e)

    def head(col, gain):
        t = qkv[:, col:col + HEAD_DIM]
        ms = jnp.mean(t * t, axis=-1, keepdims=True)
        return _rope(t * lax.rsqrt(ms + EPS) * gain, cos, sin, first_half)

    qs = [head(hd * HEAD_DIM, gq_ref[...]) * q_scale for hd in range(N_HEADS)]
    q_ref[...] = jnp.concatenate(qs, axis=1).astype(jnp.bfloat16)
    for hd in range(N_KV):
        kt_ref[hd] = head((N_HEADS + hd) * HEAD_DIM, gk_ref[...]).T.astype(jnp.bfloat16)
    v_ref[...] = qkv[:, (N_HEADS + N_KV) * HEAD_DIM:].astype(jnp.bfloat16)


def _layer_in_kernel(*refs, split_in, even):
    refs = list(refs)
    if split_in:
        x_ref, ctx_ref = refs[:2]
        del refs[:2]
        x = jnp.where(pl.program_id(1) == N_LAT_TILES, ctx_ref[...], x_ref[...])
    else:
        x = refs.pop(0)[...]
    mod_ref, g1_ref, wg_ref, wu_ref, wd_ref, gm_ref = refs[:6]
    n_out = 3 if even else 4
    stage_in = refs[6:-n_out]
    xo_ref, *stage_out = refs[-n_out:]
    x = _ffn_half_step(x, mod_ref, 0, g1_ref[...], wg_ref, wu_ref, wd_ref)
    xo_ref[...] = x
    h = _modnorm(x, gm_ref[...], mod_ref[3:4, :], mod_ref[4:5, :]).astype(jnp.bfloat16)
    if even:
        _even_in_stage(h, *stage_in, *stage_out)
    else:
        _qkv_stage(h, *stage_in, *stage_out)


def _layer_in(xs, mods, g1, wg, wu, wd, gm, layer, even, stage_args):
    batch = xs[0].shape[0]
    j = layer // 2
    split_in = len(xs) == 2
    if split_in:
        x_specs = [
            pl.BlockSpec((None, TM, D), lambda b, i: (b, jnp.minimum(i, N_LAT_TILES - 1), 0)),
            pl.BlockSpec((None, CTX, D), lambda b, i: (b, 0, 0)),
        ]
    else:
        x_specs = [_tile_spec(D)]
    in_specs = x_specs + [
        _mod_spec(),
        pl.BlockSpec((None, 1, D), lambda b, i: (layer, 0, 0)),
        _resident((None, D, D_FF_PAD), lambda b, i: (layer, 0, 0)),
        _resident((None, D, D_FF_PAD), lambda b, i: (layer, 0, 0)),
        _resident((None, D_FF_PAD, D), lambda b, i: (layer, 0, 0)),
        pl.BlockSpec((None, 1, D), lambda b, i: (layer, 0, 0)),
    ]
    out_specs = [_tile_spec(D)]
    out_shape = [jax.ShapeDtypeStruct((batch, ROWS, D), jnp.float32)]
    if even:
        in_specs += [
            _resident((None, D, D_FOURIER + 2 * D_SGU), lambda b, i: (j, 0, 0)),
            _resident((GROUP_CH, 2 * GROUP_CH), lambda b, i: (0, 0)),
            pl.BlockSpec((None, 1, D_SGU), lambda b, i: (j, 0, 0)),
            _resident((None, N_GROUPS, CHUNK, CHUNK), lambda b, i: (j, 0, 0, 0)),
            pl.BlockSpec((None, CHUNK, D_SGU), lambda b, i: (j, 0, 0)),
        ]
        out_specs += [_tile_spec(2 * D_FOURIER), _tile_spec(D_SGU)]
        out_shape += [jax.ShapeDtypeStruct((batch, ROWS, 2 * D_FOURIER), jnp.bfloat16),
                      jax.ShapeDtypeStruct((batch, ROWS, D_SGU), jnp.bfloat16)]
    else:
        in_specs += [
            _resident((None, D, D + 2 * KV_W), lambda b, i: (j, 0, 0)),
            pl.BlockSpec((None, 1, HEAD_DIM), lambda b, i: (j, 0, 0)),
            pl.BlockSpec((None, 1, HEAD_DIM), lambda b, i: (j, 0, 0)),
            pl.BlockSpec((TM, HEAD_DIM), lambda b, i: (i, 0)),
            pl.BlockSpec((TM, HEAD_DIM), lambda b, i: (i, 0)),
        ]
        out_specs += [
            _tile_spec(D),
            pl.BlockSpec((None, N_KV, HEAD_DIM, TM), lambda b, i: (b, 0, 0, i)),
            _tile_spec(KV_W),
        ]
        out_shape += [jax.ShapeDtypeStruct((batch, ROWS, D), jnp.bfloat16),
                      jax.ShapeDtypeStruct((batch, N_KV, HEAD_DIM, ROWS), jnp.bfloat16),
                      jax.ShapeDtypeStruct((batch, ROWS, KV_W), jnp.bfloat16)]
    return pl.pallas_call(
        functools.partial(_layer_in_kernel, split_in=split_in, even=even),
        grid=(batch, N_TILES),
        in_specs=in_specs,
        out_specs=out_specs,
        out_shape=out_shape,
        compiler_params=_cparams(2),
        name="layer_in_even" if even else "layer_in_odd",
    )(*xs, mods, g1, wg, wu, wd, gm, *stage_args)


def _layer_out_kernel(x_ref, mod_ref, wo_ref, *refs, n_y, final):
    y_refs = refs[:n_y]
    g2_ref, wg_ref, wu_ref, wd_ref = refs[n_y:n_y + 4]
    o_ref = refs[-1]
    y = y_refs[0][...] if n_y == 1 else jnp.concatenate([r[...] for r in y_refs], axis=1)
    x = x_ref[...] + mod_ref[5:6, :] * jnp.dot(y, wo_ref[...], preferred_element_type=jnp.float32)
    x = _ffn_half_step(x, mod_ref, 6, g2_ref[...], wg_ref, wu_ref, wd_ref)
    if final:
        gf_ref = refs[-2]
        ms = jnp.mean(x * x, axis=-1, keepdims=True)
        x = x * lax.rsqrt(ms + EPS) * gf_ref[...]
    o_ref[...] = x


def _layer_out(xc, mods, wo, ys, g2, wg, wu, wd, layer, n_tiles, g_final=None):
    batch = xc.shape[0]
    j = layer // 2
    final = g_final is not None
    in_specs = [_tile_spec(D), _mod_spec(), _resident((None, D, D), lambda b, i: (j, 0, 0))]
    in_specs += [_tile_spec(y.shape[-1]) for y in ys]
    in_specs += [
        pl.BlockSpec((None, 1, D), lambda b, i: (layer, 0, 0)),
        _resident((None, D, D_FF_PAD), lambda b, i: (layer, 0, 0)),
        _resident((None, D, D_FF_PAD), lambda b, i: (layer, 0, 0)),
        _resident((None, D_FF_PAD, D), lambda b, i: (layer, 0, 0)),
    ]
    args = [xc, mods, wo, *ys, g2, wg, wu, wd]
    if final:
        in_specs.append(pl.BlockSpec((1, D), lambda b, i: (0, 0)))
        args.append(g_final)
    return pl.pallas_call(
        functools.partial(_layer_out_kernel, n_y=len(ys), final=final),
        grid=(batch, n_tiles),
        in_specs=in_specs,
        out_specs=_tile_spec(D),
        out_shape=jax.ShapeDtypeStruct((batch, n_tiles * TM, D), jnp.float32),
        compiler_params=_cparams(2),
        name="layer_out_final" if final else "layer_out",
    )(*args)


def _dft_kernel(c_ref, s_ref, pq_ref, o_ref):
    p = pq_ref[:, :D_FOURIER]
    q = pq_ref[:, D_FOURIER:]
    acc = jnp.dot(c_ref[...], p, preferred_element_type=jnp.float32)
    acc = acc + jnp.dot(s_ref[...], q, preferred_element_type=jnp.float32)
    o_ref[...] = acc.astype(jnp.bfloat16)


def _dft(cmat, smat, pq):
    batch, rows, _ = pq.shape
    return pl.pallas_call(
        _dft_kernel,
        grid=(batch, rows // TM),
        in_specs=[
            pl.BlockSpec((TM, rows), lambda b, i: (i, 0)),
            pl.BlockSpec((TM, rows), lambda b, i: (i, 0)),
            pl.BlockSpec((None, rows, 2 * D_FOURIER), lambda b, i: (b, 0, 0)),
        ],
        out_specs=_tile_spec(D_FOURIER),
        out_shape=jax.ShapeDtypeStruct((batch, rows, D_FOURIER), jnp.bfloat16),
        compiler_params=_cparams(2),
        name="dft",
    )(cmat, smat, pq)


def _attend(q, kt, v):
    s = jnp.dot(q, kt, preferred_element_type=jnp.float32)
    p = jnp.exp2(s - jnp.max(s, axis=-1, keepdims=True)).astype(jnp.bfloat16)
    v1 = jnp.concatenate([v, jnp.ones_like(v)], axis=1)
    pv = jnp.dot(p, v1, preferred_element_type=jnp.float32)
    return (pv[:, :HEAD_DIM] / pv[:, HEAD_DIM:]).astype(jnp.bfloat16)


def _attn_kernel(q_ref, kt_ref, v_ref, o_ref):
    i = pl.program_id(2)

    def heads(kt, v):
        outs = [_attend(q_ref[:, hd * HEAD_DIM:(hd + 1) * HEAD_DIM], kt, v)
                for hd in range(GROUP)]
        o_ref[...] = jnp.concatenate(outs, axis=1)

    @pl.when(i < N_LAT_TILES)
    def _():
        heads(kt_ref[...], v_ref[...])

    @pl.when(i == N_LAT_TILES)
    def _():
        heads(kt_ref[:, SEQ:], v_ref[SEQ:, :])


def _attention(q, kt, v, n_tiles):
    batch = q.shape[0]
    gw = GROUP * HEAD_DIM
    return pl.pallas_call(
        _attn_kernel,
        grid=(batch, N_KV, n_tiles),
        in_specs=[
            pl.BlockSpec((None, TM, gw), lambda b, g, i: (b, i, g)),
            pl.BlockSpec((None, None, HEAD_DIM, ROWS), lambda b, g, i: (b, g, 0, 0)),
            pl.BlockSpec((None, ROWS, HEAD_DIM), lambda b, g, i: (b, 0, g)),
        ],
        out_specs=pl.BlockSpec((None, TM, gw), lambda b, g, i: (b, i, g)),
        out_shape=jax.ShapeDtypeStruct((batch, n_tiles * TM, D), jnp.bfloat16),
        compiler_params=_cparams(3),
        name="attention",
    )(q, kt, v)


def _cos_sin_products(n):
    r = 1 << (int(math.log2(n)) // 2)
    hi = n // r
    j = jnp.arange(n, dtype=jnp.int32)[:, None]
    ang_hi = ((j * jnp.arange(hi, dtype=jnp.int32)[None, :] * r) % n).astype(jnp.float32)
    ang_lo = ((j * jnp.arange(r, dtype=jnp.int32)[None, :]) % n).astype(jnp.float32)
    w = 2.0 * math.pi / n
    ca, sa = jnp.cos(ang_hi * w)[:, :, None], jnp.sin(ang_hi * w)[:, :, None]
    cb, sb = jnp.cos(ang_lo * w)[:, None, :], jnp.sin(ang_lo * w)[:, None, :]
    scale = n ** -0.5
    cos = ((ca * cb - sa * sb) * scale).reshape(n, n)
    sin = ((sa * cb + ca * sb) * scale).reshape(n, n)
    return cos, sin


def _dft_matrices():
    def diag(a, b):
        top = jnp.concatenate([a, jnp.zeros((SEQ, CTX), jnp.float32)], axis=1)
        bot = jnp.concatenate([jnp.zeros((CTX, SEQ), jnp.float32), b], axis=1)
        return jnp.concatenate([top, bot], axis=0).astype(jnp.bfloat16)

    cl, sl = _cos_sin_products(SEQ)
    cc, sc = _cos_sin_products(CTX)
    return diag(cl, cc), diag(sl, sc)


def _channel_dft():
    c, s = _cos_sin_products(GROUP_CH)
    return jnp.concatenate([c, -s], axis=1).astype(jnp.bfloat16)


def _rope_tables():
    half = HEAD_DIM // 2
    rows = SEQ // GRID_W
    row = jnp.repeat(jnp.arange(rows, dtype=jnp.float32), GRID_W)
    col = jnp.tile(jnp.arange(GRID_W, dtype=jnp.float32), rows)
    inv_freq = ROPE_THETA ** (-jnp.arange(0, half, 2, dtype=jnp.float32) / half)
    ang_row = row[:, None] * inv_freq
    ang_col = col[:, None] * inv_freq
    cos = jnp.concatenate([jnp.cos(ang_row)] * 2 + [jnp.cos(ang_col)] * 2, axis=1)
    sin = jnp.concatenate([-jnp.sin(ang_row), jnp.sin(ang_row),
                           -jnp.sin(ang_col), jnp.sin(ang_col)], axis=1)
    cos = jnp.concatenate([cos, jnp.ones((CTX, HEAD_DIM), jnp.float32)], axis=0)
    sin = jnp.concatenate([sin, jnp.zeros((CTX, HEAD_DIM), jnp.float32)], axis=0)
    return cos, sin


def _split_pad_ffn(w_gu, w_down):
    pad = D_FF_PAD - D_FF
    wg = jnp.pad(w_gu[:, :, :D_FF], ((0, 0), (0, 0), (0, pad))).astype(jnp.bfloat16)
    wu = jnp.pad(w_gu[:, :, D_FF:], ((0, 0), (0, 0), (0, pad))).astype(jnp.bfloat16)
    wd = jnp.pad(w_down, ((0, 0), (0, pad), (0, 0))).astype(jnp.bfloat16)
    return wg, wu, wd


def kernel(x, c, ctx, c_ctx, w_mod, b_mod, g_ffn1, w_ffn1_gu, w_ffn1_down, g_mix, g_ffn2, w_ffn2_gu, w_ffn2_down, g_final, w_in_ab, g_v, w_s, b_s, w_out_ab, w_qkv, g_q, g_k, w_o):
    batch = x.shape[0]
    depth = w_mod.shape[0]
    assert x.shape == (batch, SEQ, D) and ctx.shape == (batch, CTX, D)
    assert depth % 2 == 0

    mod_rows = -(-(batch + 1) // 8) * 8
    cc = jnp.concatenate([c, c_ctx[None], jnp.zeros((mod_rows - batch - 1, D), jnp.float32)], axis=0)
    m = _modulation(cc, w_mod, b_mod)
    m = m[:, :batch + 1].reshape(depth, batch + 1, N_MOD, D)

    wg1, wu1, wd1 = _split_pad_ffn(w_ffn1_gu, w_ffn1_down)
    wg2, wu2, wd2 = _split_pad_ffn(w_ffn2_gu, w_ffn2_down)
    g1 = g_ffn1.reshape(depth, 1, D)
    g2 = g_ffn2.reshape(depth, 1, D)
    gm = g_mix.reshape(depth, 1, D)
    w_in = w_in_ab.astype(jnp.bfloat16)
    w_out = w_out_ab.astype(jnp.bfloat16)
    ws = w_s.astype(jnp.bfloat16)
    gv = g_v.reshape(-1, 1, D_SGU)
    bs_full = jnp.repeat(jnp.swapaxes(b_s, 1, 2), GROUP_CH, axis=2)
    wqkv = w_qkv.astype(jnp.bfloat16)
    wo = w_o.astype(jnp.bfloat16)
    gq = g_q.reshape(-1, 1, HEAD_DIM)
    gk = g_k.reshape(-1, 1, HEAD_DIM)
    cmat, smat = _dft_matrices()
    ccs = _channel_dft()
    cos_t, sin_t = _rope_tables()

    xs = (x, ctx)
    for l in range(depth):
        last = l == depth - 1
        even = l % 2 == 0
        mods = m[l]
        n_out = N_LAT_TILES if last else N_TILES
        if even:
            xc, pq, sgu = _layer_in(xs, mods, g1, wg1, wu1, wd1, gm, l, True,
                                    (w_in, ccs, gv, ws, bs_full))
            ys, w_proj = [_dft(cmat, smat, pq), sgu], w_out
        else:
            xc, q, kt, v = _layer_in(xs, mods, g1, wg1, wu1, wd1, gm, l, False,
                                     (wqkv, gq, gk, cos_t, sin_t))
            ys, w_proj = [_attention(q, kt, v, n_out)], wo
        xc = _layer_out(xc, mods, w_proj, ys, g2, wg2, wu2, wd2, l, n_out,
                        g_final.reshape(1, D) if last else None)
        xs = (xc,)
    return xc
```

```python
import functools
import math

import jax
import jax.numpy as jnp
from jax import lax
from jax.experimental import pallas as pl
from jax.experimental.pallas import tpu as pltpu

D = 1024
SEQ = 4096
CTX = 256
ROWS = SEQ + CTX
GRID_W = 64
D_FF = 2752
N_MOD = 9
EPS = 1e-6
HEAD_DIM = 128
N_HEADS = 8
N_KV = 2
GROUP = N_HEADS // N_KV
KV_W = N_KV * HEAD_DIM
ROPE_THETA = 10000.0
D_FOURIER = 512
D_SGU = 512
GROUP_CH = 128
N_GROUPS = 4
CHUNK = 128

MXU_DIM = 256
TM = 256
N_LAT_TILES = SEQ // TM
N_TILES = ROWS // TM
D_FF_PAD = -(-D_FF // MXU_DIM) * MXU_DIM
VMEM_LIMIT = 48 * 1024 * 1024

assert TM == CTX and SEQ % TM == 0


def _cparams(n_axes):
    return pltpu.CompilerParams(dimension_semantics=("arbitrary",) * n_axes,
                                vmem_limit_bytes=VMEM_LIMIT)


def _resident(block_shape, index_map):
    return pl.BlockSpec(block_shape, index_map, pipeline_mode=pl.Buffered(1))


def _tile_spec(width):
    return pl.BlockSpec((None, TM, width), lambda b, i: (b, i, 0))


def _mod_spec():
    return pl.BlockSpec(
        (None, N_MOD, D),
        lambda b, i: (jnp.where(i == N_LAT_TILES, pl.num_programs(0), b), 0, 0))


def _modnorm(x, g, shift, scale):
    ms = jnp.mean(x * x, axis=-1, keepdims=True)
    y = x * lax.rsqrt(ms + EPS) * g
    return y * (1.0 + scale) + shift


def _ffn_half_step(x, mod_ref, row0, g, wg_ref, wu_ref, wd_ref):
    h = _modnorm(x, g, mod_ref[row0:row0 + 1, :], mod_ref[row0 + 1:row0 + 2, :])
    h = h.astype(jnp.bfloat16)
    gt = jnp.dot(h, wg_ref[...], preferred_element_type=jnp.float32)
    up = jnp.dot(h, wu_ref[...], preferred_element_type=jnp.float32)
    a = (gt * jax.nn.sigmoid(gt) * up).astype(jnp.bfloat16)
    y = jnp.dot(a, wd_ref[...], preferred_element_type=jnp.float32)
    return x + 0.5 * mod_ref[row0 + 2:row0 + 3, :] * y


def _mod_kernel(cc_ref, w_ref, b_ref, o_ref):
    s = cc_ref[...]
    s = s * jax.nn.sigmoid(s)
    s_hi = s.astype(jnp.bfloat16)
    s_lo = (s - s_hi.astype(jnp.float32)).astype(jnp.bfloat16)
    w = w_ref[...]
    w_hi = w.astype(jnp.bfloat16)
    w_lo = (w - w_hi.astype(jnp.float32)).astype(jnp.bfloat16)
    rows = s.shape[0]
    both = jnp.dot(jnp.concatenate([s_hi, s_lo], axis=0), w_hi,
                   preferred_element_type=jnp.float32)
    cross = jnp.dot(s_hi, w_lo, preferred_element_type=jnp.float32)
    o_ref[...] = both[:rows] + both[rows:] + cross + b_ref[...]


def _modulation(cc, w_mod, b_mod):
    depth = w_mod.shape[0]
    rows = cc.shape[0]
    tn = 1024
    return pl.pallas_call(
        _mod_kernel,
        grid=(depth, N_MOD * D // tn),
        in_specs=[
            pl.BlockSpec((rows, D), lambda l, j: (0, 0)),
            pl.BlockSpec((None, D, tn), lambda l, j: (l, 0, j)),
            pl.BlockSpec((None, 1, tn), lambda l, j: (l, 0, j)),
        ],
        out_specs=pl.BlockSpec((None, rows, tn), lambda l, j: (l, 0, j)),
        out_shape=jax.ShapeDtypeStruct((depth, rows, N_MOD * D), jnp.float32),
        compiler_params=_cparams(2),
        name="modulation",
    )(cc, w_mod, b_mod.reshape(depth, 1, N_MOD * D))


def _even_in_stage(h, win_ref, ccs_ref, gv_ref, ws_ref, bs_ref, pq_ref, sgu_ref):
    p = jnp.dot(h, win_ref[...], preferred_element_type=jnp.float32)
    a = p[:, :D_FOURIER].astype(jnp.bfloat16)
    uv = jax.nn.gelu(p[:, D_FOURIER:], approximate=True)
    u = uv[:, :D_SGU]
    v = uv[:, D_SGU:]
    ccs = ccs_ref[...]
    ps, qs, gated = [], [], []
    n_chunks = TM // CHUNK
    for grp in range(N_GROUPS):
        lo, hi = grp * GROUP_CH, (grp + 1) * GROUP_CH
        t = jnp.dot(a[:, lo:hi], ccs, preferred_element_type=jnp.float32)
        ps.append(t[:, :GROUP_CH])
        qs.append(t[:, GROUP_CH:])
        vg = v[:, lo:hi]
        ms = jnp.mean(vg * vg, axis=-1, keepdims=True)
        vh = (vg * lax.rsqrt(ms + EPS) * gv_ref[:, lo:hi]).astype(jnp.bfloat16)
        rhs = jnp.concatenate([vh[c * CHUNK:(c + 1) * CHUNK, :] for c in range(n_chunks)], axis=1)
        mixed = jnp.dot(ws_ref[grp], rhs, preferred_element_type=jnp.float32)
        mixed = jnp.concatenate(
            [mixed[:, c * GROUP_CH:(c + 1) * GROUP_CH] for c in range(n_chunks)], axis=0)
        bias = jnp.concatenate([bs_ref[:, lo:hi]] * n_chunks, axis=0)
        gated.append(u[:, lo:hi] * (mixed + bias))
    pq_ref[...] = jnp.concatenate(ps + qs, axis=1).astype(jnp.bfloat16)
    sgu_ref[...] = jnp.concatenate(gated, axis=1).astype(jnp.bfloat16)


def _rope(t, cos, sin_signed, first_half):
    partner = jnp.where(first_half, pltpu.roll(t, 3 * HEAD_DIM // 4, 1),
                        pltpu.roll(t, HEAD_DIM // 4, 1))
    return t * cos + partner * sin_signed


def _qkv_stage(h, w_ref, gq_ref, gk_ref, cos_ref, sin_ref, q_ref, kt_ref, v_ref):
    qkv = jnp.dot(h, w_ref[...], preferred_element_type=jnp.float32)
    cos = cos_ref[...]
    sin = sin_ref[...]
    lane = lax.broadcasted_iota(jnp.int32, (TM, HEAD_DIM), 1)
    first_half = (lane % (HEAD_DIM // 2)) < (HEAD_DIM // 4)
    q_scale = (HEAD_DIM ** -0.5) * math.log2(math.e)

    def head(col, gain):
        t = qkv[:, col:col + HEAD_DIM]
        ms = jnp.mean(t * t, axis=-1, keepdims=True)
        return _rope(t * lax.rsqrt(ms + EPS) * gain, cos, sin, first_half)

    qs = [head(hd * HEAD_DIM, gq_ref[...]) * q_scale for hd in range(N_HEADS)]
    q_ref[...] = jnp.concatenate(qs, axis=1).astype(jnp.bfloat16)
    for hd in range(N_KV):
        kt_ref[hd] = head((N_HEADS + hd) * HEAD_DIM, gk_ref[...]).T.astype(jnp.bfloat16)
    v_ref[...] = qkv[:, (N_HEADS + N_KV) * HEAD_DIM:].astype(jnp.bfloat16)


def _layer_in_kernel(*refs, split_in, even):
    refs = list(refs)
    if split_in:
        x_ref, ctx_ref = refs[:2]
        del refs[:2]
        x = jnp.where(pl.program_id(1) == N_LAT_TILES, ctx_ref[...], x_ref[...])
    else:
        x = refs.pop(0)[...]
    mod_ref, g1_ref, wg_ref, wu_ref, wd_ref, gm_ref = refs[:6]
    n_out = 3 if even else 4
    stage_in = refs[6:-n_out]
    xo_ref, *stage_out = refs[-n_out:]
    x = _ffn_half_step(x, mod_ref, 0, g1_ref[...], wg_ref, wu_ref, wd_ref)
    xo_ref[...] = x
    h = _modnorm(x, gm_ref[...], mod_ref[3:4, :], mod_ref[4:5, :]).astype(jnp.bfloat16)
    if even:
        _even_in_stage(h, *stage_in, *stage_out)
    else:
        _qkv_stage(h, *stage_in, *stage_out)


def _layer_in(xs, mods, g1, wg, wu, wd, gm, layer, even, stage_args):
    batch = xs[0].shape[0]
    j = layer // 2
    split_in = len(xs) == 2
    if split_in:
        x_specs = [
            pl.BlockSpec((None, TM, D), lambda b, i: (b, jnp.minimum(i, N_LAT_TILES - 1), 0)),
            pl.BlockSpec((None, CTX, D), lambda b, i: (b, 0, 0)),
        ]
    else:
        x_specs = [_tile_spec(D)]
    in_specs = x_specs + [
        _mod_spec(),
        pl.BlockSpec((None, 1, D), lambda b, i: (layer, 0, 0)),
        _resident((None, D, D_FF_PAD), lambda b, i: (layer, 0, 0)),
        _resident((None, D, D_FF_PAD), lambda b, i: (layer, 0, 0)),
        _resident((None, D_FF_PAD, D), lambda b, i: (layer, 0, 0)),
        pl.BlockSpec((None, 1, D), lambda b, i: (layer, 0, 0)),
    ]
    out_specs = [_tile_spec(D)]
    out_shape = [jax.ShapeDtypeStruct((batch, ROWS, D), jnp.float32)]
    if even:
        in_specs += [
            _resident((None, D, D_FOURIER + 2 * D_SGU), lambda b, i: (j, 0, 0)),
            _resident((GROUP_CH, 2 * GROUP_CH), lambda b, i: (0, 0)),
            pl.BlockSpec((None, 1, D_SGU), lambda b, i: (j, 0, 0)),
            _resident((None, N_GROUPS, CHUNK, CHUNK), lambda b, i: (j, 0, 0, 0)),
            pl.BlockSpec((None, CHUNK, D_SGU), lambda b, i: (j, 0, 0)),
        ]
        out_specs += [_tile_spec(2 * D_FOURIER), _tile_spec(D_SGU)]
        out_shape += [jax.ShapeDtypeStruct((batch, ROWS, 2 * D_FOURIER), jnp.bfloat16),
                      jax.ShapeDtypeStruct((batch, ROWS, D_SGU), jnp.bfloat16)]
    else:
        in_specs += [
            _resident((None, D, D + 2 * KV_W), lambda b, i: (j, 0, 0)),
            pl.BlockSpec((None, 1, HEAD_DIM), lambda b, i: (j, 0, 0)),
            pl.BlockSpec((None, 1, HEAD_DIM), lambda b, i: (j, 0, 0)),
            pl.BlockSpec((TM, HEAD_DIM), lambda b, i: (i, 0)),
            pl.BlockSpec((TM, HEAD_DIM), lambda b, i: (i, 0)),
        ]
        out_specs += [
            _tile_spec(D),
            pl.BlockSpec((None, N_KV, HEAD_DIM, TM), lambda b, i: (b, 0, 0, i)),
            _tile_spec(KV_W),
        ]
        out_shape += [jax.ShapeDtypeStruct((batch, ROWS, D), jnp.bfloat16),
                      jax.ShapeDtypeStruct((batch, N_KV, HEAD_DIM, ROWS), jnp.bfloat16),
                      jax.ShapeDtypeStruct((batch, ROWS, KV_W), jnp.bfloat16)]
    return pl.pallas_call(
        functools.partial(_layer_in_kernel, split_in=split_in, even=even),
        grid=(batch, N_TILES),
        in_specs=in_specs,
        out_specs=out_specs,
        out_shape=out_shape,
        compiler_params=_cparams(2),
        name="layer_in_even" if even else "layer_in_odd",
    )(*xs, mods, g1, wg, wu, wd, gm, *stage_args)


def _layer_out_kernel(x_ref, mod_ref, wo_ref, *refs, n_y, split_y, final):
    y_refs = refs[:n_y]
    g2_ref, wg_ref, wu_ref, wd_ref = refs[n_y:n_y + 4]
    o_ref = refs[-1]
    if split_y:
        y = jnp.where(pl.program_id(1) == N_LAT_TILES, y_refs[1][...], y_refs[0][...])
    else:
        y = y_refs[0][...] if n_y == 1 else jnp.concatenate([r[...] for r in y_refs], axis=1)
    x = x_ref[...] + mod_ref[5:6, :] * jnp.dot(y, wo_ref[...], preferred_element_type=jnp.float32)
    x = _ffn_half_step(x, mod_ref, 6, g2_ref[...], wg_ref, wu_ref, wd_ref)
    if final:
        gf_ref = refs[-2]
        ms = jnp.mean(x * x, axis=-1, keepdims=True)
        x = x * lax.rsqrt(ms + EPS) * gf_ref[...]
    o_ref[...] = x


def _layer_out(xc, mods, wo, ys, g2, wg, wu, wd, layer, n_tiles, g_final=None, split_y=False):
    batch = xc.shape[0]
    j = layer // 2
    final = g_final is not None
    in_specs = [_tile_spec(D), _mod_spec(), _resident((None, D, D), lambda b, i: (j, 0, 0))]
    if split_y:
        in_specs += [
            pl.BlockSpec((None, TM, D), lambda b, i: (b, jnp.minimum(i, N_LAT_TILES - 1), 0)),
            pl.BlockSpec((None, CTX, D), lambda b, i: (b, 0, 0)),
        ]
    else:
        in_specs += [_tile_spec(y.shape[-1]) for y in ys]
    in_specs += [
        pl.BlockSpec((None, 1, D), lambda b, i: (layer, 0, 0)),
        _resident((None, D, D_FF_PAD), lambda b, i: (layer, 0, 0)),
        _resident((None, D, D_FF_PAD), lambda b, i: (layer, 0, 0)),
        _resident((None, D_FF_PAD, D), lambda b, i: (layer, 0, 0)),
    ]
    args = [xc, mods, wo, *ys, g2, wg, wu, wd]
    if final:
        in_specs.append(pl.BlockSpec((1, D), lambda b, i: (0, 0)))
        args.append(g_final)
    return pl.pallas_call(
        functools.partial(_layer_out_kernel, n_y=len(ys), split_y=split_y, final=final),
        grid=(batch, n_tiles),
        in_specs=in_specs,
        out_specs=_tile_spec(D),
        out_shape=jax.ShapeDtypeStruct((batch, n_tiles * TM, D), jnp.float32),
        compiler_params=_cparams(2),
        name="layer_out_final" if final else "layer_out",
    )(*args)


def _dft_kernel(c_ref, s_ref, pq_ref, o_ref):
    p = pq_ref[:, :D_FOURIER]
    q = pq_ref[:, D_FOURIER:]
    acc = jnp.dot(c_ref[...], p, preferred_element_type=jnp.float32)
    acc = acc + jnp.dot(s_ref[...], q, preferred_element_type=jnp.float32)
    o_ref[...] = acc.astype(jnp.bfloat16)


def _dft(cmat, smat, pq):
    batch, rows, _ = pq.shape
    return pl.pallas_call(
        _dft_kernel,
        grid=(batch, rows // TM),
        in_specs=[
            pl.BlockSpec((TM, rows), lambda b, i: (i, 0)),
            pl.BlockSpec((TM, rows), lambda b, i: (i, 0)),
            pl.BlockSpec((None, rows, 2 * D_FOURIER), lambda b, i: (b, 0, 0)),
        ],
        out_specs=_tile_spec(D_FOURIER),
        out_shape=jax.ShapeDtypeStruct((batch, rows, D_FOURIER), jnp.bfloat16),
        compiler_params=_cparams(2),
        name="dft",
    )(cmat, smat, pq)


KEY_TILE = MXU_DIM
N_KEY_TILES = ROWS // KEY_TILE


def _with_row_sums(v):
    return jnp.concatenate([v, jnp.ones_like(v)], axis=1)


def _normalised(pv):
    return (pv[:, :HEAD_DIM] / pv[:, HEAD_DIM:]).astype(jnp.bfloat16)


def _attn_kernel(q_ref, kt_ref, v_ref, o_ref, s_ref, m_ref):
    @pl.when(pl.program_id(0) == 0)
    def _():
        s_ref[...] = jnp.zeros(s_ref.shape, s_ref.dtype)
        m_ref[...] = jnp.zeros(m_ref.shape, m_ref.dtype)

    outs = []
    for hd in range(GROUP):
        q = q_ref[:, hd * HEAD_DIM:(hd + 1) * HEAD_DIM]
        m_prev = m_ref[hd]
        m_lane = None
        acc = None
        for j in range(N_KEY_TILES):
            keys = slice(j * KEY_TILE, (j + 1) * KEY_TILE)
            s_old = s_ref[hd, :, keys]
            p = jnp.exp2(s_old - jnp.concatenate([m_prev] * (KEY_TILE // HEAD_DIM), axis=1))
            pv = jnp.dot(p.astype(jnp.bfloat16), _with_row_sums(v_ref[keys, :]),
                         preferred_element_type=jnp.float32)
            acc = pv if acc is None else acc + pv
            s_new = jnp.dot(q, kt_ref[:, keys], preferred_element_type=jnp.float32)
            s_ref[hd, :, keys] = s_new
            for c in range(KEY_TILE // HEAD_DIM):
                part = s_new[:, c * HEAD_DIM:(c + 1) * HEAD_DIM]
                m_lane = part if m_lane is None else jnp.maximum(m_lane, part)
        outs.append(_normalised(acc))
        m_ref[hd] = jnp.broadcast_to(jnp.max(m_lane, axis=-1, keepdims=True), m_lane.shape)
    o_ref[...] = jnp.concatenate(outs, axis=1)


def _attention_latent(q, kt, v):
    batch = q.shape[0]
    gw = GROUP * HEAD_DIM
    per_sample = N_KV * N_LAT_TILES
    steps = batch * per_sample + 1

    def unflatten(c):
        return c // per_sample, (c % per_sample) // N_LAT_TILES, c % N_LAT_TILES

    def cur(t):
        return unflatten(jnp.minimum(t, steps - 2))

    def prev(t):
        return unflatten(jnp.maximum(t - 1, 0))

    def q_map(t):
        b, g, i = cur(t)
        return b, i, g

    def kt_map(t):
        b, g, _ = cur(t)
        return b, g, 0, 0

    def v_map(t):
        b, g, _ = prev(t)
        return b, 0, g

    def o_map(t):
        b, g, i = prev(t)
        return b, i, g

    return pl.pallas_call(
        _attn_kernel,
        grid=(steps,),
        in_specs=[
            pl.BlockSpec((None, TM, gw), q_map),
            pl.BlockSpec((None, None, HEAD_DIM, ROWS), kt_map),
            pl.BlockSpec((None, ROWS, HEAD_DIM), v_map),
        ],
        out_specs=pl.BlockSpec((None, TM, gw), o_map),
        out_shape=jax.ShapeDtypeStruct((batch, SEQ, D), jnp.bfloat16),
        scratch_shapes=[pltpu.VMEM((GROUP, TM, ROWS), jnp.float32),
                        pltpu.VMEM((GROUP, TM, HEAD_DIM), jnp.float32)],
        compiler_params=_cparams(1),
        name="attention",
    )(q, kt, v)


def _attn_ctx_kernel(q_ref, kt_ref, v_ref, o_ref):
    v1 = _with_row_sums(v_ref[...])
    kt = kt_ref[...]
    outs = []
    for hd in range(GROUP):
        s = jnp.dot(q_ref[:, hd * HEAD_DIM:(hd + 1) * HEAD_DIM], kt,
                    preferred_element_type=jnp.float32)
        p = jnp.exp2(s - jnp.max(s, axis=-1, keepdims=True)).astype(jnp.bfloat16)
        outs.append(_normalised(jnp.dot(p, v1, preferred_element_type=jnp.float32)))
    o_ref[...] = jnp.concatenate(outs, axis=1)


def _attention_context(q, kt, v):
    batch = q.shape[0]
    gw = GROUP * HEAD_DIM
    return pl.pallas_call(
        _attn_ctx_kernel,
        grid=(batch, N_KV),
        in_specs=[
            pl.BlockSpec((None, CTX, gw), lambda b, g: (b, SEQ // CTX, g)),
            pl.BlockSpec((None, None, HEAD_DIM, CTX), lambda b, g: (b, g, 0, SEQ // CTX)),
            pl.BlockSpec((None, CTX, HEAD_DIM), lambda b, g: (b, SEQ // CTX, g)),
        ],
        out_specs=pl.BlockSpec((None, CTX, gw), lambda b, g: (b, 0, g)),
        out_shape=jax.ShapeDtypeStruct((batch, CTX, D), jnp.bfloat16),
        compiler_params=_cparams(2),
        name="attention_ctx",
    )(q, kt, v)


def _cos_sin_products(n):
    r = 1 << (int(math.log2(n)) // 2)
    hi = n // r
    j = jnp.arange(n, dtype=jnp.int32)[:, None]
    ang_hi = ((j * jnp.arange(hi, dtype=jnp.int32)[None, :] * r) % n).astype(jnp.float32)
    ang_lo = ((j * jnp.arange(r, dtype=jnp.int32)[None, :]) % n).astype(jnp.float32)
    w = 2.0 * math.pi / n
    ca, sa = jnp.cos(ang_hi * w)[:, :, None], jnp.sin(ang_hi * w)[:, :, None]
    cb, sb = jnp.cos(ang_lo * w)[:, None, :], jnp.sin(ang_lo * w)[:, None, :]
    scale = n ** -0.5
    cos = ((ca * cb - sa * sb) * scale).reshape(n, n)
    sin = ((sa * cb + ca * sb) * scale).reshape(n, n)
    return cos, sin


def _dft_matrices():
    def diag(a, b):
        top = jnp.concatenate([a, jnp.zeros((SEQ, CTX), jnp.float32)], axis=1)
        bot = jnp.concatenate([jnp.zeros((CTX, SEQ), jnp.float32), b], axis=1)
        return jnp.concatenate([top, bot], axis=0).astype(jnp.bfloat16)

    cl, sl = _cos_sin_products(SEQ)
    cc, sc = _cos_sin_products(CTX)
    return diag(cl, cc), diag(sl, sc)


def _channel_dft():
    c, s = _cos_sin_products(GROUP_CH)
    return jnp.concatenate([c, -s], axis=1).astype(jnp.bfloat16)


def _rope_tables():
    half = HEAD_DIM // 2
    rows = SEQ // GRID_W
    row = jnp.repeat(jnp.arange(rows, dtype=jnp.float32), GRID_W)
    col = jnp.tile(jnp.arange(GRID_W, dtype=jnp.float32), rows)
    inv_freq = ROPE_THETA ** (-jnp.arange(0, half, 2, dtype=jnp.float32) / half)
    ang_row = row[:, None] * inv_freq
    ang_col = col[:, None] * inv_freq
    cos = jnp.concatenate([jnp.cos(ang_row)] * 2 + [jnp.cos(ang_col)] * 2, axis=1)
    sin = jnp.concatenate([-jnp.sin(ang_row), jnp.sin(ang_row),
                           -jnp.sin(ang_col), jnp.sin(ang_col)], axis=1)
    cos = jnp.concatenate([cos, jnp.ones((CTX, HEAD_DIM), jnp.float32)], axis=0)
    sin = jnp.concatenate([sin, jnp.zeros((CTX, HEAD_DIM), jnp.float32)], axis=0)
    return cos, sin


def _split_pad_ffn(w_gu, w_down):
    pad = D_FF_PAD - D_FF
    wg = jnp.pad(w_gu[:, :, :D_FF], ((0, 0), (0, 0), (0, pad))).astype(jnp.bfloat16)
    wu = jnp.pad(w_gu[:, :, D_FF:], ((0, 0), (0, 0), (0, pad))).astype(jnp.bfloat16)
    wd = jnp.pad(w_down, ((0, 0), (0, pad), (0, 0))).astype(jnp.bfloat16)
    return wg, wu, wd


def kernel(x, c, ctx, c_ctx, w_mod, b_mod, g_ffn1, w_ffn1_gu, w_ffn1_down, g_mix, g_ffn2, w_ffn2_gu, w_ffn2_down, g_final, w_in_ab, g_v, w_s, b_s, w_out_ab, w_qkv, g_q, g_k, w_o):
    batch = x.shape[0]
    depth = w_mod.shape[0]
    assert x.shape == (batch, SEQ, D) and ctx.shape == (batch, CTX, D)
    assert depth % 2 == 0

    mod_rows = -(-(batch + 1) // 8) * 8
    cc = jnp.concatenate([c, c_ctx[None], jnp.zeros((mod_rows - batch - 1, D), jnp.float32)], axis=0)
    m = _modulation(cc, w_mod, b_mod)
    m = m[:, :batch + 1].reshape(depth, batch + 1, N_MOD, D)

    wg1, wu1, wd1 = _split_pad_ffn(w_ffn1_gu, w_ffn1_down)
    wg2, wu2, wd2 = _split_pad_ffn(w_ffn2_gu, w_ffn2_down)
    g1 = g_ffn1.reshape(depth, 1, D)
    g2 = g_ffn2.reshape(depth, 1, D)
    gm = g_mix.reshape(depth, 1, D)
    w_in = w_in_ab.astype(jnp.bfloat16)
    w_out = w_out_ab.astype(jnp.bfloat16)
    ws = w_s.astype(jnp.bfloat16)
    gv = g_v.reshape(-1, 1, D_SGU)
    bs_full = jnp.repeat(jnp.swapaxes(b_s, 1, 2), GROUP_CH, axis=2)
    wqkv = w_qkv.astype(jnp.bfloat16)
    wo = w_o.astype(jnp.bfloat16)
    gq = g_q.reshape(-1, 1, HEAD_DIM)
    gk = g_k.reshape(-1, 1, HEAD_DIM)
    cmat, smat = _dft_matrices()
    ccs = _channel_dft()
    cos_t, sin_t = _rope_tables()

    xs = (x, ctx)
    for l in range(depth):
        last = l == depth - 1
        even = l % 2 == 0
        mods = m[l]
        n_out = N_LAT_TILES if last else N_TILES
        if even:
            xc, pq, sgu = _layer_in(xs, mods, g1, wg1, wu1, wd1, gm, l, True,
                                    (w_in, ccs, gv, ws, bs_full))
            ys, w_proj = [_dft(cmat, smat, pq), sgu], w_out
        else:
            xc, q, kt, v = _layer_in(xs, mods, g1, wg1, wu1, wd1, gm, l, False,
                                     (wqkv, gq, gk, cos_t, sin_t))
            ys, w_proj = [_attention_latent(q, kt, v)], wo
            if not last:
                ys.append(_attention_context(q, kt, v))
        xc = _layer_out(xc, mods, w_proj, ys, g2, wg2, wu2, wd2, l, n_out,
                        g_final.reshape(1, D) if last else None,
                        split_y=not even and not last)
        xs = (xc,)
    return xc
```

```python
import functools
import math

import jax
import jax.numpy as jnp
from jax import lax
from jax.experimental import pallas as pl
from jax.experimental.pallas import tpu as pltpu

D = 1024
SEQ = 4096
CTX = 256
ROWS = SEQ + CTX
GRID_W = 64
D_FF = 2752
N_MOD = 9
EPS = 1e-6
HEAD_DIM = 128
N_HEADS = 8
N_KV = 2
GROUP = N_HEADS // N_KV
KV_W = N_KV * HEAD_DIM
ROPE_THETA = 10000.0
D_FOURIER = 512
D_SGU = 512
GROUP_CH = 128
N_GROUPS = 4
CHUNK = 128

MXU_DIM = 256
TM = 256
N_LAT_TILES = SEQ // TM
N_TILES = ROWS // TM
D_FF_PAD = -(-D_FF // MXU_DIM) * MXU_DIM
VMEM_LIMIT = 48 * 1024 * 1024

assert TM == CTX and SEQ % TM == 0


def _cparams(n_axes):
    return pltpu.CompilerParams(dimension_semantics=("arbitrary",) * n_axes,
                                vmem_limit_bytes=VMEM_LIMIT)


def _resident(block_shape, index_map):
    return pl.BlockSpec(block_shape, index_map, pipeline_mode=pl.Buffered(1))


def _tile_spec(width):
    return pl.BlockSpec((None, TM, width), lambda b, i: (b, i, 0))


def _mod_spec():
    return pl.BlockSpec(
        (None, N_MOD, D),
        lambda b, i: (jnp.where(i == N_LAT_TILES, pl.num_programs(0), b), 0, 0))


def _modnorm(x, g, shift, scale):
    ms = jnp.mean(x * x, axis=-1, keepdims=True)
    y = x * lax.rsqrt(ms + EPS) * g
    return y * (1.0 + scale) + shift


def _ffn_half_step(x, mod_ref, row0, g, wgu_ref, wd_ref):
    h = _modnorm(x, g, mod_ref[row0:row0 + 1, :], mod_ref[row0 + 1:row0 + 2, :])
    h = h.astype(jnp.bfloat16)
    gu = jnp.dot(h, wgu_ref[...], preferred_element_type=jnp.float32)
    gt, up = gu[:, :D_FF_PAD], gu[:, D_FF_PAD:]
    a = (gt * jax.nn.sigmoid(gt) * up).astype(jnp.bfloat16)
    y = jnp.dot(a, wd_ref[...], preferred_element_type=jnp.float32)
    return x + 0.5 * mod_ref[row0 + 2:row0 + 3, :] * y


def _mod_kernel(cc_ref, w_ref, b_ref, o_ref):
    s = cc_ref[...]
    s = s * jax.nn.sigmoid(s)
    s_hi = s.astype(jnp.bfloat16)
    s_lo = (s - s_hi.astype(jnp.float32)).astype(jnp.bfloat16)
    w = w_ref[...]
    w_hi = w.astype(jnp.bfloat16)
    w_lo = (w - w_hi.astype(jnp.float32)).astype(jnp.bfloat16)
    rows = s.shape[0]
    both = jnp.dot(jnp.concatenate([s_hi, s_lo], axis=0), w_hi,
                   preferred_element_type=jnp.float32)
    cross = jnp.dot(s_hi, w_lo, preferred_element_type=jnp.float32)
    o_ref[...] = both[:rows] + both[rows:] + cross + b_ref[...]


def _modulation(cc, w_mod, b_mod):
    depth = w_mod.shape[0]
    rows = cc.shape[0]
    tn = 1024
    return pl.pallas_call(
        _mod_kernel,
        grid=(depth, N_MOD * D // tn),
        in_specs=[
            pl.BlockSpec((rows, D), lambda l, j: (0, 0)),
            pl.BlockSpec((None, D, tn), lambda l, j: (l, 0, j)),
            pl.BlockSpec((None, 1, tn), lambda l, j: (l, 0, j)),
        ],
        out_specs=pl.BlockSpec((None, rows, tn), lambda l, j: (l, 0, j)),
        out_shape=jax.ShapeDtypeStruct((depth, rows, N_MOD * D), jnp.float32),
        compiler_params=_cparams(2),
        name="modulation",
    )(cc, w_mod, b_mod.reshape(depth, 1, N_MOD * D))


def _even_in_stage(h, win_ref, ccs_ref, gv_ref, ws_ref, bs_ref, pq_ref, sgu_ref):
    p = jnp.dot(h, win_ref[...], preferred_element_type=jnp.float32)
    a = p[:, :D_FOURIER].astype(jnp.bfloat16)
    uv = jax.nn.gelu(p[:, D_FOURIER:], approximate=True)
    u = uv[:, :D_SGU]
    v = uv[:, D_SGU:]
    ccs = ccs_ref[...]
    ps, qs, gated = [], [], []
    n_chunks = TM // CHUNK
    for grp in range(N_GROUPS):
        lo, hi = grp * GROUP_CH, (grp + 1) * GROUP_CH
        t = jnp.dot(a[:, lo:hi], ccs, preferred_element_type=jnp.float32)
        ps.append(t[:, :GROUP_CH])
        qs.append(t[:, GROUP_CH:])
        vg = v[:, lo:hi]
        ms = jnp.mean(vg * vg, axis=-1, keepdims=True)
        vh = (vg * lax.rsqrt(ms + EPS) * gv_ref[:, lo:hi]).astype(jnp.bfloat16)
        rhs = jnp.concatenate([vh[c * CHUNK:(c + 1) * CHUNK, :] for c in range(n_chunks)], axis=1)
        mixed = jnp.dot(ws_ref[grp], rhs, preferred_element_type=jnp.float32)
        mixed = jnp.concatenate(
            [mixed[:, c * GROUP_CH:(c + 1) * GROUP_CH] for c in range(n_chunks)], axis=0)
        bias = jnp.concatenate([bs_ref[:, lo:hi]] * n_chunks, axis=0)
        gated.append(u[:, lo:hi] * (mixed + bias))
    pq_ref[...] = jnp.concatenate(ps + qs, axis=1).astype(jnp.bfloat16)
    sgu_ref[...] = jnp.concatenate(gated, axis=1).astype(jnp.bfloat16)


def _rope(t, cos, sin_signed, first_half):
    partner = jnp.where(first_half, pltpu.roll(t, 3 * HEAD_DIM // 4, 1),
                        pltpu.roll(t, HEAD_DIM // 4, 1))
    return t * cos + partner * sin_signed


def _qkv_stage(h, w_ref, gq_ref, gk_ref, cos_ref, sin_ref, q_ref, kt_ref, v_ref):
    qkv = jnp.dot(h, w_ref[...], preferred_element_type=jnp.float32)
    cos = cos_ref[...]
    sin = sin_ref[...]
    lane = lax.broadcasted_iota(jnp.int32, (TM, HEAD_DIM), 1)
    first_half = (lane % (HEAD_DIM // 2)) < (HEAD_DIM // 4)
    q_scale = (HEAD_DIM ** -0.5) * math.log2(math.e)

    def head(col, gain):
        t = qkv[:, col:col + HEAD_DIM]
        ms = jnp.mean(t * t, axis=-1, keepdims=True)
        return _rope(t * lax.rsqrt(ms + EPS) * gain, cos, sin, first_half)

    qs = [head(hd * HEAD_DIM, gq_ref[...]) * q_scale for hd in range(N_HEADS)]
    q_ref[...] = jnp.concatenate(qs, axis=1).astype(jnp.bfloat16)
    for hd in range(N_KV):
        kt_ref[hd] = head((N_HEADS + hd) * HEAD_DIM, gk_ref[...]).T.astype(jnp.bfloat16)
    v_ref[...] = qkv[:, (N_HEADS + N_KV) * HEAD_DIM:].astype(jnp.bfloat16)


def _layer_in_kernel(*refs, split_in, even):
    refs = list(refs)
    if split_in:
        x_ref, ctx_ref = refs[:2]
        del refs[:2]
        x = jnp.where(pl.program_id(1) == N_LAT_TILES, ctx_ref[...], x_ref[...])
    else:
        x = refs.pop(0)[...]
    mod_ref, g1_ref, wgu_ref, wd_ref, gm_ref = refs[:5]
    n_out = 3 if even else 4
    stage_in = refs[5:-n_out]
    xo_ref, *stage_out = refs[-n_out:]
    x = _ffn_half_step(x, mod_ref, 0, g1_ref[...], wgu_ref, wd_ref)
    xo_ref[...] = x
    h = _modnorm(x, gm_ref[...], mod_ref[3:4, :], mod_ref[4:5, :]).astype(jnp.bfloat16)
    if even:
        _even_in_stage(h, *stage_in, *stage_out)
    else:
        _qkv_stage(h, *stage_in, *stage_out)


def _layer_in(xs, mods, g1, wgu, wd, gm, layer, even, stage_args):
    batch = xs[0].shape[0]
    j = layer // 2
    split_in = len(xs) == 2
    if split_in:
        x_specs = [
            pl.BlockSpec((None, TM, D), lambda b, i: (b, jnp.minimum(i, N_LAT_TILES - 1), 0)),
            pl.BlockSpec((None, CTX, D), lambda b, i: (b, 0, 0)),
        ]
    else:
        x_specs = [_tile_spec(D)]
    in_specs = x_specs + [
        _mod_spec(),
        pl.BlockSpec((None, 1, D), lambda b, i: (layer, 0, 0)),
        _resident((None, D, 2 * D_FF_PAD), lambda b, i: (layer, 0, 0)),
        _resident((None, D_FF_PAD, D), lambda b, i: (layer, 0, 0)),
        pl.BlockSpec((None, 1, D), lambda b, i: (layer, 0, 0)),
    ]
    out_specs = [_tile_spec(D)]
    out_shape = [jax.ShapeDtypeStruct((batch, ROWS, D), jnp.float32)]
    if even:
        in_specs += [
            _resident((None, D, D_FOURIER + 2 * D_SGU), lambda b, i: (j, 0, 0)),
            _resident((GROUP_CH, 2 * GROUP_CH), lambda b, i: (0, 0)),
            pl.BlockSpec((None, 1, D_SGU), lambda b, i: (j, 0, 0)),
            _resident((None, N_GROUPS, CHUNK, CHUNK), lambda b, i: (j, 0, 0, 0)),
            pl.BlockSpec((None, CHUNK, D_SGU), lambda b, i: (j, 0, 0)),
        ]
        out_specs += [_tile_spec(2 * D_FOURIER), _tile_spec(D_SGU)]
        out_shape += [jax.ShapeDtypeStruct((batch, ROWS, 2 * D_FOURIER), jnp.bfloat16),
                      jax.ShapeDtypeStruct((batch, ROWS, D_SGU), jnp.bfloat16)]
    else:
        in_specs += [
            _resident((None, D, D + 2 * KV_W), lambda b, i: (j, 0, 0)),
            pl.BlockSpec((None, 1, HEAD_DIM), lambda b, i: (j, 0, 0)),
            pl.BlockSpec((None, 1, HEAD_DIM), lambda b, i: (j, 0, 0)),
            pl.BlockSpec((TM, HEAD_DIM), lambda b, i: (i, 0)),
            pl.BlockSpec((TM, HEAD_DIM), lambda b, i: (i, 0)),
        ]
        out_specs += [
            _tile_spec(D),
            pl.BlockSpec((None, N_KV, HEAD_DIM, TM), lambda b, i: (b, 0, 0, i)),
            _tile_spec(KV_W),
        ]
        out_shape += [jax.ShapeDtypeStruct((batch, ROWS, D), jnp.bfloat16),
                      jax.ShapeDtypeStruct((batch, N_KV, HEAD_DIM, ROWS), jnp.bfloat16),
                      jax.ShapeDtypeStruct((batch, ROWS, KV_W), jnp.bfloat16)]
    return pl.pallas_call(
        functools.partial(_layer_in_kernel, split_in=split_in, even=even),
        grid=(batch, N_TILES),
        in_specs=in_specs,
        out_specs=out_specs,
        out_shape=out_shape,
        compiler_params=_cparams(2),
        name="layer_in_even" if even else "layer_in_odd",
    )(*xs, mods, g1, wgu, wd, gm, *stage_args)


def _layer_out_kernel(x_ref, mod_ref, wo_ref, *refs, n_y, split_y, final):
    y_refs = refs[:n_y]
    g2_ref, wgu_ref, wd_ref = refs[n_y:n_y + 3]
    o_ref = refs[-1]
    if split_y:
        y = jnp.where(pl.program_id(1) == N_LAT_TILES, y_refs[1][...], y_refs[0][...])
    else:
        y = y_refs[0][...] if n_y == 1 else jnp.concatenate([r[...] for r in y_refs], axis=1)
    x = x_ref[...] + mod_ref[5:6, :] * jnp.dot(y, wo_ref[...], preferred_element_type=jnp.float32)
    x = _ffn_half_step(x, mod_ref, 6, g2_ref[...], wgu_ref, wd_ref)
    if final:
        gf_ref = refs[-2]
        ms = jnp.mean(x * x, axis=-1, keepdims=True)
        x = x * lax.rsqrt(ms + EPS) * gf_ref[...]
    o_ref[...] = x


def _layer_out(xc, mods, wo, ys, g2, wgu, wd, layer, n_tiles, g_final=None, split_y=False):
    batch = xc.shape[0]
    j = layer // 2
    final = g_final is not None
    in_specs = [_tile_spec(D), _mod_spec(), _resident((None, D, D), lambda b, i: (j, 0, 0))]
    if split_y:
        in_specs += [
            pl.BlockSpec((None, TM, D), lambda b, i: (b, jnp.minimum(i, N_LAT_TILES - 1), 0)),
            pl.BlockSpec((None, CTX, D), lambda b, i: (b, 0, 0)),
        ]
    else:
        in_specs += [_tile_spec(y.shape[-1]) for y in ys]
    in_specs += [
        pl.BlockSpec((None, 1, D), lambda b, i: (layer, 0, 0)),
        _resident((None, D, 2 * D_FF_PAD), lambda b, i: (layer, 0, 0)),
        _resident((None, D_FF_PAD, D), lambda b, i: (layer, 0, 0)),
    ]
    args = [xc, mods, wo, *ys, g2, wgu, wd]
    if final:
        in_specs.append(pl.BlockSpec((1, D), lambda b, i: (0, 0)))
        args.append(g_final)
    return pl.pallas_call(
        functools.partial(_layer_out_kernel, n_y=len(ys), split_y=split_y, final=final),
        grid=(batch, n_tiles),
        in_specs=in_specs,
        out_specs=_tile_spec(D),
        out_shape=jax.ShapeDtypeStruct((batch, n_tiles * TM, D), jnp.float32),
        compiler_params=_cparams(2),
        name="layer_out_final" if final else "layer_out",
    )(*args)


HALF = SEQ // 2
FT = MXU_DIM
FW = MXU_DIM
N_FOLD_TILES = HALF // FT
DFT_SCALE = SEQ ** -0.5


def _dft_kernel(ch_ref, sh_ref, rev_ref, cc_ref, sc_ref, p_ref, q_ref, o_ref):
    rev = rev_ref[...]
    row0 = lax.broadcasted_iota(jnp.int32, (FT, FW), 0) == 0
    alt = (1 - 2 * (lax.broadcasted_iota(jnp.int32, (FT, FW), 0) & 1)).astype(jnp.float32)

    def reversed_upper(x_ref, t):
        src = HALF + FT * (N_FOLD_TILES - 1 - t)
        r = jnp.dot(rev, x_ref[src:src + FT, :], preferred_element_type=jnp.float32)
        if t == 0:
            return r
        first = HALF + FT * (N_FOLD_TILES - t)
        return jnp.where(row0, x_ref[first:first + 1, :].astype(jnp.float32), r)

    pfs, qfs, alt_sum = [], [], None
    for t in range(N_FOLD_TILES):
        rows = slice(t * FT, (t + 1) * FT)
        pf = p_ref[rows, :].astype(jnp.float32) + reversed_upper(p_ref, t)
        qf = q_ref[rows, :].astype(jnp.float32) - reversed_upper(q_ref, t)
        alt_sum = pf if alt_sum is None else alt_sum + pf
        pfs.append(pf.astype(jnp.bfloat16))
        qfs.append(qf.astype(jnp.bfloat16))
    p_mid = p_ref[HALF:HALF + 1, :].astype(jnp.float32)
    sign = jnp.concatenate([alt] * N_FOLD_TILES, axis=0)
    e = jnp.dot(ch_ref[...], jnp.concatenate(pfs, axis=0), preferred_element_type=jnp.float32)
    e = e + sign * (DFT_SCALE * p_mid)
    o = jnp.dot(sh_ref[...], jnp.concatenate(qfs, axis=0), preferred_element_type=jnp.float32)
    o_ref[0:HALF, :] = (e + o).astype(jnp.bfloat16)
    g = (e - o).astype(jnp.bfloat16)
    mid = DFT_SCALE * (jnp.sum(alt_sum * alt, axis=0, keepdims=True) + p_mid)
    for t in range(N_FOLD_TILES):
        src = FT * (N_FOLD_TILES - 1 - t)
        up = jnp.dot(rev, g[src:src + FT, :], preferred_element_type=jnp.float32)
        first = mid if t == 0 else g[src + FT:src + FT + 1, :].astype(jnp.float32)
        o_ref[HALF + t * FT:HALF + (t + 1) * FT, :] = jnp.where(row0, first, up).astype(jnp.bfloat16)
    ctx = jnp.dot(cc_ref[...], p_ref[SEQ:, :], preferred_element_type=jnp.float32)
    ctx = ctx + jnp.dot(sc_ref[...], q_ref[SEQ:, :], preferred_element_type=jnp.float32)
    o_ref[SEQ:, :] = ctx.astype(jnp.bfloat16)


def _dft(tables, pq):
    ch, sh, rev, cc, sc = tables
    batch = pq.shape[0]
    n_col = D_FOURIER // FW
    return pl.pallas_call(
        _dft_kernel,
        grid=(batch, n_col),
        in_specs=[
            _resident((HALF, HALF), lambda b, c: (0, 0)),
            _resident((HALF, HALF), lambda b, c: (0, 0)),
            _resident((FT, FT), lambda b, c: (0, 0)),
            _resident((CTX, CTX), lambda b, c: (0, 0)),
            _resident((CTX, CTX), lambda b, c: (0, 0)),
            pl.BlockSpec((None, ROWS, FW), lambda b, c: (b, 0, c)),
            pl.BlockSpec((None, ROWS, FW), lambda b, c: (b, 0, n_col + c)),
        ],
        out_specs=pl.BlockSpec((None, ROWS, FW), lambda b, c: (b, 0, c)),
        out_shape=jax.ShapeDtypeStruct((batch, ROWS, D_FOURIER), jnp.bfloat16),
        compiler_params=_cparams(2),
        name="dft",
    )(ch, sh, rev, cc, sc, pq, pq)


KEY_TILE = MXU_DIM
N_KEY_TILES = ROWS // KEY_TILE


def _with_row_sums(v):
    return jnp.concatenate([v, jnp.ones_like(v)], axis=1)


def _normalised(pv):
    return (pv[:, :HEAD_DIM] / pv[:, HEAD_DIM:]).astype(jnp.bfloat16)


def _attn_kernel(q_ref, kt_ref, v_ref, o_ref, s_ref, m_ref):
    @pl.when(pl.program_id(0) == 0)
    def _():
        s_ref[...] = jnp.zeros(s_ref.shape, s_ref.dtype)
        m_ref[...] = jnp.zeros(m_ref.shape, m_ref.dtype)

    outs = []
    for hd in range(GROUP):
        q = q_ref[:, hd * HEAD_DIM:(hd + 1) * HEAD_DIM]
        m_prev = m_ref[hd]
        m_lane = None
        acc = None
        for j in range(N_KEY_TILES):
            keys = slice(j * KEY_TILE, (j + 1) * KEY_TILE)
            s_old = s_ref[hd, :, keys]
            p = jnp.exp2(s_old - jnp.concatenate([m_prev] * (KEY_TILE // HEAD_DIM), axis=1))
            pv = jnp.dot(p.astype(jnp.bfloat16), _with_row_sums(v_ref[keys, :]),
                         preferred_element_type=jnp.float32)
            acc = pv if acc is None else acc + pv
            s_new = jnp.dot(q, kt_ref[:, keys], preferred_element_type=jnp.float32)
            s_ref[hd, :, keys] = s_new
            for c in range(KEY_TILE // HEAD_DIM):
                part = s_new[:, c * HEAD_DIM:(c + 1) * HEAD_DIM]
                m_lane = part if m_lane is None else jnp.maximum(m_lane, part)
        outs.append(_normalised(acc))
        m_ref[hd] = jnp.broadcast_to(jnp.max(m_lane, axis=-1, keepdims=True), m_lane.shape)
    o_ref[...] = jnp.concatenate(outs, axis=1)


def _attention_latent(q, kt, v):
    batch = q.shape[0]
    gw = GROUP * HEAD_DIM
    per_sample = N_KV * N_LAT_TILES
    steps = batch * per_sample + 1

    def unflatten(c):
        return c // per_sample, (c % per_sample) // N_LAT_TILES, c % N_LAT_TILES

    def cur(t):
        return unflatten(jnp.minimum(t, steps - 2))

    def prev(t):
        return unflatten(jnp.maximum(t - 1, 0))

    def q_map(t):
        b, g, i = cur(t)
        return b, i, g

    def kt_map(t):
        b, g, _ = cur(t)
        return b, g, 0, 0

    def v_map(t):
        b, g, _ = prev(t)
        return b, 0, g

    def o_map(t):
        b, g, i = prev(t)
        return b, i, g

    return pl.pallas_call(
        _attn_kernel,
        grid=(steps,),
        in_specs=[
            pl.BlockSpec((None, TM, gw), q_map),
            pl.BlockSpec((None, None, HEAD_DIM, ROWS), kt_map),
            pl.BlockSpec((None, ROWS, HEAD_DIM), v_map),
        ],
        out_specs=pl.BlockSpec((None, TM, gw), o_map),
        out_shape=jax.ShapeDtypeStruct((batch, SEQ, D), jnp.bfloat16),
        scratch_shapes=[pltpu.VMEM((GROUP, TM, ROWS), jnp.float32),
                        pltpu.VMEM((GROUP, TM, HEAD_DIM), jnp.float32)],
        compiler_params=_cparams(1),
        name="attention",
    )(q, kt, v)


def _attn_ctx_kernel(q_ref, kt_ref, v_ref, o_ref):
    v1 = _with_row_sums(v_ref[...])
    kt = kt_ref[...]
    outs = []
    for hd in range(GROUP):
        s = jnp.dot(q_ref[:, hd * HEAD_DIM:(hd + 1) * HEAD_DIM], kt,
                    preferred_element_type=jnp.float32)
        p = jnp.exp2(s - jnp.max(s, axis=-1, keepdims=True)).astype(jnp.bfloat16)
        outs.append(_normalised(jnp.dot(p, v1, preferred_element_type=jnp.float32)))
    o_ref[...] = jnp.concatenate(outs, axis=1)


def _attention_context(q, kt, v):
    batch = q.shape[0]
    gw = GROUP * HEAD_DIM
    return pl.pallas_call(
        _attn_ctx_kernel,
        grid=(batch, N_KV),
        in_specs=[
            pl.BlockSpec((None, CTX, gw), lambda b, g: (b, SEQ // CTX, g)),
            pl.BlockSpec((None, None, HEAD_DIM, CTX), lambda b, g: (b, g, 0, SEQ // CTX)),
            pl.BlockSpec((None, CTX, HEAD_DIM), lambda b, g: (b, SEQ // CTX, g)),
        ],
        out_specs=pl.BlockSpec((None, CTX, gw), lambda b, g: (b, 0, g)),
        out_shape=jax.ShapeDtypeStruct((batch, CTX, D), jnp.bfloat16),
        compiler_params=_cparams(2),
        name="attention_ctx",
    )(q, kt, v)


def _cos_sin_products(n, size):
    r = 1 << (int(math.log2(size)) // 2)
    hi = size // r
    j = jnp.arange(size, dtype=jnp.int32)[:, None]
    ang_hi = ((j * jnp.arange(hi, dtype=jnp.int32)[None, :] * r) % n).astype(jnp.float32)
    ang_lo = ((j * jnp.arange(r, dtype=jnp.int32)[None, :]) % n).astype(jnp.float32)
    w = 2.0 * math.pi / n
    ca, sa = jnp.cos(ang_hi * w)[:, :, None], jnp.sin(ang_hi * w)[:, :, None]
    cb, sb = jnp.cos(ang_lo * w)[:, None, :], jnp.sin(ang_lo * w)[:, None, :]
    scale = n ** -0.5
    cos = ((ca * cb - sa * sb) * scale).reshape(size, size)
    sin = ((sa * cb + ca * sb) * scale).reshape(size, size)
    return cos, sin


def _dft_tables():
    ch, sh = _cos_sin_products(SEQ, HALF)
    cc, sc = _cos_sin_products(CTX, CTX)
    r = jnp.arange(FT, dtype=jnp.int32)
    rev = ((r[:, None] + r[None, :]) == FT).astype(jnp.bfloat16)
    return tuple(t.astype(jnp.bfloat16) for t in (ch, sh, rev, cc, sc))


def _channel_dft():
    c, s = _cos_sin_products(GROUP_CH, GROUP_CH)
    return jnp.concatenate([c, -s], axis=1).astype(jnp.bfloat16)


def _rope_tables():
    half = HEAD_DIM // 2
    rows = SEQ // GRID_W
    row = jnp.repeat(jnp.arange(rows, dtype=jnp.float32), GRID_W)
    col = jnp.tile(jnp.arange(GRID_W, dtype=jnp.float32), rows)
    inv_freq = ROPE_THETA ** (-jnp.arange(0, half, 2, dtype=jnp.float32) / half)
    ang_row = row[:, None] * inv_freq
    ang_col = col[:, None] * inv_freq
    cos = jnp.concatenate([jnp.cos(ang_row)] * 2 + [jnp.cos(ang_col)] * 2, axis=1)
    sin = jnp.concatenate([-jnp.sin(ang_row), jnp.sin(ang_row),
                           -jnp.sin(ang_col), jnp.sin(ang_col)], axis=1)
    cos = jnp.concatenate([cos, jnp.ones((CTX, HEAD_DIM), jnp.float32)], axis=0)
    sin = jnp.concatenate([sin, jnp.zeros((CTX, HEAD_DIM), jnp.float32)], axis=0)
    return cos, sin


def _pad_ffn(w_gu, w_down):
    depth = w_gu.shape[0]
    pad = D_FF_PAD - D_FF
    wgu = jnp.pad(w_gu.reshape(depth, D, 2, D_FF), ((0, 0), (0, 0), (0, 0), (0, pad)))
    wgu = wgu.reshape(depth, D, 2 * D_FF_PAD).astype(jnp.bfloat16)
    wd = jnp.pad(w_down, ((0, 0), (0, pad), (0, 0))).astype(jnp.bfloat16)
    return wgu, wd


def kernel(x, c, ctx, c_ctx, w_mod, b_mod, g_ffn1, w_ffn1_gu, w_ffn1_down, g_mix, g_ffn2, w_ffn2_gu, w_ffn2_down, g_final, w_in_ab, g_v, w_s, b_s, w_out_ab, w_qkv, g_q, g_k, w_o):
    batch = x.shape[0]
    depth = w_mod.shape[0]
    assert x.shape == (batch, SEQ, D) and ctx.shape == (batch, CTX, D)
    assert depth % 2 == 0

    mod_rows = -(-(batch + 1) // 8) * 8
    cc = jnp.concatenate([c, c_ctx[None], jnp.zeros((mod_rows - batch - 1, D), jnp.float32)], axis=0)
    m = _modulation(cc, w_mod, b_mod)
    m = m[:, :batch + 1].reshape(depth, batch + 1, N_MOD, D)

    wgu1, wd1 = _pad_ffn(w_ffn1_gu, w_ffn1_down)
    wgu2, wd2 = _pad_ffn(w_ffn2_gu, w_ffn2_down)
    g1 = g_ffn1.reshape(depth, 1, D)
    g2 = g_ffn2.reshape(depth, 1, D)
    gm = g_mix.reshape(depth, 1, D)
    w_in = w_in_ab.astype(jnp.bfloat16)
    w_out = w_out_ab.astype(jnp.bfloat16)
    ws = w_s.astype(jnp.bfloat16)
    gv = g_v.reshape(-1, 1, D_SGU)
    bs_full = jnp.repeat(jnp.swapaxes(b_s, 1, 2), GROUP_CH, axis=2)
    wqkv = w_qkv.astype(jnp.bfloat16)
    wo = w_o.astype(jnp.bfloat16)
    gq = g_q.reshape(-1, 1, HEAD_DIM)
    gk = g_k.reshape(-1, 1, HEAD_DIM)
    dft_tables = _dft_tables()
    ccs = _channel_dft()
    cos_t, sin_t = _rope_tables()

    xs = (x, ctx)
    for l in range(depth):
        last = l == depth - 1
        even = l % 2 == 0
        mods = m[l]
        n_out = N_LAT_TILES if last else N_TILES
        if even:
            xc, pq, sgu = _layer_in(xs, mods, g1, wgu1, wd1, gm, l, True,
                                    (w_in, ccs, gv, ws, bs_full))
            ys, w_proj = [_dft(dft_tables, pq), sgu], w_out
        else:
            xc, q, kt, v = _layer_in(xs, mods, g1, wgu1, wd1, gm, l, False,
                                     (wqkv, gq, gk, cos_t, sin_t))
            ys, w_proj = [_attention_latent(q, kt, v)], wo
            if not last:
                ys.append(_attention_context(q, kt, v))
        xc = _layer_out(xc, mods, w_proj, ys, g2, wgu2, wd2, l, n_out,
                        g_final.reshape(1, D) if last else None,
                        split_y=not even and not last)
        xs = (xc,)
    return xc
```

```python
import functools
import math

import jax
import jax.numpy as jnp
from jax import lax
from jax.experimental import pallas as pl
from jax.experimental.pallas import tpu as pltpu

D = 1024
SEQ = 4096
CTX = 256
ROWS = SEQ + CTX
GRID_W = 64
D_FF = 2752
N_MOD = 9
EPS = 1e-6
HEAD_DIM = 128
N_HEADS = 8
N_KV = 2
GROUP = N_HEADS // N_KV
KV_W = N_KV * HEAD_DIM
ROPE_THETA = 10000.0
D_FOURIER = 512
D_SGU = 512
GROUP_CH = 128
N_GROUPS = 4
CHUNK = 128

MXU_DIM = 256
TM = 256
N_LAT_TILES = SEQ // TM
N_TILES = ROWS // TM
VMEM_LIMIT = 48 * 1024 * 1024

assert TM == CTX and SEQ % TM == 0


def _cparams(n_axes):
    return pltpu.CompilerParams(dimension_semantics=("arbitrary",) * n_axes,
                                vmem_limit_bytes=VMEM_LIMIT)


def _resident(block_shape, index_map):
    return pl.BlockSpec(block_shape, index_map, pipeline_mode=pl.Buffered(1))


def _tile_spec(width):
    return pl.BlockSpec((None, TM, width), lambda b, i: (b, i, 0))


def _mod_spec():
    return pl.BlockSpec(
        (None, N_MOD, D),
        lambda b, i: (jnp.where(i == N_LAT_TILES, pl.num_programs(0), b), 0, 0))


def _modnorm(x, g, shift, scale):
    ms = jnp.mean(x * x, axis=-1, keepdims=True)
    y = x * lax.rsqrt(ms + EPS) * g
    return y * (1.0 + scale) + shift


def _ffn_half_step(x, mod_ref, row0, g, wgu_ref, wd_ref):
    h = _modnorm(x, g, mod_ref[row0:row0 + 1, :], mod_ref[row0 + 1:row0 + 2, :])
    h = h.astype(jnp.bfloat16)
    gu = jnp.dot(h, wgu_ref[...], preferred_element_type=jnp.float32)
    gt, up = gu[:, :D_FF], gu[:, D_FF:]
    a = (gt * jax.nn.sigmoid(gt) * up).astype(jnp.bfloat16)
    y = jnp.dot(a, wd_ref[...], preferred_element_type=jnp.float32)
    return x + 0.5 * mod_ref[row0 + 2:row0 + 3, :] * y


def _mod_kernel(cc_ref, w_ref, b_ref, o_ref):
    s = cc_ref[...]
    s = s * jax.nn.sigmoid(s)
    s_hi = s.astype(jnp.bfloat16)
    s_lo = (s - s_hi.astype(jnp.float32)).astype(jnp.bfloat16)
    w = w_ref[...]
    w_hi = w.astype(jnp.bfloat16)
    w_lo = (w - w_hi.astype(jnp.float32)).astype(jnp.bfloat16)
    rows = s.shape[0]
    both = jnp.dot(jnp.concatenate([s_hi, s_lo], axis=0), w_hi,
                   preferred_element_type=jnp.float32)
    cross = jnp.dot(s_hi, w_lo, preferred_element_type=jnp.float32)
    o_ref[...] = both[:rows] + both[rows:] + cross + b_ref[...]


def _modulation(cc, w_mod, b_mod):
    depth = w_mod.shape[0]
    rows = cc.shape[0]
    tn = 1024
    return pl.pallas_call(
        _mod_kernel,
        grid=(depth, N_MOD * D // tn),
        in_specs=[
            pl.BlockSpec((rows, D), lambda l, j: (0, 0)),
            pl.BlockSpec((None, D, tn), lambda l, j: (l, 0, j)),
            pl.BlockSpec((None, 1, tn), lambda l, j: (l, 0, j)),
        ],
        out_specs=pl.BlockSpec((None, rows, tn), lambda l, j: (l, 0, j)),
        out_shape=jax.ShapeDtypeStruct((depth, rows, N_MOD * D), jnp.float32),
        compiler_params=_cparams(2),
        name="modulation",
    )(cc, w_mod, b_mod.reshape(depth, 1, N_MOD * D))


def _even_in_stage(h, win_ref, ccs_ref, gv_ref, ws_ref, bs_ref, pq_ref, sgu_ref):
    p = jnp.dot(h, win_ref[...], preferred_element_type=jnp.float32)
    a = p[:, :D_FOURIER].astype(jnp.bfloat16)
    uv = jax.nn.gelu(p[:, D_FOURIER:], approximate=True)
    u = uv[:, :D_SGU]
    v = uv[:, D_SGU:]
    ccs = ccs_ref[...]
    ps, qs, gated = [], [], []
    n_chunks = TM // CHUNK
    for grp in range(N_GROUPS):
        lo, hi = grp * GROUP_CH, (grp + 1) * GROUP_CH
        t = jnp.dot(a[:, lo:hi], ccs, preferred_element_type=jnp.float32)
        ps.append(t[:, :GROUP_CH])
        qs.append(t[:, GROUP_CH:])
        vg = v[:, lo:hi]
        ms = jnp.mean(vg * vg, axis=-1, keepdims=True)
        vh = (vg * lax.rsqrt(ms + EPS) * gv_ref[:, lo:hi]).astype(jnp.bfloat16)
        rhs = jnp.concatenate([vh[c * CHUNK:(c + 1) * CHUNK, :] for c in range(n_chunks)], axis=1)
        mixed = jnp.dot(ws_ref[grp], rhs, preferred_element_type=jnp.float32)
        mixed = jnp.concatenate(
            [mixed[:, c * GROUP_CH:(c + 1) * GROUP_CH] for c in range(n_chunks)], axis=0)
        bias = jnp.concatenate([bs_ref[:, lo:hi]] * n_chunks, axis=0)
        gated.append(u[:, lo:hi] * (mixed + bias))
    pq_ref[...] = jnp.concatenate(ps + qs, axis=1).astype(jnp.bfloat16)
    sgu_ref[...] = jnp.concatenate(gated, axis=1).astype(jnp.bfloat16)


def _rope(t, cos, sin_signed, first_half):
    partner = jnp.where(first_half, pltpu.roll(t, 3 * HEAD_DIM // 4, 1),
                        pltpu.roll(t, HEAD_DIM // 4, 1))
    return t * cos + partner * sin_signed


def _qkv_stage(h, w_ref, gq_ref, gk_ref, cos_ref, sin_ref, q_ref, kt_ref, v_ref):
    qkv = jnp.dot(h, w_ref[...], preferred_element_type=jnp.float32)
    cos = cos_ref[...]
    sin = sin_ref[...]
    lane = lax.broadcasted_iota(jnp.int32, (TM, HEAD_DIM), 1)
    first_half = (lane % (HEAD_DIM // 2)) < (HEAD_DIM // 4)
    q_scale = (HEAD_DIM ** -0.5) * math.log2(math.e)

    def head(col, gain):
        t = qkv[:, col:col + HEAD_DIM]
        ms = jnp.mean(t * t, axis=-1, keepdims=True)
        return _rope(t * lax.rsqrt(ms + EPS) * gain, cos, sin, first_half)

    qs = [head(hd * HEAD_DIM, gq_ref[...]) * q_scale for hd in range(N_HEADS)]
    q_ref[...] = jnp.concatenate(qs, axis=1).astype(jnp.bfloat16)
    for hd in range(N_KV):
        kt_ref[hd] = head((N_HEADS + hd) * HEAD_DIM, gk_ref[...]).T.astype(jnp.bfloat16)
    v_ref[...] = qkv[:, (N_HEADS + N_KV) * HEAD_DIM:].astype(jnp.bfloat16)


def _layer_in_kernel(*refs, split_in, even):
    refs = list(refs)
    if split_in:
        x_ref, ctx_ref = refs[:2]
        del refs[:2]
        x = jnp.where(pl.program_id(1) == N_LAT_TILES, ctx_ref[...], x_ref[...])
    else:
        x = refs.pop(0)[...]
    mod_ref, g1_ref, wgu_ref, wd_ref, gm_ref = refs[:5]
    n_out = 3 if even else 4
    stage_in = refs[5:-n_out]
    xo_ref, *stage_out = refs[-n_out:]
    x = _ffn_half_step(x, mod_ref, 0, g1_ref[...], wgu_ref, wd_ref)
    xo_ref[...] = x
    h = _modnorm(x, gm_ref[...], mod_ref[3:4, :], mod_ref[4:5, :]).astype(jnp.bfloat16)
    if even:
        _even_in_stage(h, *stage_in, *stage_out)
    else:
        _qkv_stage(h, *stage_in, *stage_out)


def _layer_in(xs, mods, g1, wgu, wd, gm, layer, even, stage_args):
    batch = xs[0].shape[0]
    j = layer // 2
    split_in = len(xs) == 2
    if split_in:
        x_specs = [
            pl.BlockSpec((None, TM, D), lambda b, i: (b, jnp.minimum(i, N_LAT_TILES - 1), 0)),
            pl.BlockSpec((None, CTX, D), lambda b, i: (b, 0, 0)),
        ]
    else:
        x_specs = [_tile_spec(D)]
    in_specs = x_specs + [
        _mod_spec(),
        pl.BlockSpec((None, 1, D), lambda b, i: (layer, 0, 0)),
        _resident((None, D, 2 * D_FF), lambda b, i: (layer, 0, 0)),
        _resident((None, D_FF, D), lambda b, i: (layer, 0, 0)),
        pl.BlockSpec((None, 1, D), lambda b, i: (layer, 0, 0)),
    ]
    out_specs = [_tile_spec(D)]
    out_shape = [jax.ShapeDtypeStruct((batch, ROWS, D), jnp.float32)]
    if even:
        in_specs += [
            _resident((None, D, D_FOURIER + 2 * D_SGU), lambda b, i: (j, 0, 0)),
            _resident((GROUP_CH, 2 * GROUP_CH), lambda b, i: (0, 0)),
            pl.BlockSpec((None, 1, D_SGU), lambda b, i: (j, 0, 0)),
            _resident((None, N_GROUPS, CHUNK, CHUNK), lambda b, i: (j, 0, 0, 0)),
            pl.BlockSpec((None, CHUNK, D_SGU), lambda b, i: (j, 0, 0)),
        ]
        out_specs += [_tile_spec(2 * D_FOURIER), _tile_spec(D_SGU)]
        out_shape += [jax.ShapeDtypeStruct((batch, ROWS, 2 * D_FOURIER), jnp.bfloat16),
                      jax.ShapeDtypeStruct((batch, ROWS, D_SGU), jnp.bfloat16)]
    else:
        in_specs += [
            _resident((None, D, D + 2 * KV_W), lambda b, i: (j, 0, 0)),
            pl.BlockSpec((None, 1, HEAD_DIM), lambda b, i: (j, 0, 0)),
            pl.BlockSpec((None, 1, HEAD_DIM), lambda b, i: (j, 0, 0)),
            pl.BlockSpec((TM, HEAD_DIM), lambda b, i: (i, 0)),
            pl.BlockSpec((TM, HEAD_DIM), lambda b, i: (i, 0)),
        ]
        out_specs += [
            _tile_spec(D),
            pl.BlockSpec((None, N_KV, HEAD_DIM, TM), lambda b, i: (b, 0, 0, i)),
            _tile_spec(KV_W),
        ]
        out_shape += [jax.ShapeDtypeStruct((batch, ROWS, D), jnp.bfloat16),
                      jax.ShapeDtypeStruct((batch, N_KV, HEAD_DIM, ROWS), jnp.bfloat16),
                      jax.ShapeDtypeStruct((batch, ROWS, KV_W), jnp.bfloat16)]
    return pl.pallas_call(
        functools.partial(_layer_in_kernel, split_in=split_in, even=even),
        grid=(batch, N_TILES),
        in_specs=in_specs,
        out_specs=out_specs,
        out_shape=out_shape,
        compiler_params=_cparams(2),
        name="layer_in_even" if even else "layer_in_odd",
    )(*xs, mods, g1, wgu, wd, gm, *stage_args)


def _layer_out_kernel(x_ref, mod_ref, wo_ref, *refs, n_y, split_y, final):
    y_refs = refs[:n_y]
    g2_ref, wgu_ref, wd_ref = refs[n_y:n_y + 3]
    o_ref = refs[-1]
    if split_y:
        y = jnp.where(pl.program_id(1) == N_LAT_TILES, y_refs[1][...], y_refs[0][...])
    else:
        y = y_refs[0][...] if n_y == 1 else jnp.concatenate([r[...] for r in y_refs], axis=1)
    x = x_ref[...] + mod_ref[5:6, :] * jnp.dot(y, wo_ref[...], preferred_element_type=jnp.float32)
    x = _ffn_half_step(x, mod_ref, 6, g2_ref[...], wgu_ref, wd_ref)
    if final:
        gf_ref = refs[-2]
        ms = jnp.mean(x * x, axis=-1, keepdims=True)
        x = x * lax.rsqrt(ms + EPS) * gf_ref[...]
    o_ref[...] = x


def _layer_out(xc, mods, wo, ys, g2, wgu, wd, layer, n_tiles, g_final=None, split_y=False):
    batch = xc.shape[0]
    j = layer // 2
    final = g_final is not None
    in_specs = [_tile_spec(D), _mod_spec(), _resident((None, D, D), lambda b, i: (j, 0, 0))]
    if split_y:
        in_specs += [
            pl.BlockSpec((None, TM, D), lambda b, i: (b, jnp.minimum(i, N_LAT_TILES - 1), 0)),
            pl.BlockSpec((None, CTX, D), lambda b, i: (b, 0, 0)),
        ]
    else:
        in_specs += [_tile_spec(y.shape[-1]) for y in ys]
    in_specs += [
        pl.BlockSpec((None, 1, D), lambda b, i: (layer, 0, 0)),
        _resident((None, D, 2 * D_FF), lambda b, i: (layer, 0, 0)),
        _resident((None, D_FF, D), lambda b, i: (layer, 0, 0)),
    ]
    args = [xc, mods, wo, *ys, g2, wgu, wd]
    if final:
        in_specs.append(pl.BlockSpec((1, D), lambda b, i: (0, 0)))
        args.append(g_final)
    return pl.pallas_call(
        functools.partial(_layer_out_kernel, n_y=len(ys), split_y=split_y, final=final),
        grid=(batch, n_tiles),
        in_specs=in_specs,
        out_specs=_tile_spec(D),
        out_shape=jax.ShapeDtypeStruct((batch, n_tiles * TM, D), jnp.float32),
        compiler_params=_cparams(2),
        name="layer_out_final" if final else "layer_out",
    )(*args)


HALF = SEQ // 2
FT = MXU_DIM
FW = MXU_DIM
N_FOLD_TILES = HALF // FT
DFT_SCALE = SEQ ** -0.5


def _dft_kernel(ch_ref, sh_ref, rev_ref, cc_ref, sc_ref, p_ref, q_ref, o_ref):
    rev = rev_ref[...]
    row0 = lax.broadcasted_iota(jnp.int32, (FT, FW), 0) == 0
    alt = (1 - 2 * (lax.broadcasted_iota(jnp.int32, (FT, FW), 0) & 1)).astype(jnp.float32)

    def reversed_upper(x_ref, t):
        src = HALF + FT * (N_FOLD_TILES - 1 - t)
        r = jnp.dot(rev, x_ref[src:src + FT, :], preferred_element_type=jnp.float32)
        if t == 0:
            return r
        first = HALF + FT * (N_FOLD_TILES - t)
        return jnp.where(row0, x_ref[first:first + 1, :].astype(jnp.float32), r)

    pfs, qfs, alt_sum = [], [], None
    for t in range(N_FOLD_TILES):
        rows = slice(t * FT, (t + 1) * FT)
        pf = p_ref[rows, :].astype(jnp.float32) + reversed_upper(p_ref, t)
        qf = q_ref[rows, :].astype(jnp.float32) - reversed_upper(q_ref, t)
        alt_sum = pf if alt_sum is None else alt_sum + pf
        pfs.append(pf.astype(jnp.bfloat16))
        qfs.append(qf.astype(jnp.bfloat16))
    p_mid = p_ref[HALF:HALF + 1, :].astype(jnp.float32)
    sign = jnp.concatenate([alt] * N_FOLD_TILES, axis=0)
    e = jnp.dot(ch_ref[...], jnp.concatenate(pfs, axis=0), preferred_element_type=jnp.float32)
    e = e + sign * (DFT_SCALE * p_mid)
    o = jnp.dot(sh_ref[...], jnp.concatenate(qfs, axis=0), preferred_element_type=jnp.float32)
    o_ref[0:HALF, :] = (e + o).astype(jnp.bfloat16)
    g = (e - o).astype(jnp.bfloat16)
    mid = DFT_SCALE * (jnp.sum(alt_sum * alt, axis=0, keepdims=True) + p_mid)
    for t in range(N_FOLD_TILES):
        src = FT * (N_FOLD_TILES - 1 - t)
        up = jnp.dot(rev, g[src:src + FT, :], preferred_element_type=jnp.float32)
        first = mid if t == 0 else g[src + FT:src + FT + 1, :].astype(jnp.float32)
        o_ref[HALF + t * FT:HALF + (t + 1) * FT, :] = jnp.where(row0, first, up).astype(jnp.bfloat16)
    ctx = jnp.dot(cc_ref[...], p_ref[SEQ:, :], preferred_element_type=jnp.float32)
    ctx = ctx + jnp.dot(sc_ref[...], q_ref[SEQ:, :], preferred_element_type=jnp.float32)
    o_ref[SEQ:, :] = ctx.astype(jnp.bfloat16)


def _dft(tables, pq):
    ch, sh, rev, cc, sc = tables
    batch = pq.shape[0]
    n_col = D_FOURIER // FW
    return pl.pallas_call(
        _dft_kernel,
        grid=(batch, n_col),
        in_specs=[
            _resident((HALF, HALF), lambda b, c: (0, 0)),
            _resident((HALF, HALF), lambda b, c: (0, 0)),
            _resident((FT, FT), lambda b, c: (0, 0)),
            _resident((CTX, CTX), lambda b, c: (0, 0)),
            _resident((CTX, CTX), lambda b, c: (0, 0)),
            pl.BlockSpec((None, ROWS, FW), lambda b, c: (b, 0, c)),
            pl.BlockSpec((None, ROWS, FW), lambda b, c: (b, 0, n_col + c)),
        ],
        out_specs=pl.BlockSpec((None, ROWS, FW), lambda b, c: (b, 0, c)),
        out_shape=jax.ShapeDtypeStruct((batch, ROWS, D_FOURIER), jnp.bfloat16),
        compiler_params=_cparams(2),
        name="dft",
    )(ch, sh, rev, cc, sc, pq, pq)


KEY_TILE = MXU_DIM
N_KEY_TILES = ROWS // KEY_TILE


def _with_row_sums(v):
    return jnp.concatenate([v, jnp.ones_like(v)], axis=1)


def _normalised(pv):
    return (pv[:, :HEAD_DIM] / pv[:, HEAD_DIM:]).astype(jnp.bfloat16)


def _attn_kernel(q_ref, kt_ref, v_ref, o_ref, s_ref, m_ref):
    @pl.when(pl.program_id(0) == 0)
    def _():
        s_ref[...] = jnp.zeros(s_ref.shape, s_ref.dtype)
        m_ref[...] = jnp.zeros(m_ref.shape, m_ref.dtype)

    outs = []
    for hd in range(GROUP):
        q = q_ref[:, hd * HEAD_DIM:(hd + 1) * HEAD_DIM]
        m_prev = m_ref[hd]
        m_lane = None
        acc = None
        for j in range(N_KEY_TILES):
            keys = slice(j * KEY_TILE, (j + 1) * KEY_TILE)
            s_old = s_ref[hd, :, keys]
            p = jnp.exp2(s_old - jnp.concatenate([m_prev] * (KEY_TILE // HEAD_DIM), axis=1))
            pv = jnp.dot(p.astype(jnp.bfloat16), _with_row_sums(v_ref[keys, :]),
                         preferred_element_type=jnp.float32)
            acc = pv if acc is None else acc + pv
            s_new = jnp.dot(q, kt_ref[:, keys], preferred_element_type=jnp.float32)
            s_ref[hd, :, keys] = s_new
            for c in range(KEY_TILE // HEAD_DIM):
                part = s_new[:, c * HEAD_DIM:(c + 1) * HEAD_DIM]
                m_lane = part if m_lane is None else jnp.maximum(m_lane, part)
        outs.append(_normalised(acc))
        m_ref[hd] = jnp.broadcast_to(jnp.max(m_lane, axis=-1, keepdims=True), m_lane.shape)
    o_ref[...] = jnp.concatenate(outs, axis=1)


def _attention_latent(q, kt, v):
    batch = q.shape[0]
    gw = GROUP * HEAD_DIM
    per_sample = N_KV * N_LAT_TILES
    steps = batch * per_sample + 1

    def unflatten(c):
        return c // per_sample, (c % per_sample) // N_LAT_TILES, c % N_LAT_TILES

    def cur(t):
        return unflatten(jnp.minimum(t, steps - 2))

    def prev(t):
        return unflatten(jnp.maximum(t - 1, 0))

    def q_map(t):
        b, g, i = cur(t)
        return b, i, g

    def kt_map(t):
        b, g, _ = cur(t)
        return b, g, 0, 0

    def v_map(t):
        b, g, _ = prev(t)
        return b, 0, g

    def o_map(t):
        b, g, i = prev(t)
        return b, i, g

    return pl.pallas_call(
        _attn_kernel,
        grid=(steps,),
        in_specs=[
            pl.BlockSpec((None, TM, gw), q_map),
            pl.BlockSpec((None, None, HEAD_DIM, ROWS), kt_map),
            pl.BlockSpec((None, ROWS, HEAD_DIM), v_map),
        ],
        out_specs=pl.BlockSpec((None, TM, gw), o_map),
        out_shape=jax.ShapeDtypeStruct((batch, SEQ, D), jnp.bfloat16),
        scratch_shapes=[pltpu.VMEM((GROUP, TM, ROWS), jnp.float32),
                        pltpu.VMEM((GROUP, TM, HEAD_DIM), jnp.float32)],
        compiler_params=_cparams(1),
        name="attention",
    )(q, kt, v)


def _attn_ctx_kernel(q_ref, kt_ref, v_ref, o_ref):
    v1 = _with_row_sums(v_ref[...])
    kt = kt_ref[...]
    outs = []
    for hd in range(GROUP):
        s = jnp.dot(q_ref[:, hd * HEAD_DIM:(hd + 1) * HEAD_DIM], kt,
                    preferred_element_type=jnp.float32)
        p = jnp.exp2(s - jnp.max(s, axis=-1, keepdims=True)).astype(jnp.bfloat16)
        outs.append(_normalised(jnp.dot(p, v1, preferred_element_type=jnp.float32)))
    o_ref[...] = jnp.concatenate(outs, axis=1)


def _attention_context(q, kt, v):
    batch = q.shape[0]
    gw = GROUP * HEAD_DIM
    return pl.pallas_call(
        _attn_ctx_kernel,
        grid=(batch, N_KV),
        in_specs=[
            pl.BlockSpec((None, CTX, gw), lambda b, g: (b, SEQ // CTX, g)),
            pl.BlockSpec((None, None, HEAD_DIM, CTX), lambda b, g: (b, g, 0, SEQ // CTX)),
            pl.BlockSpec((None, CTX, HEAD_DIM), lambda b, g: (b, SEQ // CTX, g)),
        ],
        out_specs=pl.BlockSpec((None, CTX, gw), lambda b, g: (b, 0, g)),
        out_shape=jax.ShapeDtypeStruct((batch, CTX, D), jnp.bfloat16),
        compiler_params=_cparams(2),
        name="attention_ctx",
    )(q, kt, v)


def _cos_sin_products(n, size):
    r = 1 << (int(math.log2(size)) // 2)
    hi = size // r
    j = jnp.arange(size, dtype=jnp.int32)[:, None]
    ang_hi = ((j * jnp.arange(hi, dtype=jnp.int32)[None, :] * r) % n).astype(jnp.float32)
    ang_lo = ((j * jnp.arange(r, dtype=jnp.int32)[None, :]) % n).astype(jnp.float32)
    w = 2.0 * math.pi / n
    ca, sa = jnp.cos(ang_hi * w)[:, :, None], jnp.sin(ang_hi * w)[:, :, None]
    cb, sb = jnp.cos(ang_lo * w)[:, None, :], jnp.sin(ang_lo * w)[:, None, :]
    scale = n ** -0.5
    cos = ((ca * cb - sa * sb) * scale).reshape(size, size)
    sin = ((sa * cb + ca * sb) * scale).reshape(size, size)
    return cos, sin


def _dft_tables():
    ch, sh = _cos_sin_products(SEQ, HALF)
    cc, sc = _cos_sin_products(CTX, CTX)
    r = jnp.arange(FT, dtype=jnp.int32)
    rev = ((r[:, None] + r[None, :]) == FT).astype(jnp.bfloat16)
    return tuple(t.astype(jnp.bfloat16) for t in (ch, sh, rev, cc, sc))


def _channel_dft():
    c, s = _cos_sin_products(GROUP_CH, GROUP_CH)
    return jnp.concatenate([c, -s], axis=1).astype(jnp.bfloat16)


def _rope_tables():
    half = HEAD_DIM // 2
    rows = SEQ // GRID_W
    row = jnp.repeat(jnp.arange(rows, dtype=jnp.float32), GRID_W)
    col = jnp.tile(jnp.arange(GRID_W, dtype=jnp.float32), rows)
    inv_freq = ROPE_THETA ** (-jnp.arange(0, half, 2, dtype=jnp.float32) / half)
    ang_row = row[:, None] * inv_freq
    ang_col = col[:, None] * inv_freq
    cos = jnp.concatenate([jnp.cos(ang_row)] * 2 + [jnp.cos(ang_col)] * 2, axis=1)
    sin = jnp.concatenate([-jnp.sin(ang_row), jnp.sin(ang_row),
                           -jnp.sin(ang_col), jnp.sin(ang_col)], axis=1)
    cos = jnp.concatenate([cos, jnp.ones((CTX, HEAD_DIM), jnp.float32)], axis=0)
    sin = jnp.concatenate([sin, jnp.zeros((CTX, HEAD_DIM), jnp.float32)], axis=0)
    return cos, sin


def kernel(x, c, ctx, c_ctx, w_mod, b_mod, g_ffn1, w_ffn1_gu, w_ffn1_down, g_mix, g_ffn2, w_ffn2_gu, w_ffn2_down, g_final, w_in_ab, g_v, w_s, b_s, w_out_ab, w_qkv, g_q, g_k, w_o):
    batch = x.shape[0]
    depth = w_mod.shape[0]
    assert x.shape == (batch, SEQ, D) and ctx.shape == (batch, CTX, D)
    assert depth % 2 == 0

    mod_rows = -(-(batch + 1) // 8) * 8
    cc = jnp.concatenate([c, c_ctx[None], jnp.zeros((mod_rows - batch - 1, D), jnp.float32)], axis=0)
    m = _modulation(cc, w_mod, b_mod)
    m = m[:, :batch + 1].reshape(depth, batch + 1, N_MOD, D)

    wgu1, wd1 = w_ffn1_gu.astype(jnp.bfloat16), w_ffn1_down.astype(jnp.bfloat16)
    wgu2, wd2 = w_ffn2_gu.astype(jnp.bfloat16), w_ffn2_down.astype(jnp.bfloat16)
    g1 = g_ffn1.reshape(depth, 1, D)
    g2 = g_ffn2.reshape(depth, 1, D)
    gm = g_mix.reshape(depth, 1, D)
    w_in = w_in_ab.astype(jnp.bfloat16)
    w_out = w_out_ab.astype(jnp.bfloat16)
    ws = w_s.astype(jnp.bfloat16)
    gv = g_v.reshape(-1, 1, D_SGU)
    bs_full = jnp.repeat(jnp.swapaxes(b_s, 1, 2), GROUP_CH, axis=2)
    wqkv = w_qkv.astype(jnp.bfloat16)
    wo = w_o.astype(jnp.bfloat16)
    gq = g_q.reshape(-1, 1, HEAD_DIM)
    gk = g_k.reshape(-1, 1, HEAD_DIM)
    dft_tables = _dft_tables()
    ccs = _channel_dft()
    cos_t, sin_t = _rope_tables()

    xs = (x, ctx)
    for l in range(depth):
        last = l == depth - 1
        even = l % 2 == 0
        mods = m[l]
        n_out = N_LAT_TILES if last else N_TILES
        if even:
            xc, pq, sgu = _layer_in(xs, mods, g1, wgu1, wd1, gm, l, True,
                                    (w_in, ccs, gv, ws, bs_full))
            ys, w_proj = [_dft(dft_tables, pq), sgu], w_out
        else:
            xc, q, kt, v = _layer_in(xs, mods, g1, wgu1, wd1, gm, l, False,
                                     (wqkv, gq, gk, cos_t, sin_t))
            ys, w_proj = [_attention_latent(q, kt, v)], wo
            if not last:
                ys.append(_attention_context(q, kt, v))
        xc = _layer_out(xc, mods, w_proj, ys, g2, wgu2, wd2, l, n_out,
                        g_final.reshape(1, D) if last else None,
                        split_y=not even and not last)
        xs = (xc,)
    return xc
```

```python
import functools
import math

import jax
import jax.numpy as jnp
from jax import lax
from jax.experimental import pallas as pl
from jax.experimental.pallas import tpu as pltpu

D = 1024
SEQ = 4096
CTX = 256
ROWS = SEQ + CTX
GRID_W = 64
D_FF = 2752
N_MOD = 9
EPS = 1e-6
HEAD_DIM = 128
N_HEADS = 8
N_KV = 2
GROUP = N_HEADS // N_KV
KV_W = N_KV * HEAD_DIM
ROPE_THETA = 10000.0
D_FOURIER = 512
D_SGU = 512
GROUP_CH = 128
N_GROUPS = 4
CHUNK = 128

MXU_DIM = 256
TM = 256
N_LAT_TILES = SEQ // TM
N_TILES = ROWS // TM
VMEM_LIMIT = 48 * 1024 * 1024

assert TM == CTX and SEQ % TM == 0


def _cparams(n_axes):
    return pltpu.CompilerParams(dimension_semantics=("arbitrary",) * n_axes,
                                vmem_limit_bytes=VMEM_LIMIT)


def _resident(block_shape, index_map):
    return pl.BlockSpec(block_shape, index_map, pipeline_mode=pl.Buffered(1))


def _tile_spec(width):
    return pl.BlockSpec((None, TM, width), lambda b, i: (b, i, 0))


def _mod_spec():
    return pl.BlockSpec(
        (None, N_MOD, D),
        lambda b, i: (jnp.where(i == N_LAT_TILES, pl.num_programs(0), b), 0, 0))


def _modnorm(x, g, shift, scale):
    ms = jnp.mean(x * x, axis=-1, keepdims=True)
    y = x * lax.rsqrt(ms + EPS) * g
    return y * (1.0 + scale) + shift


def _ffn_norm(x, mod_ref, row0, g):
    h = _modnorm(x, g, mod_ref[row0:row0 + 1, :], mod_ref[row0 + 1:row0 + 2, :])
    return h.astype(jnp.bfloat16)


def _ffn_apply(x, h, mod_ref, row0, wgu_ref, wd_ref):
    gu = jnp.dot(h, wgu_ref[...], preferred_element_type=jnp.float32)
    gt, up = gu[:, :D_FF], gu[:, D_FF:]
    a = (gt * jax.nn.sigmoid(gt) * up).astype(jnp.bfloat16)
    y = jnp.dot(a, wd_ref[...], preferred_element_type=jnp.float32)
    return x + 0.5 * mod_ref[row0 + 2:row0 + 3, :] * y


def _mod_kernel(cc_ref, w_ref, b_ref, o_ref):
    s = cc_ref[...]
    s = s * jax.nn.sigmoid(s)
    s_hi = s.astype(jnp.bfloat16)
    s_lo = (s - s_hi.astype(jnp.float32)).astype(jnp.bfloat16)
    w = w_ref[...]
    w_hi = w.astype(jnp.bfloat16)
    w_lo = (w - w_hi.astype(jnp.float32)).astype(jnp.bfloat16)
    rows = s.shape[0]
    both = jnp.dot(jnp.concatenate([s_hi, s_lo], axis=0), w_hi,
                   preferred_element_type=jnp.float32)
    cross = jnp.dot(s_hi, w_lo, preferred_element_type=jnp.float32)
    o_ref[...] = both[:rows] + both[rows:] + cross + b_ref[...]


def _modulation(cc, w_mod, b_mod):
    depth = w_mod.shape[0]
    rows = cc.shape[0]
    tn = 1024
    return pl.pallas_call(
        _mod_kernel,
        grid=(depth, N_MOD * D // tn),
        in_specs=[
            pl.BlockSpec((rows, D), lambda l, j: (0, 0)),
            pl.BlockSpec((None, D, tn), lambda l, j: (l, 0, j)),
            pl.BlockSpec((None, 1, tn), lambda l, j: (l, 0, j)),
        ],
        out_specs=pl.BlockSpec((None, rows, tn), lambda l, j: (l, 0, j)),
        out_shape=jax.ShapeDtypeStruct((depth, rows, N_MOD * D), jnp.float32),
        compiler_params=_cparams(2),
        name="modulation",
    )(cc, w_mod, b_mod.reshape(depth, 1, N_MOD * D))


def _even_in_stage(p, ccs_ref, gv_ref, ws_ref, bs_ref, pq_ref, sgu_ref):
    a = p[:, :D_FOURIER].astype(jnp.bfloat16)
    uv = jax.nn.gelu(p[:, D_FOURIER:], approximate=True)
    u = uv[:, :D_SGU]
    v = uv[:, D_SGU:]
    ccs = ccs_ref[...]
    ps, qs, gated = [], [], []
    n_chunks = TM // CHUNK
    for grp in range(N_GROUPS):
        lo, hi = grp * GROUP_CH, (grp + 1) * GROUP_CH
        t = jnp.dot(a[:, lo:hi], ccs, preferred_element_type=jnp.float32)
        ps.append(t[:, :GROUP_CH])
        qs.append(t[:, GROUP_CH:])
        vg = v[:, lo:hi]
        ms = jnp.mean(vg * vg, axis=-1, keepdims=True)
        vh = (vg * lax.rsqrt(ms + EPS) * gv_ref[:, lo:hi]).astype(jnp.bfloat16)
        rhs = jnp.concatenate([vh[c * CHUNK:(c + 1) * CHUNK, :] for c in range(n_chunks)], axis=1)
        mixed = jnp.dot(ws_ref[grp], rhs, preferred_element_type=jnp.float32)
        mixed = jnp.concatenate(
            [mixed[:, c * GROUP_CH:(c + 1) * GROUP_CH] for c in range(n_chunks)], axis=0)
        bias = jnp.concatenate([bs_ref[:, lo:hi]] * n_chunks, axis=0)
        gated.append(u[:, lo:hi] * (mixed + bias))
    pq_ref[...] = jnp.concatenate(ps + qs, axis=1).astype(jnp.bfloat16)
    sgu_ref[...] = jnp.concatenate(gated, axis=1).astype(jnp.bfloat16)


def _rope(t, cos, sin_signed, first_half):
    partner = jnp.where(first_half, pltpu.roll(t, 3 * HEAD_DIM // 4, 1),
                        pltpu.roll(t, HEAD_DIM // 4, 1))
    return t * cos + partner * sin_signed


def _qkv_stage(qkv, gq_ref, gk_ref, cos_ref, sin_ref, q_ref, kt_ref, v_ref):
    cos = cos_ref[...]
    sin = sin_ref[...]
    lane = lax.broadcasted_iota(jnp.int32, (TM, HEAD_DIM), 1)
    first_half = (lane % (HEAD_DIM // 2)) < (HEAD_DIM // 4)
    q_scale = (HEAD_DIM ** -0.5) * math.log2(math.e)

    def head(col, gain):
        t = qkv[:, col:col + HEAD_DIM]
        ms = jnp.mean(t * t, axis=-1, keepdims=True)
        return _rope(t * lax.rsqrt(ms + EPS) * gain, cos, sin, first_half)

    qs = [head(hd * HEAD_DIM, gq_ref[...]) * q_scale for hd in range(N_HEADS)]
    q_ref[...] = jnp.concatenate(qs, axis=1).astype(jnp.bfloat16)
    for hd in range(N_KV):
        kt_ref[hd] = head((N_HEADS + hd) * HEAD_DIM, gk_ref[...]).T.astype(jnp.bfloat16)
    v_ref[...] = qkv[:, (N_HEADS + N_KV) * HEAD_DIM:].astype(jnp.bfloat16)


def _layer_in_kernel(*refs, split_in, even, steps):
    refs = list(refs)
    h_ref = refs.pop()
    t = pl.program_id(0)

    @pl.when(t == 0)
    def _():
        h_ref[...] = jnp.zeros(h_ref.shape, h_ref.dtype)

    if split_in:
        x_ref, ctx_ref = refs[:2]
        del refs[:2]
        tile = jnp.minimum(t, steps - 2) % N_TILES
        x = jnp.where(tile == N_LAT_TILES, ctx_ref[...], x_ref[...])
    else:
        x = refs.pop(0)[...]
    mod_ref, g1_ref, wgu_ref, wd_ref, gm_ref = refs[:5]
    n_out = 3 if even else 4
    stage_in = refs[5:-n_out]
    xo_ref, *stage_out = refs[-n_out:]
    w_stage_ref, *stage_in = stage_in
    proj = jnp.dot(h_ref[...], w_stage_ref[...], preferred_element_type=jnp.float32)
    h = _ffn_norm(x, mod_ref, 0, g1_ref[...])
    if even:
        _even_in_stage(proj, *stage_in, *stage_out)
    else:
        _qkv_stage(proj, *stage_in, *stage_out)
    x = _ffn_apply(x, h, mod_ref, 0, wgu_ref, wd_ref)
    xo_ref[...] = x
    h_ref[...] = _modnorm(x, gm_ref[...], mod_ref[3:4, :], mod_ref[4:5, :]).astype(jnp.bfloat16)


def _layer_in(xs, mods, g1, wgu, wd, gm, layer, even, stage_args):
    batch = xs[0].shape[0]
    j = layer // 2
    split_in = len(xs) == 2
    steps = batch * N_TILES + 1

    def cur(t):
        c = jnp.minimum(t, steps - 2)
        return c // N_TILES, c % N_TILES

    def prev(t):
        c = jnp.maximum(t - 1, 0)
        return c // N_TILES, c % N_TILES

    def cur_tile(width):
        return pl.BlockSpec((None, TM, width), lambda t: (*cur(t), 0))

    def prev_tile(width):
        return pl.BlockSpec((None, TM, width), lambda t: (*prev(t), 0))

    def const(block_shape, *index):
        return _resident(block_shape, lambda t: index)

    def mod_map(t):
        b, i = cur(t)
        return jnp.where(i == N_LAT_TILES, batch, b), 0, 0

    if split_in:
        def x_map(t):
            b, i = cur(t)
            return b, jnp.minimum(i, N_LAT_TILES - 1), 0
        x_specs = [pl.BlockSpec((None, TM, D), x_map),
                   pl.BlockSpec((None, CTX, D), lambda t: (cur(t)[0], 0, 0))]
    else:
        x_specs = [cur_tile(D)]
    in_specs = x_specs + [
        pl.BlockSpec((None, N_MOD, D), mod_map),
        const((None, 1, D), layer, 0, 0),
        const((None, D, 2 * D_FF), layer, 0, 0),
        const((None, D_FF, D), layer, 0, 0),
        const((None, 1, D), layer, 0, 0),
    ]
    out_specs = [cur_tile(D)]
    out_shape = [jax.ShapeDtypeStruct((batch, ROWS, D), jnp.float32)]
    if even:
        in_specs += [
            const((None, D, D_FOURIER + 2 * D_SGU), j, 0, 0),
            const((GROUP_CH, 2 * GROUP_CH), 0, 0),
            const((None, 1, D_SGU), j, 0, 0),
            const((None, N_GROUPS, CHUNK, CHUNK), j, 0, 0, 0),
            const((None, CHUNK, D_SGU), j, 0, 0),
        ]
        out_specs += [prev_tile(2 * D_FOURIER), prev_tile(D_SGU)]
        out_shape += [jax.ShapeDtypeStruct((batch, ROWS, 2 * D_FOURIER), jnp.bfloat16),
                      jax.ShapeDtypeStruct((batch, ROWS, D_SGU), jnp.bfloat16)]
    else:
        in_specs += [
            const((None, D, D + 2 * KV_W), j, 0, 0),
            const((None, 1, HEAD_DIM), j, 0, 0),
            const((None, 1, HEAD_DIM), j, 0, 0),
            pl.BlockSpec((TM, HEAD_DIM), lambda t: (prev(t)[1], 0)),
            pl.BlockSpec((TM, HEAD_DIM), lambda t: (prev(t)[1], 0)),
        ]
        out_specs += [
            prev_tile(D),
            pl.BlockSpec((None, N_KV, HEAD_DIM, TM), lambda t: (prev(t)[0], 0, 0, prev(t)[1])),
            prev_tile(KV_W),
        ]
        out_shape += [jax.ShapeDtypeStruct((batch, ROWS, D), jnp.bfloat16),
                      jax.ShapeDtypeStruct((batch, N_KV, HEAD_DIM, ROWS), jnp.bfloat16),
                      jax.ShapeDtypeStruct((batch, ROWS, KV_W), jnp.bfloat16)]
    return pl.pallas_call(
        functools.partial(_layer_in_kernel, split_in=split_in, even=even, steps=steps),
        grid=(steps,),
        in_specs=in_specs,
        out_specs=out_specs,
        out_shape=out_shape,
        scratch_shapes=[pltpu.VMEM((TM, D), jnp.bfloat16)],
        compiler_params=_cparams(1),
        name="layer_in_even" if even else "layer_in_odd",
    )(*xs, mods, g1, wgu, wd, gm, *stage_args)


def _layer_out_kernel(x_ref, mod_ref, wo_ref, *refs, n_y, split_y, final):
    y_refs = refs[:n_y]
    g2_ref, wgu_ref, wd_ref = refs[n_y:n_y + 3]
    o_ref = refs[-1]
    if split_y:
        y = jnp.where(pl.program_id(1) == N_LAT_TILES, y_refs[1][...], y_refs[0][...])
    else:
        y = y_refs[0][...] if n_y == 1 else jnp.concatenate([r[...] for r in y_refs], axis=1)
    x = x_ref[...] + mod_ref[5:6, :] * jnp.dot(y, wo_ref[...], preferred_element_type=jnp.float32)
    x = _ffn_apply(x, _ffn_norm(x, mod_ref, 6, g2_ref[...]), mod_ref, 6, wgu_ref, wd_ref)
    if final:
        gf_ref = refs[-2]
        ms = jnp.mean(x * x, axis=-1, keepdims=True)
        x = x * lax.rsqrt(ms + EPS) * gf_ref[...]
    o_ref[...] = x


def _layer_out(xc, mods, wo, ys, g2, wgu, wd, layer, n_tiles, g_final=None, split_y=False):
    batch = xc.shape[0]
    j = layer // 2
    final = g_final is not None
    in_specs = [_tile_spec(D), _mod_spec(), _resident((None, D, D), lambda b, i: (j, 0, 0))]
    if split_y:
        in_specs += [
            pl.BlockSpec((None, TM, D), lambda b, i: (b, jnp.minimum(i, N_LAT_TILES - 1), 0)),
            pl.BlockSpec((None, CTX, D), lambda b, i: (b, 0, 0)),
        ]
    else:
        in_specs += [_tile_spec(y.shape[-1]) for y in ys]
    in_specs += [
        pl.BlockSpec((None, 1, D), lambda b, i: (layer, 0, 0)),
        _resident((None, D, 2 * D_FF), lambda b, i: (layer, 0, 0)),
        _resident((None, D_FF, D), lambda b, i: (layer, 0, 0)),
    ]
    args = [xc, mods, wo, *ys, g2, wgu, wd]
    if final:
        in_specs.append(pl.BlockSpec((1, D), lambda b, i: (0, 0)))
        args.append(g_final)
    return pl.pallas_call(
        functools.partial(_layer_out_kernel, n_y=len(ys), split_y=split_y, final=final),
        grid=(batch, n_tiles),
        in_specs=in_specs,
        out_specs=_tile_spec(D),
        out_shape=jax.ShapeDtypeStruct((batch, n_tiles * TM, D), jnp.float32),
        compiler_params=_cparams(2),
        name="layer_out_final" if final else "layer_out",
    )(*args)


HALF = SEQ // 2
FT = MXU_DIM
FW = MXU_DIM
N_FOLD_TILES = HALF // FT
DFT_SCALE = SEQ ** -0.5


def _dft_kernel(ch_ref, sh_ref, rev_ref, cc_ref, sc_ref, p_ref, q_ref, o_ref):
    rev = rev_ref[...]
    row0 = lax.broadcasted_iota(jnp.int32, (FT, FW), 0) == 0
    alt = (1 - 2 * (lax.broadcasted_iota(jnp.int32, (FT, FW), 0) & 1)).astype(jnp.float32)

    def reversed_upper(x_ref, t):
        src = HALF + FT * (N_FOLD_TILES - 1 - t)
        r = jnp.dot(rev, x_ref[src:src + FT, :], preferred_element_type=jnp.float32)
        if t == 0:
            return r
        first = HALF + FT * (N_FOLD_TILES - t)
        return jnp.where(row0, x_ref[first:first + 1, :].astype(jnp.float32), r)

    pfs, qfs, alt_sum = [], [], None
    for t in range(N_FOLD_TILES):
        rows = slice(t * FT, (t + 1) * FT)
        pf = p_ref[rows, :].astype(jnp.float32) + reversed_upper(p_ref, t)
        qf = q_ref[rows, :].astype(jnp.float32) - reversed_upper(q_ref, t)
        alt_sum = pf if alt_sum is None else alt_sum + pf
        pfs.append(pf.astype(jnp.bfloat16))
        qfs.append(qf.astype(jnp.bfloat16))
    p_mid = p_ref[HALF:HALF + 1, :].astype(jnp.float32)
    sign = jnp.concatenate([alt] * N_FOLD_TILES, axis=0)
    e = jnp.dot(ch_ref[...], jnp.concatenate(pfs, axis=0), preferred_element_type=jnp.float32)
    e = e + sign * (DFT_SCALE * p_mid)
    o = jnp.dot(sh_ref[...], jnp.concatenate(qfs, axis=0), preferred_element_type=jnp.float32)
    o_ref[0:HALF, :] = (e + o).astype(jnp.bfloat16)
    g = (e - o).astype(jnp.bfloat16)
    mid = DFT_SCALE * (jnp.sum(alt_sum * alt, axis=0, keepdims=True) + p_mid)
    for t in range(N_FOLD_TILES):
        src = FT * (N_FOLD_TILES - 1 - t)
        up = jnp.dot(rev, g[src:src + FT, :], preferred_element_type=jnp.float32)
        first = mid if t == 0 else g[src + FT:src + FT + 1, :].astype(jnp.float32)
        o_ref[HALF + t * FT:HALF + (t + 1) * FT, :] = jnp.where(row0, first, up).astype(jnp.bfloat16)
    ctx = jnp.dot(cc_ref[...], p_ref[SEQ:, :], preferred_element_type=jnp.float32)
    ctx = ctx + jnp.dot(sc_ref[...], q_ref[SEQ:, :], preferred_element_type=jnp.float32)
    o_ref[SEQ:, :] = ctx.astype(jnp.bfloat16)


def _dft(tables, pq):
    ch, sh, rev, cc, sc = tables
    batch = pq.shape[0]
    n_col = D_FOURIER // FW
    return pl.pallas_call(
        _dft_kernel,
        grid=(batch, n_col),
        in_specs=[
            _resident((HALF, HALF), lambda b, c: (0, 0)),
            _resident((HALF, HALF), lambda b, c: (0, 0)),
            _resident((FT, FT), lambda b, c: (0, 0)),
            _resident((CTX, CTX), lambda b, c: (0, 0)),
            _resident((CTX, CTX), lambda b, c: (0, 0)),
            pl.BlockSpec((None, ROWS, FW), lambda b, c: (b, 0, c)),
            pl.BlockSpec((None, ROWS, FW), lambda b, c: (b, 0, n_col + c)),
        ],
        out_specs=pl.BlockSpec((None, ROWS, FW), lambda b, c: (b, 0, c)),
        out_shape=jax.ShapeDtypeStruct((batch, ROWS, D_FOURIER), jnp.bfloat16),
        compiler_params=_cparams(2),
        name="dft",
    )(ch, sh, rev, cc, sc, pq, pq)


KEY_TILE = MXU_DIM
N_KEY_TILES = ROWS // KEY_TILE


def _with_row_sums(v):
    return jnp.concatenate([v, jnp.ones_like(v)], axis=1)


def _normalised(pv):
    return (pv[:, :HEAD_DIM] / pv[:, HEAD_DIM:]).astype(jnp.bfloat16)


def _attn_kernel(q_ref, kt_ref, v_ref, o_ref, s_ref, m_ref):
    @pl.when(pl.program_id(0) == 0)
    def _():
        s_ref[...] = jnp.zeros(s_ref.shape, s_ref.dtype)
        m_ref[...] = jnp.zeros(m_ref.shape, m_ref.dtype)

    outs = []
    for hd in range(GROUP):
        q = q_ref[:, hd * HEAD_DIM:(hd + 1) * HEAD_DIM]
        m_prev = m_ref[hd]
        m_lane = None
        acc = None
        for j in range(N_KEY_TILES):
            keys = slice(j * KEY_TILE, (j + 1) * KEY_TILE)
            s_old = s_ref[hd, :, keys]
            p = jnp.exp2(s_old - jnp.concatenate([m_prev] * (KEY_TILE // HEAD_DIM), axis=1))
            pv = jnp.dot(p.astype(jnp.bfloat16), _with_row_sums(v_ref[keys, :]),
                         preferred_element_type=jnp.float32)
            acc = pv if acc is None else acc + pv
            s_new = jnp.dot(q, kt_ref[:, keys], preferred_element_type=jnp.float32)
            s_ref[hd, :, keys] = s_new
            for c in range(KEY_TILE // HEAD_DIM):
                part = s_new[:, c * HEAD_DIM:(c + 1) * HEAD_DIM]
                m_lane = part if m_lane is None else jnp.maximum(m_lane, part)
        outs.append(_normalised(acc))
        m_ref[hd] = jnp.broadcast_to(jnp.max(m_lane, axis=-1, keepdims=True), m_lane.shape)
    o_ref[...] = jnp.concatenate(outs, axis=1)


def _attention_latent(q, kt, v):
    batch = q.shape[0]
    gw = GROUP * HEAD_DIM
    per_sample = N_KV * N_LAT_TILES
    steps = batch * per_sample + 1

    def unflatten(c):
        return c // per_sample, (c % per_sample) // N_LAT_TILES, c % N_LAT_TILES

    def cur(t):
        return unflatten(jnp.minimum(t, steps - 2))

    def prev(t):
        return unflatten(jnp.maximum(t - 1, 0))

    def q_map(t):
        b, g, i = cur(t)
        return b, i, g

    def kt_map(t):
        b, g, _ = cur(t)
        return b, g, 0, 0

    def v_map(t):
        b, g, _ = prev(t)
        return b, 0, g

    def o_map(t):
        b, g, i = prev(t)
        return b, i, g

    return pl.pallas_call(
        _attn_kernel,
        grid=(steps,),
        in_specs=[
            pl.BlockSpec((None, TM, gw), q_map),
            pl.BlockSpec((None, None, HEAD_DIM, ROWS), kt_map),
            pl.BlockSpec((None, ROWS, HEAD_DIM), v_map),
        ],
        out_specs=pl.BlockSpec((None, TM, gw), o_map),
        out_shape=jax.ShapeDtypeStruct((batch, SEQ, D), jnp.bfloat16),
        scratch_shapes=[pltpu.VMEM((GROUP, TM, ROWS), jnp.float32),
                        pltpu.VMEM((GROUP, TM, HEAD_DIM), jnp.float32)],
        compiler_params=_cparams(1),
        name="attention",
    )(q, kt, v)


def _attn_ctx_kernel(q_ref, kt_ref, v_ref, o_ref):
    v1 = _with_row_sums(v_ref[...])
    kt = kt_ref[...]
    outs = []
    for hd in range(GROUP):
        s = jnp.dot(q_ref[:, hd * HEAD_DIM:(hd + 1) * HEAD_DIM], kt,
                    preferred_element_type=jnp.float32)
        p = jnp.exp2(s - jnp.max(s, axis=-1, keepdims=True)).astype(jnp.bfloat16)
        outs.append(_normalised(jnp.dot(p, v1, preferred_element_type=jnp.float32)))
    o_ref[...] = jnp.concatenate(outs, axis=1)


def _attention_context(q, kt, v):
    batch = q.shape[0]
    gw = GROUP * HEAD_DIM
    return pl.pallas_call(
        _attn_ctx_kernel,
        grid=(batch, N_KV),
        in_specs=[
            pl.BlockSpec((None, CTX, gw), lambda b, g: (b, SEQ // CTX, g)),
            pl.BlockSpec((None, None, HEAD_DIM, CTX), lambda b, g: (b, g, 0, SEQ // CTX)),
            pl.BlockSpec((None, CTX, HEAD_DIM), lambda b, g: (b, SEQ // CTX, g)),
        ],
        out_specs=pl.BlockSpec((None, CTX, gw), lambda b, g: (b, 0, g)),
        out_shape=jax.ShapeDtypeStruct((batch, CTX, D), jnp.bfloat16),
        compiler_params=_cparams(2),
        name="attention_ctx",
    )(q, kt, v)


def _cos_sin_products(n, size):
    r = 1 << (int(math.log2(size)) // 2)
    hi = size // r
    j = jnp.arange(size, dtype=jnp.int32)[:, None]
    ang_hi = ((j * jnp.arange(hi, dtype=jnp.int32)[None, :] * r) % n).astype(jnp.float32)
    ang_lo = ((j * jnp.arange(r, dtype=jnp.int32)[None, :]) % n).astype(jnp.float32)
    w = 2.0 * math.pi / n
    ca, sa = jnp.cos(ang_hi * w)[:, :, None], jnp.sin(ang_hi * w)[:, :, None]
    cb, sb = jnp.cos(ang_lo * w)[:, None, :], jnp.sin(ang_lo * w)[:, None, :]
    scale = n ** -0.5
    cos = ((ca * cb - sa * sb) * scale).reshape(size, size)
    sin = ((sa * cb + ca * sb) * scale).reshape(size, size)
    return cos, sin


def _dft_tables():
    ch, sh = _cos_sin_products(SEQ, HALF)
    cc, sc = _cos_sin_products(CTX, CTX)
    r = jnp.arange(FT, dtype=jnp.int32)
    rev = ((r[:, None] + r[None, :]) == FT).astype(jnp.bfloat16)
    return tuple(t.astype(jnp.bfloat16) for t in (ch, sh, rev, cc, sc))


def _channel_dft():
    c, s = _cos_sin_products(GROUP_CH, GROUP_CH)
    return jnp.concatenate([c, -s], axis=1).astype(jnp.bfloat16)


def _rope_tables():
    half = HEAD_DIM // 2
    rows = SEQ // GRID_W
    row = jnp.repeat(jnp.arange(rows, dtype=jnp.float32), GRID_W)
    col = jnp.tile(jnp.arange(GRID_W, dtype=jnp.float32), rows)
    inv_freq = ROPE_THETA ** (-jnp.arange(0, half, 2, dtype=jnp.float32) / half)
    ang_row = row[:, None] * inv_freq
    ang_col = col[:, None] * inv_freq
    cos = jnp.concatenate([jnp.cos(ang_row)] * 2 + [jnp.cos(ang_col)] * 2, axis=1)
    sin = jnp.concatenate([-jnp.sin(ang_row), jnp.sin(ang_row),
                           -jnp.sin(ang_col), jnp.sin(ang_col)], axis=1)
    cos = jnp.concatenate([cos, jnp.ones((CTX, HEAD_DIM), jnp.float32)], axis=0)
    sin = jnp.concatenate([sin, jnp.zeros((CTX, HEAD_DIM), jnp.float32)], axis=0)
    return cos, sin


def kernel(x, c, ctx, c_ctx, w_mod, b_mod, g_ffn1, w_ffn1_gu, w_ffn1_down, g_mix, g_ffn2, w_ffn2_gu, w_ffn2_down, g_final, w_in_ab, g_v, w_s, b_s, w_out_ab, w_qkv, g_q, g_k, w_o):
    batch = x.shape[0]
    depth = w_mod.shape[0]
    assert x.shape == (batch, SEQ, D) and ctx.shape == (batch, CTX, D)
    assert depth % 2 == 0

    mod_rows = -(-(batch + 1) // 8) * 8
    cc = jnp.concatenate([c, c_ctx[None], jnp.zeros((mod_rows - batch - 1, D), jnp.float32)], axis=0)
    m = _modulation(cc, w_mod, b_mod)
    m = m[:, :batch + 1].reshape(depth, batch + 1, N_MOD, D)

    wgu1, wd1 = w_ffn1_gu.astype(jnp.bfloat16), w_ffn1_down.astype(jnp.bfloat16)
    wgu2, wd2 = w_ffn2_gu.astype(jnp.bfloat16), w_ffn2_down.astype(jnp.bfloat16)
    g1 = g_ffn1.reshape(depth, 1, D)
    g2 = g_ffn2.reshape(depth, 1, D)
    gm = g_mix.reshape(depth, 1, D)
    w_in = w_in_ab.astype(jnp.bfloat16)
    w_out = w_out_ab.astype(jnp.bfloat16)
    ws = w_s.astype(jnp.bfloat16)
    gv = g_v.reshape(-1, 1, D_SGU)
    bs_full = jnp.repeat(jnp.swapaxes(b_s, 1, 2), GROUP_CH, axis=2)
    wqkv = w_qkv.astype(jnp.bfloat16)
    wo = w_o.astype(jnp.bfloat16)
    gq = g_q.reshape(-1, 1, HEAD_DIM)
    gk = g_k.reshape(-1, 1, HEAD_DIM)
    dft_tables = _dft_tables()
    ccs = _channel_dft()
    cos_t, sin_t = _rope_tables()

    xs = (x, ctx)
    for l in range(depth):
        last = l == depth - 1
        even = l % 2 == 0
        mods = m[l]
        n_out = N_LAT_TILES if last else N_TILES
        if even:
            xc, pq, sgu = _layer_in(xs, mods, g1, wgu1, wd1, gm, l, True,
                                    (w_in, ccs, gv, ws, bs_full))
            ys, w_proj = [_dft(dft_tables, pq), sgu], w_out
        else:
            xc, q, kt, v = _layer_in(xs, mods, g1, wgu1, wd1, gm, l, False,
                                     (wqkv, gq, gk, cos_t, sin_t))
            ys, w_proj = [_attention_latent(q, kt, v)], wo
            if not last:
                ys.append(_attention_context(q, kt, v))
        xc = _layer_out(xc, mods, w_proj, ys, g2, wgu2, wd2, l, n_out,
                        g_final.reshape(1, D) if last else None,
                        split_y=not even and not last)
        xs = (xc,)
    return xc
```

```python
import functools
import math

import jax
import jax.numpy as jnp
from jax import lax
from jax.experimental import pallas as pl
from jax.experimental.pallas import tpu as pltpu

D = 1024
SEQ = 4096
CTX = 256
ROWS = SEQ + CTX
GRID_W = 64
D_FF = 2752
N_MOD = 9
EPS = 1e-6
HEAD_DIM = 128
N_HEADS = 8
N_KV = 2
GROUP = N_HEADS // N_KV
KV_W = N_KV * HEAD_DIM
ROPE_THETA = 10000.0
D_FOURIER = 512
D_SGU = 512
GROUP_CH = 128
N_GROUPS = 4
CHUNK = 128

MXU_DIM = 256
TM = 256
N_LAT_TILES = SEQ // TM
N_TILES = ROWS // TM
VMEM_LIMIT = 56 * 1024 * 1024

assert TM == CTX and SEQ % TM == 0


def _cparams(n_axes):
    return pltpu.CompilerParams(dimension_semantics=("arbitrary",) * n_axes,
                                vmem_limit_bytes=VMEM_LIMIT)


def _resident(block_shape, index_map):
    return pl.BlockSpec(block_shape, index_map, pipeline_mode=pl.Buffered(1))


def _tile_spec(width):
    return pl.BlockSpec((None, TM, width), lambda b, i: (b, i, 0))


def _mod_spec():
    return pl.BlockSpec(
        (None, N_MOD, D),
        lambda b, i: (jnp.where(i == N_LAT_TILES, pl.num_programs(0), b), 0, 0))


def _modnorm(x, g, shift, scale):
    ms = jnp.mean(x * x, axis=-1, keepdims=True)
    y = x * lax.rsqrt(ms + EPS) * g
    return y * (1.0 + scale) + shift


def _ffn_norm(x, mod_ref, row0, g):
    h = _modnorm(x, g, mod_ref[row0:row0 + 1, :], mod_ref[row0 + 1:row0 + 2, :])
    return h.astype(jnp.bfloat16)


def _ffn_apply(x, h, mod_ref, row0, wgu_ref, wd_ref):
    gu = jnp.dot(h, wgu_ref[...], preferred_element_type=jnp.float32)
    gt, up = gu[:, :D_FF], gu[:, D_FF:]
    a = (gt * jax.nn.sigmoid(gt) * up).astype(jnp.bfloat16)
    y = jnp.dot(a, wd_ref[...], preferred_element_type=jnp.float32)
    return x + 0.5 * mod_ref[row0 + 2:row0 + 3, :] * y


def _mod_kernel(cc_ref, w_ref, b_ref, o_ref):
    s = cc_ref[...]
    s = s * jax.nn.sigmoid(s)
    s_hi = s.astype(jnp.bfloat16)
    s_lo = (s - s_hi.astype(jnp.float32)).astype(jnp.bfloat16)
    w = w_ref[...]
    w_hi = w.astype(jnp.bfloat16)
    w_lo = (w - w_hi.astype(jnp.float32)).astype(jnp.bfloat16)
    rows = s.shape[0]
    both = jnp.dot(jnp.concatenate([s_hi, s_lo], axis=0), w_hi,
                   preferred_element_type=jnp.float32)
    cross = jnp.dot(s_hi, w_lo, preferred_element_type=jnp.float32)
    o_ref[...] = both[:rows] + both[rows:] + cross + b_ref[...]


def _modulation(cc, w_mod, b_mod):
    depth = w_mod.shape[0]
    rows = cc.shape[0]
    tn = 1024
    return pl.pallas_call(
        _mod_kernel,
        grid=(depth, N_MOD * D // tn),
        in_specs=[
            pl.BlockSpec((rows, D), lambda l, j: (0, 0)),
            pl.BlockSpec((None, D, tn), lambda l, j: (l, 0, j)),
            pl.BlockSpec((None, 1, tn), lambda l, j: (l, 0, j)),
        ],
        out_specs=pl.BlockSpec((None, rows, tn), lambda l, j: (l, 0, j)),
        out_shape=jax.ShapeDtypeStruct((depth, rows, N_MOD * D), jnp.float32),
        compiler_params=_cparams(2),
        name="modulation",
    )(cc, w_mod, b_mod.reshape(depth, 1, N_MOD * D))


def _even_in_stage(p, ccs_ref, gv_ref, ws_ref, bs_ref, pq_ref, sgu_ref):
    a = p[:, :D_FOURIER].astype(jnp.bfloat16)
    uv = jax.nn.gelu(p[:, D_FOURIER:], approximate=True)
    u = uv[:, :D_SGU]
    v = uv[:, D_SGU:]
    ccs = ccs_ref[...]
    ps, qs, gated = [], [], []
    n_chunks = TM // CHUNK
    for grp in range(N_GROUPS):
        lo, hi = grp * GROUP_CH, (grp + 1) * GROUP_CH
        t = jnp.dot(a[:, lo:hi], ccs, preferred_element_type=jnp.float32)
        ps.append(t[:, :GROUP_CH])
        qs.append(t[:, GROUP_CH:])
        vg = v[:, lo:hi]
        ms = jnp.mean(vg * vg, axis=-1, keepdims=True)
        vh = (vg * lax.rsqrt(ms + EPS) * gv_ref[:, lo:hi]).astype(jnp.bfloat16)
        rhs = jnp.concatenate([vh[c * CHUNK:(c + 1) * CHUNK, :] for c in range(n_chunks)], axis=1)
        mixed = jnp.dot(ws_ref[grp], rhs, preferred_element_type=jnp.float32)
        mixed = jnp.concatenate(
            [mixed[:, c * GROUP_CH:(c + 1) * GROUP_CH] for c in range(n_chunks)], axis=0)
        bias = jnp.concatenate([bs_ref[:, lo:hi]] * n_chunks, axis=0)
        gated.append(u[:, lo:hi] * (mixed + bias))
    pq_ref[...] = jnp.concatenate(ps + qs, axis=1).astype(jnp.bfloat16)
    sgu_ref[...] = jnp.concatenate(gated, axis=1).astype(jnp.bfloat16)


def _rope(t, cos, sin_signed, first_half):
    partner = jnp.where(first_half, pltpu.roll(t, 3 * HEAD_DIM // 4, 1),
                        pltpu.roll(t, HEAD_DIM // 4, 1))
    return t * cos + partner * sin_signed


def _qkv_stage(qkv, gq_ref, gk_ref, cos_ref, sin_ref, q_ref, kt_ref, v_ref):
    cos = cos_ref[...]
    sin = sin_ref[...]
    lane = lax.broadcasted_iota(jnp.int32, (TM, HEAD_DIM), 1)
    first_half = (lane % (HEAD_DIM // 2)) < (HEAD_DIM // 4)
    q_scale = (HEAD_DIM ** -0.5) * math.log2(math.e)

    def head(col, gain):
        t = qkv[:, col:col + HEAD_DIM]
        ms = jnp.mean(t * t, axis=-1, keepdims=True)
        return _rope(t * lax.rsqrt(ms + EPS) * gain, cos, sin, first_half)

    qs = [head(hd * HEAD_DIM, gq_ref[...]) * q_scale for hd in range(N_HEADS)]
    q_ref[...] = jnp.concatenate(qs, axis=1).astype(jnp.bfloat16)
    for hd in range(N_KV):
        kt_ref[hd] = head((N_HEADS + hd) * HEAD_DIM, gk_ref[...]).T.astype(jnp.bfloat16)
    v_ref[...] = qkv[:, (N_HEADS + N_KV) * HEAD_DIM:].astype(jnp.bfloat16)


def _layer_in_kernel(*refs, split_in, even, steps):
    refs = list(refs)
    h_ref = refs.pop()
    t = pl.program_id(0)

    @pl.when(t == 0)
    def _():
        h_ref[...] = jnp.zeros(h_ref.shape, h_ref.dtype)

    if split_in:
        x_ref, ctx_ref = refs[:2]
        del refs[:2]
        tile = jnp.minimum(t, steps - 2) % N_TILES
        x = jnp.where(tile == N_LAT_TILES, ctx_ref[...], x_ref[...])
    else:
        x = refs.pop(0)[...]
    mod_ref, g1_ref, wgu_ref, wd_ref, gm_ref = refs[:5]
    n_out = 3 if even else 4
    stage_in = refs[5:-n_out]
    xo_ref, *stage_out = refs[-n_out:]
    w_stage_ref, *stage_in = stage_in
    proj = jnp.dot(h_ref[...], w_stage_ref[...], preferred_element_type=jnp.float32)
    h = _ffn_norm(x, mod_ref, 0, g1_ref[...])
    if even:
        _even_in_stage(proj, *stage_in, *stage_out)
    else:
        _qkv_stage(proj, *stage_in, *stage_out)
    x = _ffn_apply(x, h, mod_ref, 0, wgu_ref, wd_ref)
    xo_ref[...] = x
    h_ref[...] = _modnorm(x, gm_ref[...], mod_ref[3:4, :], mod_ref[4:5, :]).astype(jnp.bfloat16)


def _layer_in(xs, mods, g1, wgu, wd, gm, layer, even, stage_args):
    batch = xs[0].shape[0]
    j = layer // 2
    split_in = len(xs) == 2
    steps = batch * N_TILES + 1

    def cur(t):
        c = jnp.minimum(t, steps - 2)
        return c // N_TILES, c % N_TILES

    def prev(t):
        c = jnp.maximum(t - 1, 0)
        return c // N_TILES, c % N_TILES

    def cur_tile(width):
        return pl.BlockSpec((None, TM, width), lambda t: (*cur(t), 0))

    def prev_tile(width):
        return pl.BlockSpec((None, TM, width), lambda t: (*prev(t), 0))

    def const(block_shape, *index):
        return _resident(block_shape, lambda t: index)

    def mod_map(t):
        b, i = cur(t)
        return jnp.where(i == N_LAT_TILES, batch, b), 0, 0

    if split_in:
        def x_map(t):
            b, i = cur(t)
            return b, jnp.minimum(i, N_LAT_TILES - 1), 0
        x_specs = [pl.BlockSpec((None, TM, D), x_map),
                   pl.BlockSpec((None, CTX, D), lambda t: (cur(t)[0], 0, 0))]
    else:
        x_specs = [cur_tile(D)]
    in_specs = x_specs + [
        pl.BlockSpec((None, N_MOD, D), mod_map),
        const((None, 1, D), layer, 0, 0),
        const((None, D, 2 * D_FF), layer, 0, 0),
        const((None, D_FF, D), layer, 0, 0),
        const((None, 1, D), layer, 0, 0),
    ]
    out_specs = [cur_tile(D)]
    out_shape = [jax.ShapeDtypeStruct((batch, ROWS, D), jnp.float32)]
    if even:
        in_specs += [
            const((None, D, D_FOURIER + 2 * D_SGU), j, 0, 0),
            const((GROUP_CH, 2 * GROUP_CH), 0, 0),
            const((None, 1, D_SGU), j, 0, 0),
            const((None, N_GROUPS, CHUNK, CHUNK), j, 0, 0, 0),
            const((None, CHUNK, D_SGU), j, 0, 0),
        ]
        out_specs += [prev_tile(2 * D_FOURIER), prev_tile(D_SGU)]
        out_shape += [jax.ShapeDtypeStruct((batch, ROWS, 2 * D_FOURIER), jnp.bfloat16),
                      jax.ShapeDtypeStruct((batch, ROWS, D_SGU), jnp.bfloat16)]
    else:
        in_specs += [
            const((None, D, D + 2 * KV_W), j, 0, 0),
            const((None, 1, HEAD_DIM), j, 0, 0),
            const((None, 1, HEAD_DIM), j, 0, 0),
            pl.BlockSpec((TM, HEAD_DIM), lambda t: (prev(t)[1], 0)),
            pl.BlockSpec((TM, HEAD_DIM), lambda t: (prev(t)[1], 0)),
        ]
        out_specs += [
            prev_tile(D),
            pl.BlockSpec((None, N_KV, HEAD_DIM, TM), lambda t: (prev(t)[0], 0, 0, prev(t)[1])),
            prev_tile(KV_W),
        ]
        out_shape += [jax.ShapeDtypeStruct((batch, ROWS, D), jnp.bfloat16),
                      jax.ShapeDtypeStruct((batch, N_KV, HEAD_DIM, ROWS), jnp.bfloat16),
                      jax.ShapeDtypeStruct((batch, ROWS, KV_W), jnp.bfloat16)]
    return pl.pallas_call(
        functools.partial(_layer_in_kernel, split_in=split_in, even=even, steps=steps),
        grid=(steps,),
        in_specs=in_specs,
        out_specs=out_specs,
        out_shape=out_shape,
        scratch_shapes=[pltpu.VMEM((TM, D), jnp.bfloat16)],
        compiler_params=_cparams(1),
        name="layer_in_even" if even else "layer_in_odd",
    )(*xs, mods, g1, wgu, wd, gm, *stage_args)


def _per_tile(fn, x, mod_refs):
    return jnp.concatenate([fn(x[i * TM:(i + 1) * TM], m) for i, m in enumerate(mod_refs)],
                           axis=0)


def _layer_out_kernel(x_ref, *refs, n_mod, n_y, split_y, final):
    mod_refs = refs[:n_mod] * (2 // n_mod)
    wo_ref = refs[n_mod]
    y_refs = refs[n_mod + 1:n_mod + 1 + n_y]
    g2_ref, wgu_ref, wd_ref = refs[n_mod + 1 + n_y:n_mod + 4 + n_y]
    o_ref = refs[-1]
    if split_y:
        tile0 = 2 * pl.program_id(0)
        y = jnp.concatenate(
            [jnp.where((tile0 + i) % N_TILES == N_LAT_TILES, y_refs[2 * i + 1][...],
                       y_refs[2 * i][...]) for i in range(2)], axis=0)
    else:
        y = y_refs[0][...] if n_y == 1 else jnp.concatenate([r[...] for r in y_refs], axis=1)
    proj = jnp.dot(y, wo_ref[...], preferred_element_type=jnp.float32)
    x = x_ref[...] + _per_tile(lambda p, m: m[5:6, :] * p, proj, mod_refs)
    g2 = g2_ref[...]
    h = _per_tile(lambda t, m: _ffn_norm(t, m, 6, g2), x, mod_refs)
    gu = jnp.dot(h, wgu_ref[...], preferred_element_type=jnp.float32)
    gt, up = gu[:, :D_FF], gu[:, D_FF:]
    a = (gt * jax.nn.sigmoid(gt) * up).astype(jnp.bfloat16)
    y2 = jnp.dot(a, wd_ref[...], preferred_element_type=jnp.float32)
    x = x + _per_tile(lambda t, m: 0.5 * m[8:9, :] * t, y2, mod_refs)
    if final:
        gf_ref = refs[-2]
        ms = jnp.mean(x * x, axis=-1, keepdims=True)
        x = x * lax.rsqrt(ms + EPS) * gf_ref[...]
    o_ref[...] = x


def _layer_out(xc, mods, wo, ys, g2, wgu, wd, layer, n_tiles, g_final=None, split_y=False):
    batch = xc.shape[0]
    j = layer // 2
    final = g_final is not None
    rows2 = 2 * TM

    def const(block_shape, *index):
        return _resident(block_shape, lambda *_: index)

    if n_tiles == N_LAT_TILES:
        grid = (batch, N_LAT_TILES // 2)
        x_arr, x_spec = xc, pl.BlockSpec((None, rows2, D), lambda b, k: (b, k, 0))
        mod_specs = [pl.BlockSpec((None, N_MOD, D), lambda b, k: (b, 0, 0))]
        y_arrs = list(ys)
        y_specs = [pl.BlockSpec((None, rows2, y.shape[-1]), lambda b, k: (b, k, 0)) for y in ys]
        out_spec = pl.BlockSpec((None, rows2, D), lambda b, k: (b, k, 0))
        out_shape = jax.ShapeDtypeStruct((batch, SEQ, D), jnp.float32)
    else:
        assert (batch * N_TILES) % 2 == 0
        grid = (batch * N_TILES // 2,)
        x_arr, x_spec = xc.reshape(batch * ROWS, D), pl.BlockSpec((rows2, D), lambda k: (k, 0))

        def tile(k, i):
            t = 2 * k + i
            return t // N_TILES, t % N_TILES

        def mod_spec(i):
            def index(k):
                b, r = tile(k, i)
                return jnp.where(r == N_LAT_TILES, batch, b), 0, 0
            return pl.BlockSpec((None, N_MOD, D), index)

        mod_specs = [mod_spec(0), mod_spec(1)]
        if split_y:
            y_arrs, y_specs = [], []
            for i in range(2):
                def lat(k, i=i):
                    b, r = tile(k, i)
                    return b, jnp.minimum(r, N_LAT_TILES - 1), 0
                y_arrs += list(ys)
                y_specs += [pl.BlockSpec((None, TM, D), lat),
                            pl.BlockSpec((None, CTX, D), lambda k, i=i: (tile(k, i)[0], 0, 0))]
        else:
            y_arrs = [y.reshape(batch * ROWS, y.shape[-1]) for y in ys]
            y_specs = [pl.BlockSpec((rows2, y.shape[-1]), lambda k: (k, 0)) for y in y_arrs]
        out_spec = pl.BlockSpec((rows2, D), lambda k: (k, 0))
        out_shape = jax.ShapeDtypeStruct((batch * ROWS, D), jnp.float32)
    in_specs = [x_spec] + mod_specs + [const((None, D, D), j, 0, 0)] + y_specs + [
        const((None, 1, D), layer, 0, 0),
        const((None, D, 2 * D_FF), layer, 0, 0),
        const((None, D_FF, D), layer, 0, 0),
    ]
    args = [x_arr] + [mods] * len(mod_specs) + [wo] + y_arrs + [g2, wgu, wd]
    if final:
        in_specs.append(const((1, D), 0, 0))
        args.append(g_final)
    out = pl.pallas_call(
        functools.partial(_layer_out_kernel, n_mod=len(mod_specs), n_y=len(y_arrs),
                          split_y=split_y, final=final),
        grid=grid,
        in_specs=in_specs,
        out_specs=out_spec,
        out_shape=out_shape,
        compiler_params=_cparams(len(grid)),
        name="layer_out_final" if final else "layer_out",
    )(*args)
    return out.reshape(batch, -1, D)


HALF = SEQ // 2
FT = MXU_DIM
FW = MXU_DIM
N_FOLD_TILES = HALF // FT
DFT_SCALE = SEQ ** -0.5


def _dft_kernel(ch_ref, sh_ref, rev_ref, cc_ref, sc_ref, p_ref, q_ref, o_ref):
    rev = rev_ref[...]
    row0 = lax.broadcasted_iota(jnp.int32, (FT, FW), 0) == 0
    alt = (1 - 2 * (lax.broadcasted_iota(jnp.int32, (FT, FW), 0) & 1)).astype(jnp.float32)

    def reversed_upper(x_ref, t):
        src = HALF + FT * (N_FOLD_TILES - 1 - t)
        r = jnp.dot(rev, x_ref[src:src + FT, :], preferred_element_type=jnp.float32)
        if t == 0:
            return r
        first = HALF + FT * (N_FOLD_TILES - t)
        return jnp.where(row0, x_ref[first:first + 1, :].astype(jnp.float32), r)

    pfs, qfs, alt_sum = [], [], None
    for t in range(N_FOLD_TILES):
        rows = slice(t * FT, (t + 1) * FT)
        pf = p_ref[rows, :].astype(jnp.float32) + reversed_upper(p_ref, t)
        qf = q_ref[rows, :].astype(jnp.float32) - reversed_upper(q_ref, t)
        alt_sum = pf if alt_sum is None else alt_sum + pf
        pfs.append(pf.astype(jnp.bfloat16))
        qfs.append(qf.astype(jnp.bfloat16))
    p_mid = p_ref[HALF:HALF + 1, :].astype(jnp.float32)
    sign = jnp.concatenate([alt] * N_FOLD_TILES, axis=0)
    e = jnp.dot(ch_ref[...], jnp.concatenate(pfs, axis=0), preferred_element_type=jnp.float32)
    e = e + sign * (DFT_SCALE * p_mid)
    o = jnp.dot(sh_ref[...], jnp.concatenate(qfs, axis=0), preferred_element_type=jnp.float32)
    o_ref[0:HALF, :] = (e + o).astype(jnp.bfloat16)
    g = (e - o).astype(jnp.bfloat16)
    mid = DFT_SCALE * (jnp.sum(alt_sum * alt, axis=0, keepdims=True) + p_mid)
    for t in range(N_FOLD_TILES):
        src = FT * (N_FOLD_TILES - 1 - t)
        up = jnp.dot(rev, g[src:src + FT, :], preferred_element_type=jnp.float32)
        first = mid if t == 0 else g[src + FT:src + FT + 1, :].astype(jnp.float32)
        o_ref[HALF + t * FT:HALF + (t + 1) * FT, :] = jnp.where(row0, first, up).astype(jnp.bfloat16)
    ctx = jnp.dot(cc_ref[...], p_ref[SEQ:, :], preferred_element_type=jnp.float32)
    ctx = ctx + jnp.dot(sc_ref[...], q_ref[SEQ:, :], preferred_element_type=jnp.float32)
    o_ref[SEQ:, :] = ctx.astype(jnp.bfloat16)


def _dft(tables, pq):
    ch, sh, rev, cc, sc = tables
    batch = pq.shape[0]
    n_col = D_FOURIER // FW
    return pl.pallas_call(
        _dft_kernel,
        grid=(batch, n_col),
        in_specs=[
            _resident((HALF, HALF), lambda b, c: (0, 0)),
            _resident((HALF, HALF), lambda b, c: (0, 0)),
            _resident((FT, FT), lambda b, c: (0, 0)),
            _resident((CTX, CTX), lambda b, c: (0, 0)),
            _resident((CTX, CTX), lambda b, c: (0, 0)),
            pl.BlockSpec((None, ROWS, FW), lambda b, c: (b, 0, c)),
            pl.BlockSpec((None, ROWS, FW), lambda b, c: (b, 0, n_col + c)),
        ],
        out_specs=pl.BlockSpec((None, ROWS, FW), lambda b, c: (b, 0, c)),
        out_shape=jax.ShapeDtypeStruct((batch, ROWS, D_FOURIER), jnp.bfloat16),
        compiler_params=_cparams(2),
        name="dft",
    )(ch, sh, rev, cc, sc, pq, pq)


KEY_TILE = MXU_DIM
N_KEY_TILES = ROWS // KEY_TILE


def _with_row_sums(v):
    return jnp.concatenate([v, jnp.ones_like(v)], axis=1)


def _normalised(pv):
    return (pv[:, :HEAD_DIM] / pv[:, HEAD_DIM:]).astype(jnp.bfloat16)


def _attn_kernel(q_ref, kt_ref, v_ref, o_ref, s_ref, m_ref):
    @pl.when(pl.program_id(0) == 0)
    def _():
        s_ref[...] = jnp.zeros(s_ref.shape, s_ref.dtype)
        m_ref[...] = jnp.zeros(m_ref.shape, m_ref.dtype)

    outs = []
    for hd in range(GROUP):
        q = q_ref[:, hd * HEAD_DIM:(hd + 1) * HEAD_DIM]
        m_prev = m_ref[hd]
        m_lane = None
        acc = None
        for j in range(N_KEY_TILES):
            keys = slice(j * KEY_TILE, (j + 1) * KEY_TILE)
            s_old = s_ref[hd, :, keys]
            p = jnp.exp2(s_old - jnp.concatenate([m_prev] * (KEY_TILE // HEAD_DIM), axis=1))
            pv = jnp.dot(p.astype(jnp.bfloat16), _with_row_sums(v_ref[keys, :]),
                         preferred_element_type=jnp.float32)
            acc = pv if acc is None else acc + pv
            s_new = jnp.dot(q, kt_ref[:, keys], preferred_element_type=jnp.float32)
            s_ref[hd, :, keys] = s_new
            for c in range(KEY_TILE // HEAD_DIM):
                part = s_new[:, c * HEAD_DIM:(c + 1) * HEAD_DIM]
                m_lane = part if m_lane is None else jnp.maximum(m_lane, part)
        outs.append(_normalised(acc))
        m_ref[hd] = jnp.broadcast_to(jnp.max(m_lane, axis=-1, keepdims=True), m_lane.shape)
    o_ref[...] = jnp.concatenate(outs, axis=1)


def _attention_latent(q, kt, v):
    batch = q.shape[0]
    gw = GROUP * HEAD_DIM
    per_sample = N_KV * N_LAT_TILES
    steps = batch * per_sample + 1

    def unflatten(c):
        return c // per_sample, (c % per_sample) // N_LAT_TILES, c % N_LAT_TILES

    def cur(t):
        return unflatten(jnp.minimum(t, steps - 2))

    def prev(t):
        return unflatten(jnp.maximum(t - 1, 0))

    def q_map(t):
        b, g, i = cur(t)
        return b, i, g

    def kt_map(t):
        b, g, _ = cur(t)
        return b, g, 0, 0

    def v_map(t):
        b, g, _ = prev(t)
        return b, 0, g

    def o_map(t):
        b, g, i = prev(t)
        return b, i, g

    return pl.pallas_call(
        _attn_kernel,
        grid=(steps,),
        in_specs=[
            pl.BlockSpec((None, TM, gw), q_map),
            pl.BlockSpec((None, None, HEAD_DIM, ROWS), kt_map),
            pl.BlockSpec((None, ROWS, HEAD_DIM), v_map),
        ],
        out_specs=pl.BlockSpec((None, TM, gw), o_map),
        out_shape=jax.ShapeDtypeStruct((batch, SEQ, D), jnp.bfloat16),
        scratch_shapes=[pltpu.VMEM((GROUP, TM, ROWS), jnp.float32),
                        pltpu.VMEM((GROUP, TM, HEAD_DIM), jnp.float32)],
        compiler_params=_cparams(1),
        name="attention",
    )(q, kt, v)


def _attn_ctx_kernel(q_ref, kt_ref, v_ref, o_ref):
    v1 = _with_row_sums(v_ref[...])
    kt = kt_ref[...]
    outs = []
    for hd in range(GROUP):
        s = jnp.dot(q_ref[:, hd * HEAD_DIM:(hd + 1) * HEAD_DIM], kt,
                    preferred_element_type=jnp.float32)
        p = jnp.exp2(s - jnp.max(s, axis=-1, keepdims=True)).astype(jnp.bfloat16)
        outs.append(_normalised(jnp.dot(p, v1, preferred_element_type=jnp.float32)))
    o_ref[...] = jnp.concatenate(outs, axis=1)


def _attention_context(q, kt, v):
    batch = q.shape[0]
    gw = GROUP * HEAD_DIM
    return pl.pallas_call(
        _attn_ctx_kernel,
        grid=(batch, N_KV),
        in_specs=[
            pl.BlockSpec((None, CTX, gw), lambda b, g: (b, SEQ // CTX, g)),
            pl.BlockSpec((None, None, HEAD_DIM, CTX), lambda b, g: (b, g, 0, SEQ // CTX)),
            pl.BlockSpec((None, CTX, HEAD_DIM), lambda b, g: (b, SEQ // CTX, g)),
        ],
        out_specs=pl.BlockSpec((None, CTX, gw), lambda b, g: (b, 0, g)),
        out_shape=jax.ShapeDtypeStruct((batch, CTX, D), jnp.bfloat16),
        compiler_params=_cparams(2),
        name="attention_ctx",
    )(q, kt, v)


def _cos_sin_products(n, size):
    r = 1 << (int(math.log2(size)) // 2)
    hi = size // r
    j = jnp.arange(size, dtype=jnp.int32)[:, None]
    ang_hi = ((j * jnp.arange(hi, dtype=jnp.int32)[None, :] * r) % n).astype(jnp.float32)
    ang_lo = ((j * jnp.arange(r, dtype=jnp.int32)[None, :]) % n).astype(jnp.float32)
    w = 2.0 * math.pi / n
    ca, sa = jnp.cos(ang_hi * w)[:, :, None], jnp.sin(ang_hi * w)[:, :, None]
    cb, sb = jnp.cos(ang_lo * w)[:, None, :], jnp.sin(ang_lo * w)[:, None, :]
    scale = n ** -0.5
    cos = ((ca * cb - sa * sb) * scale).reshape(size, size)
    sin = ((sa * cb + ca * sb) * scale).reshape(size, size)
    return cos, sin


def _dft_tables():
    ch, sh = _cos_sin_products(SEQ, HALF)
    cc, sc = _cos_sin_products(CTX, CTX)
    r = jnp.arange(FT, dtype=jnp.int32)
    rev = ((r[:, None] + r[None, :]) == FT).astype(jnp.bfloat16)
    return tuple(t.astype(jnp.bfloat16) for t in (ch, sh, rev, cc, sc))


def _channel_dft():
    c, s = _cos_sin_products(GROUP_CH, GROUP_CH)
    return jnp.concatenate([c, -s], axis=1).astype(jnp.bfloat16)


def _rope_tables():
    half = HEAD_DIM // 2
    rows = SEQ // GRID_W
    row = jnp.repeat(jnp.arange(rows, dtype=jnp.float32), GRID_W)
    col = jnp.tile(jnp.arange(GRID_W, dtype=jnp.float32), rows)
    inv_freq = ROPE_THETA ** (-jnp.arange(0, half, 2, dtype=jnp.float32) / half)
    ang_row = row[:, None] * inv_freq
    ang_col = col[:, None] * inv_freq
    cos = jnp.concatenate([jnp.cos(ang_row)] * 2 + [jnp.cos(ang_col)] * 2, axis=1)
    sin = jnp.concatenate([-jnp.sin(ang_row), jnp.sin(ang_row),
                           -jnp.sin(ang_col), jnp.sin(ang_col)], axis=1)
    cos = jnp.concatenate([cos, jnp.ones((CTX, HEAD_DIM), jnp.float32)], axis=0)
    sin = jnp.concatenate([sin, jnp.zeros((CTX, HEAD_DIM), jnp.float32)], axis=0)
    return cos, sin


def kernel(x, c, ctx, c_ctx, w_mod, b_mod, g_ffn1, w_ffn1_gu, w_ffn1_down, g_mix, g_ffn2, w_ffn2_gu, w_ffn2_down, g_final, w_in_ab, g_v, w_s, b_s, w_out_ab, w_qkv, g_q, g_k, w_o):
    batch = x.shape[0]
    depth = w_mod.shape[0]
    assert x.shape == (batch, SEQ, D) and ctx.shape == (batch, CTX, D)
    assert depth % 2 == 0

    mod_rows = -(-(batch + 1) // 8) * 8
    cc = jnp.concatenate([c, c_ctx[None], jnp.zeros((mod_rows - batch - 1, D), jnp.float32)], axis=0)
    m = _modulation(cc, w_mod, b_mod)
    m = m[:, :batch + 1].reshape(depth, batch + 1, N_MOD, D)

    wgu1, wd1 = w_ffn1_gu.astype(jnp.bfloat16), w_ffn1_down.astype(jnp.bfloat16)
    wgu2, wd2 = w_ffn2_gu.astype(jnp.bfloat16), w_ffn2_down.astype(jnp.bfloat16)
    g1 = g_ffn1.reshape(depth, 1, D)
    g2 = g_ffn2.reshape(depth, 1, D)
    gm = g_mix.reshape(depth, 1, D)
    w_in = w_in_ab.astype(jnp.bfloat16)
    w_out = w_out_ab.astype(jnp.bfloat16)
    ws = w_s.astype(jnp.bfloat16)
    gv = g_v.reshape(-1, 1, D_SGU)
    bs_full = jnp.repeat(jnp.swapaxes(b_s, 1, 2), GROUP_CH, axis=2)
    wqkv = w_qkv.astype(jnp.bfloat16)
    wo = w_o.astype(jnp.bfloat16)
    gq = g_q.reshape(-1, 1, HEAD_DIM)
    gk = g_k.reshape(-1, 1, HEAD_DIM)
    dft_tables = _dft_tables()
    ccs = _channel_dft()
    cos_t, sin_t = _rope_tables()

    xs = (x, ctx)
    for l in range(depth):
        last = l == depth - 1
        even = l % 2 == 0
        mods = m[l]
        n_out = N_LAT_TILES if last else N_TILES
        if even:
            xc, pq, sgu = _layer_in(xs, mods, g1, wgu1, wd1, gm, l, True,
                                    (w_in, ccs, gv, ws, bs_full))
            ys, w_proj = [_dft(dft_tables, pq), sgu], w_out
        else:
            xc, q, kt, v = _layer_in(xs, mods, g1, wgu1, wd1, gm, l, False,
                                     (wqkv, gq, gk, cos_t, sin_t))
            ys, w_proj = [_attention_latent(q, kt, v)], wo
            if not last:
                ys.append(_attention_context(q, kt, v))
        xc = _layer_out(xc, mods, w_proj, ys, g2, wgu2, wd2, l, n_out,
                        g_final.reshape(1, D) if last else None,
                        split_y=not even and not last)
        xs = (xc,)
    return xc
```

```python
import functools
import math

import jax
import jax.numpy as jnp
from jax import lax
from jax.experimental import pallas as pl
from jax.experimental.pallas import tpu as pltpu

D = 1024
SEQ = 4096
CTX = 256
ROWS = SEQ + CTX
GRID_W = 64
D_FF = 2752
N_MOD = 9
EPS = 1e-6
HEAD_DIM = 128
N_HEADS = 8
N_KV = 2
GROUP = N_HEADS // N_KV
KV_W = N_KV * HEAD_DIM
ROPE_THETA = 10000.0
D_FOURIER = 512
D_SGU = 512
GROUP_CH = 128
N_GROUPS = 4
CHUNK = 128

MXU_DIM = 256
TM = 256
N_LAT_TILES = SEQ // TM
N_TILES = ROWS // TM
VMEM_LIMIT = 56 * 1024 * 1024

assert TM == CTX and SEQ % TM == 0


def _cparams(n_axes):
    return pltpu.CompilerParams(dimension_semantics=("arbitrary",) * n_axes,
                                vmem_limit_bytes=VMEM_LIMIT)


def _resident(block_shape, index_map):
    return pl.BlockSpec(block_shape, index_map, pipeline_mode=pl.Buffered(1))


def _modnorm(x, g, shift, scale):
    ms = jnp.mean(x * x, axis=-1, keepdims=True)
    y = x * lax.rsqrt(ms + EPS) * g
    return y * (1.0 + scale) + shift


def _ffn_norm(x, mod_ref, row0, g):
    h = _modnorm(x, g, mod_ref[row0:row0 + 1, :], mod_ref[row0 + 1:row0 + 2, :])
    return h.astype(jnp.bfloat16)


def _per_tile(fn, x, mod_refs):
    return jnp.concatenate([fn(x[i * TM:(i + 1) * TM], m) for i, m in enumerate(mod_refs)],
                           axis=0)


def _swiglu(h, wgu_ref, wd_ref):
    gu = jnp.dot(h, wgu_ref[...], preferred_element_type=jnp.float32)
    gt, up = gu[:, :D_FF], gu[:, D_FF:]
    a = (gt * jax.nn.sigmoid(gt) * up).astype(jnp.bfloat16)
    return jnp.dot(a, wd_ref[...], preferred_element_type=jnp.float32)


def _ffn_half_step(x, mod_refs, row0, g, wgu_ref, wd_ref, h=None):
    if h is None:
        h = _per_tile(lambda t, m: _ffn_norm(t, m, row0, g), x, mod_refs)
    y = _swiglu(h, wgu_ref, wd_ref)
    return x + _per_tile(lambda t, m: 0.5 * m[row0 + 2:row0 + 3, :] * t, y, mod_refs)


def _mod_kernel(cc_ref, w_ref, b_ref, o_ref):
    s = cc_ref[...]
    s = s * jax.nn.sigmoid(s)
    s_hi = s.astype(jnp.bfloat16)
    s_lo = (s - s_hi.astype(jnp.float32)).astype(jnp.bfloat16)
    w = w_ref[...]
    w_hi = w.astype(jnp.bfloat16)
    w_lo = (w - w_hi.astype(jnp.float32)).astype(jnp.bfloat16)
    rows = s.shape[0]
    both = jnp.dot(jnp.concatenate([s_hi, s_lo], axis=0), w_hi,
                   preferred_element_type=jnp.float32)
    cross = jnp.dot(s_hi, w_lo, preferred_element_type=jnp.float32)
    o_ref[...] = both[:rows] + both[rows:] + cross + b_ref[...]


def _modulation(cc, w_mod, b_mod):
    depth = w_mod.shape[0]
    rows = cc.shape[0]
    tn = 1024
    return pl.pallas_call(
        _mod_kernel,
        grid=(depth, N_MOD * D // tn),
        in_specs=[
            pl.BlockSpec((rows, D), lambda l, j: (0, 0)),
            pl.BlockSpec((None, D, tn), lambda l, j: (l, 0, j)),
            pl.BlockSpec((None, 1, tn), lambda l, j: (l, 0, j)),
        ],
        out_specs=pl.BlockSpec((None, rows, tn), lambda l, j: (l, 0, j)),
        out_shape=jax.ShapeDtypeStruct((depth, rows, N_MOD * D), jnp.float32),
        compiler_params=_cparams(2),
        name="modulation",
    )(cc, w_mod, b_mod.reshape(depth, 1, N_MOD * D))


def _even_in_stage(p, ccs_ref, gv_ref, ws_ref, bs_ref, pq_ref, sgu_ref):
    a = p[:, :D_FOURIER].astype(jnp.bfloat16)
    uv = jax.nn.gelu(p[:, D_FOURIER:], approximate=True)
    u = uv[:, :D_SGU]
    v = uv[:, D_SGU:]
    ccs = ccs_ref[...]
    ps, qs, gated = [], [], []
    n_chunks = p.shape[0] // CHUNK
    for grp in range(N_GROUPS):
        lo, hi = grp * GROUP_CH, (grp + 1) * GROUP_CH
        t = jnp.dot(a[:, lo:hi], ccs, preferred_element_type=jnp.float32)
        ps.append(t[:, :GROUP_CH])
        qs.append(t[:, GROUP_CH:])
        vg = v[:, lo:hi]
        ms = jnp.mean(vg * vg, axis=-1, keepdims=True)
        vh = (vg * lax.rsqrt(ms + EPS) * gv_ref[:, lo:hi]).astype(jnp.bfloat16)
        rhs = jnp.concatenate([vh[c * CHUNK:(c + 1) * CHUNK, :] for c in range(n_chunks)], axis=1)
        mixed = jnp.dot(ws_ref[grp], rhs, preferred_element_type=jnp.float32)
        mixed = jnp.concatenate(
            [mixed[:, c * GROUP_CH:(c + 1) * GROUP_CH] for c in range(n_chunks)], axis=0)
        bias = jnp.concatenate([bs_ref[:, lo:hi]] * n_chunks, axis=0)
        gated.append(u[:, lo:hi] * (mixed + bias))
    pq_ref[...] = jnp.concatenate(ps + qs, axis=1).astype(jnp.bfloat16)
    sgu_ref[...] = jnp.concatenate(gated, axis=1).astype(jnp.bfloat16)


def _rope(t, cos, sin_signed, first_half):
    partner = jnp.where(first_half, pltpu.roll(t, 3 * HEAD_DIM // 4, 1),
                        pltpu.roll(t, HEAD_DIM // 4, 1))
    return t * cos + partner * sin_signed


def _qkv_stage(qkv, gq_ref, gk_ref, cos, sin, q_ref, kt_ref, v_ref):
    lane = lax.broadcasted_iota(jnp.int32, (qkv.shape[0], HEAD_DIM), 1)
    first_half = (lane % (HEAD_DIM // 2)) < (HEAD_DIM // 4)
    q_scale = (HEAD_DIM ** -0.5) * math.log2(math.e)

    def head(col, gain):
        t = qkv[:, col:col + HEAD_DIM]
        ms = jnp.mean(t * t, axis=-1, keepdims=True)
        return _rope(t * lax.rsqrt(ms + EPS) * gain, cos, sin, first_half)

    qs = [head(hd * HEAD_DIM, gq_ref[...]) * q_scale for hd in range(N_HEADS)]
    q_ref[...] = jnp.concatenate(qs, axis=1).astype(jnp.bfloat16)
    for hd in range(N_KV):
        kt_ref[hd] = head((N_HEADS + hd) * HEAD_DIM, gk_ref[...]).T.astype(jnp.bfloat16)
    v_ref[...] = qkv[:, (N_HEADS + N_KV) * HEAD_DIM:].astype(jnp.bfloat16)


PAIR = 2 * TM


def _layer_in_kernel(*refs, split_in, even, steps):
    refs = list(refs)
    h_ref = refs.pop()
    t = pl.program_id(0)

    @pl.when(t == 0)
    def _():
        h_ref[...] = jnp.zeros(h_ref.shape, h_ref.dtype)

    if split_in:
        tile0 = 2 * jnp.minimum(t, steps - 2)
        x = jnp.concatenate(
            [jnp.where((tile0 + i) % N_TILES == N_LAT_TILES, refs[2 * i + 1][...], refs[2 * i][...])
             for i in range(2)], axis=0)
        del refs[:4]
    else:
        x = refs.pop(0)[...]
    mod_refs = refs[:2]
    g1_ref, wgu_ref, wd_ref, gm_ref, w_stage_ref = refs[2:7]
    n_out = 3 if even else 4
    stage_in = refs[7:-n_out]
    xo_ref, *stage_out = refs[-n_out:]
    proj = jnp.dot(h_ref[...], w_stage_ref[...], preferred_element_type=jnp.float32)
    if even:
        _even_in_stage(proj, *stage_in, *stage_out)
    else:
        gq_ref, gk_ref, cos0, sin0, cos1, sin1 = stage_in
        _qkv_stage(proj, gq_ref, gk_ref, jnp.concatenate([cos0[...], cos1[...]], axis=0),
                   jnp.concatenate([sin0[...], sin1[...]], axis=0), *stage_out)
    x = _ffn_half_step(x, mod_refs, 0, g1_ref[...], wgu_ref, wd_ref)
    xo_ref[...] = x
    gm = gm_ref[...]
    h_ref[...] = _per_tile(lambda r, m: _ffn_norm(r, m, 3, gm), x, mod_refs)


def _layer_in(xs, mods, g1, wgu, wd, gm, layer, even, stage_args):
    batch = xs[0].shape[0]
    j = layer // 2
    split_in = len(xs) == 2
    n_rows = batch * ROWS
    assert (batch * N_TILES) % 2 == 0
    steps = batch * N_TILES // 2 + 1

    def cur(t):
        return jnp.minimum(t, steps - 2)

    def prev(t):
        return jnp.maximum(t - 1, 0)

    def tile(pair, i):
        n = 2 * pair + i
        return n // N_TILES, n % N_TILES

    def rows_spec(width, which):
        return pl.BlockSpec((PAIR, width), lambda t: (which(t), 0))

    def const(block_shape, *index):
        return _resident(block_shape, lambda t: index)

    def mod_spec(i):
        def index(t):
            b, r = tile(cur(t), i)
            return jnp.where(r == N_LAT_TILES, batch, b), 0, 0
        return pl.BlockSpec((None, N_MOD, D), index)

    if split_in:
        x_arrs, x_specs = [], []
        for i in range(2):
            def lat(t, i=i):
                b, r = tile(cur(t), i)
                return b, jnp.minimum(r, N_LAT_TILES - 1), 0
            x_arrs += list(xs)
            x_specs += [pl.BlockSpec((None, TM, D), lat),
                        pl.BlockSpec((None, CTX, D), lambda t, i=i: (tile(cur(t), i)[0], 0, 0))]
    else:
        x_arrs, x_specs = [xs[0].reshape(n_rows, D)], [rows_spec(D, cur)]
    in_specs = x_specs + [
        mod_spec(0), mod_spec(1),
        const((None, 1, D), layer, 0, 0),
        const((None, D, 2 * D_FF), layer, 0, 0),
        const((None, D_FF, D), layer, 0, 0),
        const((None, 1, D), layer, 0, 0),
    ]
    out_specs = [rows_spec(D, cur)]
    out_shape = [jax.ShapeDtypeStruct((n_rows, D), jnp.float32)]
    if even:
        in_specs += [
            const((None, D, D_FOURIER + 2 * D_SGU), j, 0, 0),
            const((GROUP_CH, 2 * GROUP_CH), 0, 0),
            const((None, 1, D_SGU), j, 0, 0),
            const((None, N_GROUPS, CHUNK, CHUNK), j, 0, 0, 0),
            const((None, CHUNK, D_SGU), j, 0, 0),
        ]
        out_specs += [rows_spec(2 * D_FOURIER, prev), rows_spec(D_SGU, prev)]
        out_shape += [jax.ShapeDtypeStruct((n_rows, 2 * D_FOURIER), jnp.bfloat16),
                      jax.ShapeDtypeStruct((n_rows, D_SGU), jnp.bfloat16)]
        stage_arrs = list(stage_args)
    else:
        wqkv, gq, gk, cos_t, sin_t = stage_args

        def rope_spec(i):
            return pl.BlockSpec((TM, HEAD_DIM), lambda t: (tile(prev(t), i)[1], 0))

        in_specs += [
            const((None, D, D + 2 * KV_W), j, 0, 0),
            const((None, 1, HEAD_DIM), j, 0, 0),
            const((None, 1, HEAD_DIM), j, 0, 0),
            rope_spec(0), rope_spec(0), rope_spec(1), rope_spec(1),
        ]
        stage_arrs = [wqkv, gq, gk, cos_t, sin_t, cos_t, sin_t]
        out_specs += [
            rows_spec(D, prev),
            pl.BlockSpec((N_KV, HEAD_DIM, PAIR), lambda t: (0, 0, prev(t))),
            rows_spec(KV_W, prev),
        ]
        out_shape += [jax.ShapeDtypeStruct((n_rows, D), jnp.bfloat16),
                      jax.ShapeDtypeStruct((N_KV, HEAD_DIM, n_rows), jnp.bfloat16),
                      jax.ShapeDtypeStruct((n_rows, KV_W), jnp.bfloat16)]
    outs = pl.pallas_call(
        functools.partial(_layer_in_kernel, split_in=split_in, even=even, steps=steps),
        grid=(steps,),
        in_specs=in_specs,
        out_specs=out_specs,
        out_shape=out_shape,
        scratch_shapes=[pltpu.VMEM((PAIR, D), jnp.bfloat16)],
        compiler_params=_cparams(1),
        name="layer_in_even" if even else "layer_in_odd",
    )(*x_arrs, mods, mods, g1, wgu, wd, gm, *stage_arrs)
    return [o if o.shape[0] == N_KV and o.ndim == 3 else o.reshape(batch, ROWS, o.shape[-1])
            for o in outs]


def _layer_out_kernel(x_ref, *refs, n_mod, n_y, split_y, final):
    mod_refs = refs[:n_mod] * (2 // n_mod)
    wo_ref = refs[n_mod]
    y_refs = refs[n_mod + 1:n_mod + 1 + n_y]
    g2_ref, wgu_ref, wd_ref = refs[n_mod + 1 + n_y:n_mod + 4 + n_y]
    o_ref = refs[-1]
    if split_y:
        tile0 = 2 * pl.program_id(0)
        y = jnp.concatenate(
            [jnp.where((tile0 + i) % N_TILES == N_LAT_TILES, y_refs[2 * i + 1][...],
                       y_refs[2 * i][...]) for i in range(2)], axis=0)
    else:
        y = y_refs[0][...] if n_y == 1 else jnp.concatenate([r[...] for r in y_refs], axis=1)
    proj = jnp.dot(y, wo_ref[...], preferred_element_type=jnp.float32)
    x = x_ref[...] + _per_tile(lambda p, m: m[5:6, :] * p, proj, mod_refs)
    x = _ffn_half_step(x, mod_refs, 6, g2_ref[...], wgu_ref, wd_ref)
    if final:
        gf_ref = refs[-2]
        ms = jnp.mean(x * x, axis=-1, keepdims=True)
        x = x * lax.rsqrt(ms + EPS) * gf_ref[...]
    o_ref[...] = x


def _layer_out(xc, mods, wo, ys, g2, wgu, wd, layer, n_tiles, g_final=None, split_y=False):
    batch = xc.shape[0]
    j = layer // 2
    final = g_final is not None

    def const(block_shape, *index):
        return _resident(block_shape, lambda *_: index)

    if n_tiles == N_LAT_TILES:
        grid = (batch, N_LAT_TILES // 2)
        x_arr, x_spec = xc, pl.BlockSpec((None, PAIR, D), lambda b, k: (b, k, 0))
        mod_specs = [pl.BlockSpec((None, N_MOD, D), lambda b, k: (b, 0, 0))]
        y_arrs = list(ys)
        y_specs = [pl.BlockSpec((None, PAIR, y.shape[-1]), lambda b, k: (b, k, 0)) for y in ys]
        out_spec = pl.BlockSpec((None, PAIR, D), lambda b, k: (b, k, 0))
        out_shape = jax.ShapeDtypeStruct((batch, SEQ, D), jnp.float32)
    else:
        assert (batch * N_TILES) % 2 == 0
        grid = (batch * N_TILES // 2,)
        x_arr, x_spec = xc.reshape(batch * ROWS, D), pl.BlockSpec((PAIR, D), lambda k: (k, 0))

        def tile(k, i):
            t = 2 * k + i
            return t // N_TILES, t % N_TILES

        def mod_spec(i):
            def index(k):
                b, r = tile(k, i)
                return jnp.where(r == N_LAT_TILES, batch, b), 0, 0
            return pl.BlockSpec((None, N_MOD, D), index)

        mod_specs = [mod_spec(0), mod_spec(1)]
        if split_y:
            y_arrs, y_specs = [], []
            for i in range(2):
                def lat(k, i=i):
                    b, r = tile(k, i)
                    return b, jnp.minimum(r, N_LAT_TILES - 1), 0
                y_arrs += list(ys)
                y_specs += [pl.BlockSpec((None, TM, D), lat),
                            pl.BlockSpec((None, CTX, D), lambda k, i=i: (tile(k, i)[0], 0, 0))]
        else:
            y_arrs = [y.reshape(batch * ROWS, y.shape[-1]) for y in ys]
            y_specs = [pl.BlockSpec((PAIR, y.shape[-1]), lambda k: (k, 0)) for y in y_arrs]
        out_spec = pl.BlockSpec((PAIR, D), lambda k: (k, 0))
        out_shape = jax.ShapeDtypeStruct((batch * ROWS, D), jnp.float32)
    in_specs = [x_spec] + mod_specs + [const((None, D, D), j, 0, 0)] + y_specs + [
        const((None, 1, D), layer, 0, 0),
        const((None, D, 2 * D_FF), layer, 0, 0),
        const((None, D_FF, D), layer, 0, 0),
    ]
    args = [x_arr] + [mods] * len(mod_specs) + [wo] + y_arrs + [g2, wgu, wd]
    if final:
        in_specs.append(const((1, D), 0, 0))
        args.append(g_final)
    out = pl.pallas_call(
        functools.partial(_layer_out_kernel, n_mod=len(mod_specs), n_y=len(y_arrs),
                          split_y=split_y, final=final),
        grid=grid,
        in_specs=in_specs,
        out_specs=out_spec,
        out_shape=out_shape,
        compiler_params=_cparams(len(grid)),
        name="layer_out_final" if final else "layer_out",
    )(*args)
    return out.reshape(batch, -1, D)


HALF = SEQ // 2
FT = MXU_DIM
FW = MXU_DIM
N_FOLD_TILES = HALF // FT
DFT_SCALE = SEQ ** -0.5


def _dft_kernel(ch_ref, sh_ref, rev_ref, cc_ref, sc_ref, p_ref, q_ref, o_ref):
    rev = rev_ref[...]
    row0 = lax.broadcasted_iota(jnp.int32, (FT, FW), 0) == 0
    alt = (1 - 2 * (lax.broadcasted_iota(jnp.int32, (FT, FW), 0) & 1)).astype(jnp.float32)

    def reversed_upper(x_ref, t):
        src = HALF + FT * (N_FOLD_TILES - 1 - t)
        r = jnp.dot(rev, x_ref[src:src + FT, :], preferred_element_type=jnp.float32)
        if t == 0:
            return r
        first = HALF + FT * (N_FOLD_TILES - t)
        return jnp.where(row0, x_ref[first:first + 1, :].astype(jnp.float32), r)

    pfs, qfs, alt_sum = [], [], None
    for t in range(N_FOLD_TILES):
        rows = slice(t * FT, (t + 1) * FT)
        pf = p_ref[rows, :].astype(jnp.float32) + reversed_upper(p_ref, t)
        qf = q_ref[rows, :].astype(jnp.float32) - reversed_upper(q_ref, t)
        alt_sum = pf if alt_sum is None else alt_sum + pf
        pfs.append(pf.astype(jnp.bfloat16))
        qfs.append(qf.astype(jnp.bfloat16))
    p_mid = p_ref[HALF:HALF + 1, :].astype(jnp.float32)
    sign = jnp.concatenate([alt] * N_FOLD_TILES, axis=0)
    e = jnp.dot(ch_ref[...], jnp.concatenate(pfs, axis=0), preferred_element_type=jnp.float32)
    e = e + sign * (DFT_SCALE * p_mid)
    o = jnp.dot(sh_ref[...], jnp.concatenate(qfs, axis=0), preferred_element_type=jnp.float32)
    o_ref[0:HALF, :] = (e + o).astype(jnp.bfloat16)
    g = (e - o).astype(jnp.bfloat16)
    mid = DFT_SCALE * (jnp.sum(alt_sum * alt, axis=0, keepdims=True) + p_mid)
    for t in range(N_FOLD_TILES):
        src = FT * (N_FOLD_TILES - 1 - t)
        up = jnp.dot(rev, g[src:src + FT, :], preferred_element_type=jnp.float32)
        first = mid if t == 0 else g[src + FT:src + FT + 1, :].astype(jnp.float32)
        o_ref[HALF + t * FT:HALF + (t + 1) * FT, :] = jnp.where(row0, first, up).astype(jnp.bfloat16)
    ctx = jnp.dot(cc_ref[...], p_ref[SEQ:, :], preferred_element_type=jnp.float32)
    ctx = ctx + jnp.dot(sc_ref[...], q_ref[SEQ:, :], preferred_element_type=jnp.float32)
    o_ref[SEQ:, :] = ctx.astype(jnp.bfloat16)


def _dft(tables, pq):
    ch, sh, rev, cc, sc = tables
    batch = pq.shape[0]
    n_col = D_FOURIER // FW
    return pl.pallas_call(
        _dft_kernel,
        grid=(batch, n_col),
        in_specs=[
            _resident((HALF, HALF), lambda b, c: (0, 0)),
            _resident((HALF, HALF), lambda b, c: (0, 0)),
            _resident((FT, FT), lambda b, c: (0, 0)),
            _resident((CTX, CTX), lambda b, c: (0, 0)),
            _resident((CTX, CTX), lambda b, c: (0, 0)),
            pl.BlockSpec((None, ROWS, FW), lambda b, c: (b, 0, c)),
            pl.BlockSpec((None, ROWS, FW), lambda b, c: (b, 0, n_col + c)),
        ],
        out_specs=pl.BlockSpec((None, ROWS, FW), lambda b, c: (b, 0, c)),
        out_shape=jax.ShapeDtypeStruct((batch, ROWS, D_FOURIER), jnp.bfloat16),
        compiler_params=_cparams(2),
        name="dft",
    )(ch, sh, rev, cc, sc, pq, pq)


KEY_TILE = MXU_DIM
N_KEY_TILES = ROWS // KEY_TILE


def _with_row_sums(v):
    return jnp.concatenate([v, jnp.ones_like(v)], axis=1)


def _normalised(pv):
    return (pv[:, :HEAD_DIM] / pv[:, HEAD_DIM:]).astype(jnp.bfloat16)


def _attn_kernel(q_ref, kt_ref, v_ref, o_ref, s_ref, m_ref):
    @pl.when(pl.program_id(0) == 0)
    def _():
        s_ref[...] = jnp.zeros(s_ref.shape, s_ref.dtype)
        m_ref[...] = jnp.zeros(m_ref.shape, m_ref.dtype)

    outs = []
    for hd in range(GROUP):
        q = q_ref[:, hd * HEAD_DIM:(hd + 1) * HEAD_DIM]
        m_prev = m_ref[hd]
        m_lane = None
        acc = None
        for j in range(N_KEY_TILES):
            keys = slice(j * KEY_TILE, (j + 1) * KEY_TILE)
            s_old = s_ref[hd, :, keys]
            p = jnp.exp2(s_old - jnp.concatenate([m_prev] * (KEY_TILE // HEAD_DIM), axis=1))
            pv = jnp.dot(p.astype(jnp.bfloat16), _with_row_sums(v_ref[keys, :]),
                         preferred_element_type=jnp.float32)
            acc = pv if acc is None else acc + pv
            s_new = jnp.dot(q, kt_ref[:, keys], preferred_element_type=jnp.float32)
            s_ref[hd, :, keys] = s_new
            for c in range(KEY_TILE // HEAD_DIM):
                part = s_new[:, c * HEAD_DIM:(c + 1) * HEAD_DIM]
                m_lane = part if m_lane is None else jnp.maximum(m_lane, part)
        outs.append(_normalised(acc))
        m_ref[hd] = jnp.broadcast_to(jnp.max(m_lane, axis=-1, keepdims=True), m_lane.shape)
    o_ref[...] = jnp.concatenate(outs, axis=1)


def _attention_latent(q, kt, v):
    batch = q.shape[0]
    gw = GROUP * HEAD_DIM
    per_sample = N_KV * N_LAT_TILES
    steps = batch * per_sample + 1

    def unflatten(c):
        return c // per_sample, (c % per_sample) // N_LAT_TILES, c % N_LAT_TILES

    def cur(t):
        return unflatten(jnp.minimum(t, steps - 2))

    def prev(t):
        return unflatten(jnp.maximum(t - 1, 0))

    def q_map(t):
        b, g, i = cur(t)
        return b, i, g

    def kt_map(t):
        b, g, _ = cur(t)
        return g, 0, b

    def v_map(t):
        b, g, _ = prev(t)
        return b, 0, g

    def o_map(t):
        b, g, i = prev(t)
        return b, i, g

    return pl.pallas_call(
        _attn_kernel,
        grid=(steps,),
        in_specs=[
            pl.BlockSpec((None, TM, gw), q_map),
            pl.BlockSpec((None, HEAD_DIM, ROWS), kt_map),
            pl.BlockSpec((None, ROWS, HEAD_DIM), v_map),
        ],
        out_specs=pl.BlockSpec((None, TM, gw), o_map),
        out_shape=jax.ShapeDtypeStruct((batch, SEQ, D), jnp.bfloat16),
        scratch_shapes=[pltpu.VMEM((GROUP, TM, ROWS), jnp.float32),
                        pltpu.VMEM((GROUP, TM, HEAD_DIM), jnp.float32)],
        compiler_params=_cparams(1),
        name="attention",
    )(q, kt, v)


def _attn_ctx_kernel(q_ref, kt_ref, v_ref, o_ref):
    v1 = _with_row_sums(v_ref[...])
    kt = kt_ref[...]
    outs = []
    for hd in range(GROUP):
        s = jnp.dot(q_ref[:, hd * HEAD_DIM:(hd + 1) * HEAD_DIM], kt,
                    preferred_element_type=jnp.float32)
        p = jnp.exp2(s - jnp.max(s, axis=-1, keepdims=True)).astype(jnp.bfloat16)
        outs.append(_normalised(jnp.dot(p, v1, preferred_element_type=jnp.float32)))
    o_ref[...] = jnp.concatenate(outs, axis=1)


def _attention_context(q, kt, v):
    batch = q.shape[0]
    gw = GROUP * HEAD_DIM
    return pl.pallas_call(
        _attn_ctx_kernel,
        grid=(batch, N_KV),
        in_specs=[
            pl.BlockSpec((None, CTX, gw), lambda b, g: (b, SEQ // CTX, g)),
            pl.BlockSpec((None, HEAD_DIM, CTX), lambda b, g: (g, 0, b * N_TILES + N_LAT_TILES)),
            pl.BlockSpec((None, CTX, HEAD_DIM), lambda b, g: (b, SEQ // CTX, g)),
        ],
        out_specs=pl.BlockSpec((None, CTX, gw), lambda b, g: (b, 0, g)),
        out_shape=jax.ShapeDtypeStruct((batch, CTX, D), jnp.bfloat16),
        compiler_params=_cparams(2),
        name="attention_ctx",
    )(q, kt, v)


def _cos_sin_products(n, size):
    r = 1 << (int(math.log2(size)) // 2)
    hi = size // r
    j = jnp.arange(size, dtype=jnp.int32)[:, None]
    ang_hi = ((j * jnp.arange(hi, dtype=jnp.int32)[None, :] * r) % n).astype(jnp.float32)
    ang_lo = ((j * jnp.arange(r, dtype=jnp.int32)[None, :]) % n).astype(jnp.float32)
    w = 2.0 * math.pi / n
    ca, sa = jnp.cos(ang_hi * w)[:, :, None], jnp.sin(ang_hi * w)[:, :, None]
    cb, sb = jnp.cos(ang_lo * w)[:, None, :], jnp.sin(ang_lo * w)[:, None, :]
    scale = n ** -0.5
    cos = ((ca * cb - sa * sb) * scale).reshape(size, size)
    sin = ((sa * cb + ca * sb) * scale).reshape(size, size)
    return cos, sin


def _dft_tables():
    ch, sh = _cos_sin_products(SEQ, HALF)
    cc, sc = _cos_sin_products(CTX, CTX)
    r = jnp.arange(FT, dtype=jnp.int32)
    rev = ((r[:, None] + r[None, :]) == FT).astype(jnp.bfloat16)
    return tuple(t.astype(jnp.bfloat16) for t in (ch, sh, rev, cc, sc))


def _channel_dft():
    c, s = _cos_sin_products(GROUP_CH, GROUP_CH)
    return jnp.concatenate([c, -s], axis=1).astype(jnp.bfloat16)


def _rope_tables():
    half = HEAD_DIM // 2
    rows = SEQ // GRID_W
    row = jnp.repeat(jnp.arange(rows, dtype=jnp.float32), GRID_W)
    col = jnp.tile(jnp.arange(GRID_W, dtype=jnp.float32), rows)
    inv_freq = ROPE_THETA ** (-jnp.arange(0, half, 2, dtype=jnp.float32) / half)
    ang_row = row[:, None] * inv_freq
    ang_col = col[:, None] * inv_freq
    cos = jnp.concatenate([jnp.cos(ang_row)] * 2 + [jnp.cos(ang_col)] * 2, axis=1)
    sin = jnp.concatenate([-jnp.sin(ang_row), jnp.sin(ang_row),
                           -jnp.sin(ang_col), jnp.sin(ang_col)], axis=1)
    cos = jnp.concatenate([cos, jnp.ones((CTX, HEAD_DIM), jnp.float32)], axis=0)
    sin = jnp.concatenate([sin, jnp.zeros((CTX, HEAD_DIM), jnp.float32)], axis=0)
    return cos, sin


def kernel(x, c, ctx, c_ctx, w_mod, b_mod, g_ffn1, w_ffn1_gu, w_ffn1_down, g_mix, g_ffn2, w_ffn2_gu, w_ffn2_down, g_final, w_in_ab, g_v, w_s, b_s, w_out_ab, w_qkv, g_q, g_k, w_o):
    batch = x.shape[0]
    depth = w_mod.shape[0]
    assert x.shape == (batch, SEQ, D) and ctx.shape == (batch, CTX, D)
    assert depth % 2 == 0

    mod_rows = -(-(batch + 1) // 8) * 8
    cc = jnp.concatenate([c, c_ctx[None], jnp.zeros((mod_rows - batch - 1, D), jnp.float32)], axis=0)
    m = _modulation(cc, w_mod, b_mod)
    m = m[:, :batch + 1].reshape(depth, batch + 1, N_MOD, D)

    wgu1, wd1 = w_ffn1_gu.astype(jnp.bfloat16), w_ffn1_down.astype(jnp.bfloat16)
    wgu2, wd2 = w_ffn2_gu.astype(jnp.bfloat16), w_ffn2_down.astype(jnp.bfloat16)
    g1 = g_ffn1.reshape(depth, 1, D)
    g2 = g_ffn2.reshape(depth, 1, D)
    gm = g_mix.reshape(depth, 1, D)
    w_in = w_in_ab.astype(jnp.bfloat16)
    w_out = w_out_ab.astype(jnp.bfloat16)
    ws = w_s.astype(jnp.bfloat16)
    gv = g_v.reshape(-1, 1, D_SGU)
    bs_full = jnp.repeat(jnp.swapaxes(b_s, 1, 2), GROUP_CH, axis=2)
    wqkv = w_qkv.astype(jnp.bfloat16)
    wo = w_o.astype(jnp.bfloat16)
    gq = g_q.reshape(-1, 1, HEAD_DIM)
    gk = g_k.reshape(-1, 1, HEAD_DIM)
    dft_tables = _dft_tables()
    ccs = _channel_dft()
    cos_t, sin_t = _rope_tables()

    xs = (x, ctx)
    for l in range(depth):
        last = l == depth - 1
        even = l % 2 == 0
        mods = m[l]
        n_out = N_LAT_TILES if last else N_TILES
        if even:
            xc, pq, sgu = _layer_in(xs, mods, g1, wgu1, wd1, gm, l, True,
                                    (w_in, ccs, gv, ws, bs_full))
            ys, w_proj = [_dft(dft_tables, pq), sgu], w_out
        else:
            xc, q, kt, v = _layer_in(xs, mods, g1, wgu1, wd1, gm, l, False,
                                     (wqkv, gq, gk, cos_t, sin_t))
            ys, w_proj = [_attention_latent(q, kt, v)], wo
            if not last:
                ys.append(_attention_context(q, kt, v))
        xc = _layer_out(xc, mods, w_proj, ys, g2, wgu2, wd2, l, n_out,
                        g_final.reshape(1, D) if last else None,
                        split_y=not even and not last)
        xs = (xc,)
    return xc
```

```python
import functools
import math

import jax
import jax.numpy as jnp
from jax import lax
from jax.experimental import pallas as pl
from jax.experimental.pallas import tpu as pltpu

D = 1024
SEQ = 4096
CTX = 256
ROWS = SEQ + CTX
GRID_W = 64
D_FF = 2752
N_MOD = 9
EPS = 1e-6
HEAD_DIM = 128
N_HEADS = 8
N_KV = 2
GROUP = N_HEADS // N_KV
KV_W = N_KV * HEAD_DIM
ROPE_THETA = 10000.0
D_FOURIER = 512
D_SGU = 512
GROUP_CH = 128
N_GROUPS = 4
CHUNK = 128

MXU_DIM = 256
TM = 256
N_LAT_TILES = SEQ // TM
N_TILES = ROWS // TM
VMEM_LIMIT = 56 * 1024 * 1024

assert TM == CTX and SEQ % TM == 0


def _cparams(n_axes):
    return pltpu.CompilerParams(dimension_semantics=("arbitrary",) * n_axes,
                                vmem_limit_bytes=VMEM_LIMIT)


def _resident(block_shape, index_map):
    return pl.BlockSpec(block_shape, index_map, pipeline_mode=pl.Buffered(1))


def _modnorm(x, g, shift, scale):
    ms = jnp.mean(x * x, axis=-1, keepdims=True)
    return x * lax.rsqrt(ms + EPS) * (g * (1.0 + scale)) + shift


def _ffn_norm(x, mod_ref, row0, g):
    h = _modnorm(x, g, mod_ref[row0:row0 + 1, :], mod_ref[row0 + 1:row0 + 2, :])
    return h.astype(jnp.bfloat16)


def _per_tile(fn, x, mod_refs):
    return jnp.concatenate([fn(x[i * TM:(i + 1) * TM], m) for i, m in enumerate(mod_refs)],
                           axis=0)


def _swiglu(h, wgu_ref, wd_ref):
    gu = jnp.dot(h, wgu_ref[...], preferred_element_type=jnp.float32)
    gt, up = gu[:, :D_FF], gu[:, D_FF:]
    a = (gt * jax.nn.sigmoid(gt) * up).astype(jnp.bfloat16)
    return jnp.dot(a, wd_ref[...], preferred_element_type=jnp.float32)


def _ffn_half_step(x, mod_refs, row0, g, wgu_ref, wd_ref, h=None):
    if h is None:
        h = _per_tile(lambda t, m: _ffn_norm(t, m, row0, g), x, mod_refs)
    y = _swiglu(h, wgu_ref, wd_ref)
    return x + _per_tile(lambda t, m: 0.5 * m[row0 + 2:row0 + 3, :] * t, y, mod_refs)


def _mod_kernel(cc_ref, w_ref, b_ref, o_ref):
    s = cc_ref[...]
    s = s * jax.nn.sigmoid(s)
    s_hi = s.astype(jnp.bfloat16)
    s_lo = (s - s_hi.astype(jnp.float32)).astype(jnp.bfloat16)
    w = w_ref[...]
    w_hi = w.astype(jnp.bfloat16)
    w_lo = (w - w_hi.astype(jnp.float32)).astype(jnp.bfloat16)
    rows = s.shape[0]
    both = jnp.dot(jnp.concatenate([s_hi, s_lo], axis=0), w_hi,
                   preferred_element_type=jnp.float32)
    cross = jnp.dot(s_hi, w_lo, preferred_element_type=jnp.float32)
    o_ref[...] = both[:rows] + both[rows:] + cross + b_ref[...]


def _modulation(cc, w_mod, b_mod):
    depth = w_mod.shape[0]
    rows = cc.shape[0]
    tn = 1024
    return pl.pallas_call(
        _mod_kernel,
        grid=(depth, N_MOD * D // tn),
        in_specs=[
            pl.BlockSpec((rows, D), lambda l, j: (0, 0)),
            pl.BlockSpec((None, D, tn), lambda l, j: (l, 0, j)),
            pl.BlockSpec((None, 1, tn), lambda l, j: (l, 0, j)),
        ],
        out_specs=pl.BlockSpec((None, rows, tn), lambda l, j: (l, 0, j)),
        out_shape=jax.ShapeDtypeStruct((depth, rows, N_MOD * D), jnp.float32),
        compiler_params=_cparams(2),
        name="modulation",
    )(cc, w_mod, b_mod.reshape(depth, 1, N_MOD * D))


def _even_in_stage(p, ccs_ref, gv_ref, ws_ref, bs_ref, pq_ref, sgu_ref):
    a = p[:, :D_FOURIER].astype(jnp.bfloat16)
    uv = jax.nn.gelu(p[:, D_FOURIER:], approximate=True)
    u = uv[:, :D_SGU]
    v = uv[:, D_SGU:]
    ccs = ccs_ref[...]
    ps, qs, gated = [], [], []
    n_chunks = p.shape[0] // CHUNK
    for grp in range(N_GROUPS):
        lo, hi = grp * GROUP_CH, (grp + 1) * GROUP_CH
        t = jnp.dot(a[:, lo:hi], ccs, preferred_element_type=jnp.float32)
        ps.append(t[:, :GROUP_CH])
        qs.append(t[:, GROUP_CH:])
        vg = v[:, lo:hi]
        ms = jnp.mean(vg * vg, axis=-1, keepdims=True)
        vh = (vg * lax.rsqrt(ms + EPS) * gv_ref[:, lo:hi]).astype(jnp.bfloat16)
        rhs = jnp.concatenate([vh[c * CHUNK:(c + 1) * CHUNK, :] for c in range(n_chunks)], axis=1)
        mixed = jnp.dot(ws_ref[grp], rhs, preferred_element_type=jnp.float32)
        mixed = jnp.concatenate(
            [mixed[:, c * GROUP_CH:(c + 1) * GROUP_CH] for c in range(n_chunks)], axis=0)
        bias = jnp.concatenate([bs_ref[:, lo:hi]] * n_chunks, axis=0)
        gated.append(u[:, lo:hi] * (mixed + bias))
    pq_ref[...] = jnp.concatenate(ps + qs, axis=1).astype(jnp.bfloat16)
    sgu_ref[...] = jnp.concatenate(gated, axis=1).astype(jnp.bfloat16)


def _rope(t, cos, sin_signed):
    return t * cos + pltpu.roll(t, HEAD_DIM // 2, 1) * sin_signed


def _qkv_stage(qkv, gq_ref, gk_ref, cos, sin, q_ref, kt_ref, v_ref):
    q_scale = (HEAD_DIM ** -0.5) * math.log2(math.e)

    def head(col, gain):
        t = qkv[:, col:col + HEAD_DIM]
        ms = jnp.mean(t * t, axis=-1, keepdims=True)
        return _rope(t * lax.rsqrt(ms + EPS) * gain, cos, sin)

    qs = [head(hd * HEAD_DIM, gq_ref[...]) * q_scale for hd in range(N_HEADS)]
    q_ref[...] = jnp.concatenate(qs, axis=1).astype(jnp.bfloat16)
    for hd in range(N_KV):
        kt_ref[hd] = head((N_HEADS + hd) * HEAD_DIM, gk_ref[...]).T.astype(jnp.bfloat16)
    v_ref[...] = qkv[:, (N_HEADS + N_KV) * HEAD_DIM:].astype(jnp.bfloat16)


PAIR = 2 * TM


def _layer_in_kernel(*refs, split_in, even, steps):
    refs = list(refs)
    h_ref = refs.pop()
    t = pl.program_id(0)

    @pl.when(t == 0)
    def _():
        h_ref[...] = jnp.zeros(h_ref.shape, h_ref.dtype)

    if split_in:
        tile0 = 2 * jnp.minimum(t, steps - 2)
        x = jnp.concatenate(
            [jnp.where((tile0 + i) % N_TILES == N_LAT_TILES, refs[2 * i + 1][...], refs[2 * i][...])
             for i in range(2)], axis=0)
        del refs[:4]
    else:
        x = refs.pop(0)[...]
    mod_refs = refs[:2]
    g1_ref, wgu_ref, wd_ref, gm_ref, w_stage_ref = refs[2:7]
    n_out = 3 if even else 4
    stage_in = refs[7:-n_out]
    xo_ref, *stage_out = refs[-n_out:]
    proj = jnp.dot(h_ref[...], w_stage_ref[...], preferred_element_type=jnp.float32)
    if even:
        _even_in_stage(proj, *stage_in, *stage_out)
    else:
        gq_ref, gk_ref, cos0, sin0, cos1, sin1 = stage_in
        _qkv_stage(proj, gq_ref, gk_ref, jnp.concatenate([cos0[...], cos1[...]], axis=0),
                   jnp.concatenate([sin0[...], sin1[...]], axis=0), *stage_out)
    x = _ffn_half_step(x, mod_refs, 0, g1_ref[...], wgu_ref, wd_ref)
    xo_ref[...] = x
    gm = gm_ref[...]
    h_ref[...] = _per_tile(lambda r, m: _ffn_norm(r, m, 3, gm), x, mod_refs)


def _layer_in(xs, mods, g1, wgu, wd, gm, layer, even, stage_args):
    batch = xs[0].shape[0]
    j = layer // 2
    split_in = len(xs) == 2
    n_rows = batch * ROWS
    assert (batch * N_TILES) % 2 == 0
    steps = batch * N_TILES // 2 + 1

    def cur(t):
        return jnp.minimum(t, steps - 2)

    def prev(t):
        return jnp.maximum(t - 1, 0)

    def tile(pair, i):
        n = 2 * pair + i
        return n // N_TILES, n % N_TILES

    def rows_spec(width, which):
        return pl.BlockSpec((PAIR, width), lambda t: (which(t), 0))

    def const(block_shape, *index):
        return _resident(block_shape, lambda t: index)

    def mod_spec(i):
        def index(t):
            b, r = tile(cur(t), i)
            return jnp.where(r == N_LAT_TILES, batch, b), 0, 0
        return pl.BlockSpec((None, N_MOD, D), index)

    if split_in:
        x_arrs, x_specs = [], []
        for i in range(2):
            def lat(t, i=i):
                b, r = tile(cur(t), i)
                return b, jnp.minimum(r, N_LAT_TILES - 1), 0
            x_arrs += list(xs)
            x_specs += [pl.BlockSpec((None, TM, D), lat),
                        pl.BlockSpec((None, CTX, D), lambda t, i=i: (tile(cur(t), i)[0], 0, 0))]
    else:
        x_arrs, x_specs = [xs[0].reshape(n_rows, D)], [rows_spec(D, cur)]
    in_specs = x_specs + [
        mod_spec(0), mod_spec(1),
        const((None, 1, D), layer, 0, 0),
        const((None, D, 2 * D_FF), layer, 0, 0),
        const((None, D_FF, D), layer, 0, 0),
        const((None, 1, D), layer, 0, 0),
    ]
    out_specs = [rows_spec(D, cur)]
    out_shape = [jax.ShapeDtypeStruct((n_rows, D), jnp.float32)]
    if even:
        in_specs += [
            const((None, D, D_FOURIER + 2 * D_SGU), j, 0, 0),
            const((GROUP_CH, 2 * GROUP_CH), 0, 0),
            const((None, 1, D_SGU), j, 0, 0),
            const((None, N_GROUPS, CHUNK, CHUNK), j, 0, 0, 0),
            const((None, CHUNK, D_SGU), j, 0, 0),
        ]
        out_specs += [rows_spec(2 * D_FOURIER, prev), rows_spec(D_SGU, prev)]
        out_shape += [jax.ShapeDtypeStruct((n_rows, 2 * D_FOURIER), jnp.bfloat16),
                      jax.ShapeDtypeStruct((n_rows, D_SGU), jnp.bfloat16)]
        stage_arrs = list(stage_args)
    else:
        wqkv, gq, gk, cos_t, sin_t = stage_args

        def rope_spec(i):
            return pl.BlockSpec((TM, HEAD_DIM), lambda t: (tile(prev(t), i)[1], 0))

        in_specs += [
            const((None, D, D + 2 * KV_W), j, 0, 0),
            const((None, 1, HEAD_DIM), j, 0, 0),
            const((None, 1, HEAD_DIM), j, 0, 0),
            rope_spec(0), rope_spec(0), rope_spec(1), rope_spec(1),
        ]
        stage_arrs = [wqkv, gq, gk, cos_t, sin_t, cos_t, sin_t]
        out_specs += [
            rows_spec(D, prev),
            pl.BlockSpec((N_KV, HEAD_DIM, PAIR), lambda t: (0, 0, prev(t))),
            rows_spec(KV_W, prev),
        ]
        out_shape += [jax.ShapeDtypeStruct((n_rows, D), jnp.bfloat16),
                      jax.ShapeDtypeStruct((N_KV, HEAD_DIM, n_rows), jnp.bfloat16),
                      jax.ShapeDtypeStruct((n_rows, KV_W), jnp.bfloat16)]
    outs = pl.pallas_call(
        functools.partial(_layer_in_kernel, split_in=split_in, even=even, steps=steps),
        grid=(steps,),
        in_specs=in_specs,
        out_specs=out_specs,
        out_shape=out_shape,
        scratch_shapes=[pltpu.VMEM((PAIR, D), jnp.bfloat16)],
        compiler_params=_cparams(1),
        name="layer_in_even" if even else "layer_in_odd",
    )(*x_arrs, mods, mods, g1, wgu, wd, gm, *stage_arrs)
    return [o if o.shape[0] == N_KV and o.ndim == 3 else o.reshape(batch, ROWS, o.shape[-1])
            for o in outs]


def _layer_out_kernel(x_ref, *refs, n_mod, n_y, split_y, final):
    mod_refs = refs[:n_mod] * (2 // n_mod)
    wo_ref = refs[n_mod]
    y_refs = refs[n_mod + 1:n_mod + 1 + n_y]
    g2_ref, wgu_ref, wd_ref = refs[n_mod + 1 + n_y:n_mod + 4 + n_y]
    o_ref = refs[-1]
    if split_y:
        tile0 = 2 * pl.program_id(0)
        y = jnp.concatenate(
            [jnp.where((tile0 + i) % N_TILES == N_LAT_TILES, y_refs[2 * i + 1][...],
                       y_refs[2 * i][...]) for i in range(2)], axis=0)
    else:
        y = y_refs[0][...] if n_y == 1 else jnp.concatenate([r[...] for r in y_refs], axis=1)
    proj = jnp.dot(y, wo_ref[...], preferred_element_type=jnp.float32)
    x = x_ref[...] + _per_tile(lambda p, m: m[5:6, :] * p, proj, mod_refs)
    x = _ffn_half_step(x, mod_refs, 6, g2_ref[...], wgu_ref, wd_ref)
    if final:
        gf_ref = refs[-2]
        ms = jnp.mean(x * x, axis=-1, keepdims=True)
        x = x * lax.rsqrt(ms + EPS) * gf_ref[...]
    o_ref[...] = x


def _layer_out(xc, mods, wo, ys, g2, wgu, wd, layer, n_tiles, g_final=None, split_y=False):
    batch = xc.shape[0]
    j = layer // 2
    final = g_final is not None

    def const(block_shape, *index):
        return _resident(block_shape, lambda *_: index)

    if n_tiles == N_LAT_TILES:
        grid = (batch, N_LAT_TILES // 2)
        x_arr, x_spec = xc, pl.BlockSpec((None, PAIR, D), lambda b, k: (b, k, 0))
        mod_specs = [pl.BlockSpec((None, N_MOD, D), lambda b, k: (b, 0, 0))]
        y_arrs = list(ys)
        y_specs = [pl.BlockSpec((None, PAIR, y.shape[-1]), lambda b, k: (b, k, 0)) for y in ys]
        out_spec = pl.BlockSpec((None, PAIR, D), lambda b, k: (b, k, 0))
        out_shape = jax.ShapeDtypeStruct((batch, SEQ, D), jnp.float32)
    else:
        assert (batch * N_TILES) % 2 == 0
        grid = (batch * N_TILES // 2,)
        x_arr, x_spec = xc.reshape(batch * ROWS, D), pl.BlockSpec((PAIR, D), lambda k: (k, 0))

        def tile(k, i):
            t = 2 * k + i
            return t // N_TILES, t % N_TILES

        def mod_spec(i):
            def index(k):
                b, r = tile(k, i)
                return jnp.where(r == N_LAT_TILES, batch, b), 0, 0
            return pl.BlockSpec((None, N_MOD, D), index)

        mod_specs = [mod_spec(0), mod_spec(1)]
        if split_y:
            y_arrs, y_specs = [], []
            for i in range(2):
                def lat(k, i=i):
                    b, r = tile(k, i)
                    return b, jnp.minimum(r, N_LAT_TILES - 1), 0
                y_arrs += list(ys)
                y_specs += [pl.BlockSpec((None, TM, D), lat),
                            pl.BlockSpec((None, CTX, D), lambda k, i=i: (tile(k, i)[0], 0, 0))]
        else:
            y_arrs = [y.reshape(batch * ROWS, y.shape[-1]) for y in ys]
            y_specs = [pl.BlockSpec((PAIR, y.shape[-1]), lambda k: (k, 0)) for y in y_arrs]
        out_spec = pl.BlockSpec((PAIR, D), lambda k: (k, 0))
        out_shape = jax.ShapeDtypeStruct((batch * ROWS, D), jnp.float32)
    in_specs = [x_spec] + mod_specs + [const((None, D, D), j, 0, 0)] + y_specs + [
        const((None, 1, D), layer, 0, 0),
        const((None, D, 2 * D_FF), layer, 0, 0),
        const((None, D_FF, D), layer, 0, 0),
    ]
    args = [x_arr] + [mods] * len(mod_specs) + [wo] + y_arrs + [g2, wgu, wd]
    if final:
        in_specs.append(const((1, D), 0, 0))
        args.append(g_final)
    out = pl.pallas_call(
        functools.partial(_layer_out_kernel, n_mod=len(mod_specs), n_y=len(y_arrs),
                          split_y=split_y, final=final),
        grid=grid,
        in_specs=in_specs,
        out_specs=out_spec,
        out_shape=out_shape,
        compiler_params=_cparams(len(grid)),
        name="layer_out_final" if final else "layer_out",
    )(*args)
    return out.reshape(batch, -1, D)


HALF = SEQ // 2
FT = MXU_DIM
FW = MXU_DIM
N_FOLD_TILES = HALF // FT
DFT_SCALE = SEQ ** -0.5


def _dft_kernel(ch_ref, sh_ref, rev_ref, cc_ref, sc_ref, p_ref, q_ref, o_ref):
    rev = rev_ref[...]
    row0 = lax.broadcasted_iota(jnp.int32, (FT, FW), 0) == 0
    alt = (1 - 2 * (lax.broadcasted_iota(jnp.int32, (FT, FW), 0) & 1)).astype(jnp.float32)

    def reversed_upper(x_ref, t):
        src = HALF + FT * (N_FOLD_TILES - 1 - t)
        r = jnp.dot(rev, x_ref[src:src + FT, :], preferred_element_type=jnp.float32)
        if t == 0:
            return r
        first = HALF + FT * (N_FOLD_TILES - t)
        return jnp.where(row0, x_ref[first:first + 1, :].astype(jnp.float32), r)

    pfs, qfs, alt_sum = [], [], None
    for t in range(N_FOLD_TILES):
        rows = slice(t * FT, (t + 1) * FT)
        pf = p_ref[rows, :].astype(jnp.float32) + reversed_upper(p_ref, t)
        qf = q_ref[rows, :].astype(jnp.float32) - reversed_upper(q_ref, t)
        alt_sum = pf if alt_sum is None else alt_sum + pf
        pfs.append(pf.astype(jnp.bfloat16))
        qfs.append(qf.astype(jnp.bfloat16))
    p_mid = p_ref[HALF:HALF + 1, :].astype(jnp.float32)
    sign = jnp.concatenate([alt] * N_FOLD_TILES, axis=0)
    e = jnp.dot(ch_ref[...], jnp.concatenate(pfs, axis=0), preferred_element_type=jnp.float32)
    e = e + sign * (DFT_SCALE * p_mid)
    o = jnp.dot(sh_ref[...], jnp.concatenate(qfs, axis=0), preferred_element_type=jnp.float32)
    o_ref[0:HALF, :] = (e + o).astype(jnp.bfloat16)
    g = (e - o).astype(jnp.bfloat16)
    mid = DFT_SCALE * (jnp.sum(alt_sum * alt, axis=0, keepdims=True) + p_mid)
    for t in range(N_FOLD_TILES):
        src = FT * (N_FOLD_TILES - 1 - t)
        up = jnp.dot(rev, g[src:src + FT, :], preferred_element_type=jnp.float32)
        first = mid if t == 0 else g[src + FT:src + FT + 1, :].astype(jnp.float32)
        o_ref[HALF + t * FT:HALF + (t + 1) * FT, :] = jnp.where(row0, first, up).astype(jnp.bfloat16)
    ctx = jnp.dot(cc_ref[...], p_ref[SEQ:, :], preferred_element_type=jnp.float32)
    ctx = ctx + jnp.dot(sc_ref[...], q_ref[SEQ:, :], preferred_element_type=jnp.float32)
    o_ref[SEQ:, :] = ctx.astype(jnp.bfloat16)


def _dft(tables, pq):
    ch, sh, rev, cc, sc = tables
    batch = pq.shape[0]
    n_col = D_FOURIER // FW
    return pl.pallas_call(
        _dft_kernel,
        grid=(batch, n_col),
        in_specs=[
            _resident((HALF, HALF), lambda b, c: (0, 0)),
            _resident((HALF, HALF), lambda b, c: (0, 0)),
            _resident((FT, FT), lambda b, c: (0, 0)),
            _resident((CTX, CTX), lambda b, c: (0, 0)),
            _resident((CTX, CTX), lambda b, c: (0, 0)),
            pl.BlockSpec((None, ROWS, FW), lambda b, c: (b, 0, c)),
            pl.BlockSpec((None, ROWS, FW), lambda b, c: (b, 0, n_col + c)),
        ],
        out_specs=pl.BlockSpec((None, ROWS, FW), lambda b, c: (b, 0, c)),
        out_shape=jax.ShapeDtypeStruct((batch, ROWS, D_FOURIER), jnp.bfloat16),
        compiler_params=_cparams(2),
        name="dft",
    )(ch, sh, rev, cc, sc, pq, pq)


KEY_TILE = MXU_DIM
N_KEY_TILES = ROWS // KEY_TILE


def _with_row_sums(v):
    return jnp.concatenate([v, jnp.ones_like(v)], axis=1)


def _normalised(pv):
    return (pv[:, :HEAD_DIM] / pv[:, HEAD_DIM:]).astype(jnp.bfloat16)


def _attn_kernel(q_ref, kt_ref, v_ref, o_ref, s_ref, m_ref):
    @pl.when(pl.program_id(0) == 0)
    def _():
        s_ref[...] = jnp.zeros(s_ref.shape, s_ref.dtype)
        m_ref[...] = jnp.zeros(m_ref.shape, m_ref.dtype)

    outs = []
    for hd in range(GROUP):
        q = q_ref[:, hd * HEAD_DIM:(hd + 1) * HEAD_DIM]
        m_prev = m_ref[hd]
        m_lane = None
        acc = None
        for j in range(N_KEY_TILES):
            keys = slice(j * KEY_TILE, (j + 1) * KEY_TILE)
            s_old = s_ref[hd, :, keys]
            p = jnp.exp2(s_old - jnp.concatenate([m_prev] * (KEY_TILE // HEAD_DIM), axis=1))
            pv = jnp.dot(p.astype(jnp.bfloat16), _with_row_sums(v_ref[keys, :]),
                         preferred_element_type=jnp.float32)
            acc = pv if acc is None else acc + pv
            s_new = jnp.dot(q, kt_ref[:, keys], preferred_element_type=jnp.float32)
            s_ref[hd, :, keys] = s_new
            for c in range(KEY_TILE // HEAD_DIM):
                part = s_new[:, c * HEAD_DIM:(c + 1) * HEAD_DIM]
                m_lane = part if m_lane is None else jnp.maximum(m_lane, part)
        outs.append(_normalised(acc))
        m_ref[hd] = jnp.broadcast_to(jnp.max(m_lane, axis=-1, keepdims=True), m_lane.shape)
    o_ref[...] = jnp.concatenate(outs, axis=1)


def _attention_latent(q, kt, v):
    batch = q.shape[0]
    gw = GROUP * HEAD_DIM
    per_sample = N_KV * N_LAT_TILES
    steps = batch * per_sample + 1

    def unflatten(c):
        return c // per_sample, (c % per_sample) // N_LAT_TILES, c % N_LAT_TILES

    def cur(t):
        return unflatten(jnp.minimum(t, steps - 2))

    def prev(t):
        return unflatten(jnp.maximum(t - 1, 0))

    def q_map(t):
        b, g, i = cur(t)
        return b, i, g

    def kt_map(t):
        b, g, _ = cur(t)
        return g, 0, b

    def v_map(t):
        b, g, _ = prev(t)
        return b, 0, g

    def o_map(t):
        b, g, i = prev(t)
        return b, i, g

    return pl.pallas_call(
        _attn_kernel,
        grid=(steps,),
        in_specs=[
            pl.BlockSpec((None, TM, gw), q_map),
            pl.BlockSpec((None, HEAD_DIM, ROWS), kt_map),
            pl.BlockSpec((None, ROWS, HEAD_DIM), v_map),
        ],
        out_specs=pl.BlockSpec((None, TM, gw), o_map),
        out_shape=jax.ShapeDtypeStruct((batch, SEQ, D), jnp.bfloat16),
        scratch_shapes=[pltpu.VMEM((GROUP, TM, ROWS), jnp.float32),
                        pltpu.VMEM((GROUP, TM, HEAD_DIM), jnp.float32)],
        compiler_params=_cparams(1),
        name="attention",
    )(q, kt, v)


def _attn_ctx_kernel(q_ref, kt_ref, v_ref, o_ref):
    v1 = _with_row_sums(v_ref[...])
    kt = kt_ref[...]
    outs = []
    for hd in range(GROUP):
        s = jnp.dot(q_ref[:, hd * HEAD_DIM:(hd + 1) * HEAD_DIM], kt,
                    preferred_element_type=jnp.float32)
        p = jnp.exp2(s - jnp.max(s, axis=-1, keepdims=True)).astype(jnp.bfloat16)
        outs.append(_normalised(jnp.dot(p, v1, preferred_element_type=jnp.float32)))
    o_ref[...] = jnp.concatenate(outs, axis=1)


def _attention_context(q, kt, v):
    batch = q.shape[0]
    gw = GROUP * HEAD_DIM
    return pl.pallas_call(
        _attn_ctx_kernel,
        grid=(batch, N_KV),
        in_specs=[
            pl.BlockSpec((None, CTX, gw), lambda b, g: (b, SEQ // CTX, g)),
            pl.BlockSpec((None, HEAD_DIM, CTX), lambda b, g: (g, 0, b * N_TILES + N_LAT_TILES)),
            pl.BlockSpec((None, CTX, HEAD_DIM), lambda b, g: (b, SEQ // CTX, g)),
        ],
        out_specs=pl.BlockSpec((None, CTX, gw), lambda b, g: (b, 0, g)),
        out_shape=jax.ShapeDtypeStruct((batch, CTX, D), jnp.bfloat16),
        compiler_params=_cparams(2),
        name="attention_ctx",
    )(q, kt, v)


def _cos_sin_products(n, size):
    r = 1 << (int(math.log2(size)) // 2)
    hi = size // r
    j = jnp.arange(size, dtype=jnp.int32)[:, None]
    ang_hi = ((j * jnp.arange(hi, dtype=jnp.int32)[None, :] * r) % n).astype(jnp.float32)
    ang_lo = ((j * jnp.arange(r, dtype=jnp.int32)[None, :]) % n).astype(jnp.float32)
    w = 2.0 * math.pi / n
    ca, sa = jnp.cos(ang_hi * w)[:, :, None], jnp.sin(ang_hi * w)[:, :, None]
    cb, sb = jnp.cos(ang_lo * w)[:, None, :], jnp.sin(ang_lo * w)[:, None, :]
    scale = n ** -0.5
    cos = ((ca * cb - sa * sb) * scale).reshape(size, size)
    sin = ((sa * cb + ca * sb) * scale).reshape(size, size)
    return cos, sin


def _dft_tables():
    ch, sh = _cos_sin_products(SEQ, HALF)
    cc, sc = _cos_sin_products(CTX, CTX)
    r = jnp.arange(FT, dtype=jnp.int32)
    rev = ((r[:, None] + r[None, :]) == FT).astype(jnp.bfloat16)
    return tuple(t.astype(jnp.bfloat16) for t in (ch, sh, rev, cc, sc))


def _channel_dft():
    c, s = _cos_sin_products(GROUP_CH, GROUP_CH)
    return jnp.concatenate([c, -s], axis=1).astype(jnp.bfloat16)


def _rope_tables():
    half = HEAD_DIM // 2
    rows = SEQ // GRID_W
    row = jnp.repeat(jnp.arange(rows, dtype=jnp.float32), GRID_W)
    col = jnp.tile(jnp.arange(GRID_W, dtype=jnp.float32), rows)
    inv_freq = ROPE_THETA ** (-jnp.arange(0, half, 2, dtype=jnp.float32) / half)
    ang_row = row[:, None] * inv_freq
    ang_col = col[:, None] * inv_freq
    cos = jnp.concatenate([jnp.cos(ang_row)] * 2 + [jnp.cos(ang_col)] * 2, axis=1)
    sin = jnp.concatenate([-jnp.sin(ang_row), jnp.sin(ang_row),
                           -jnp.sin(ang_col), jnp.sin(ang_col)], axis=1)
    cos = jnp.concatenate([cos, jnp.ones((CTX, HEAD_DIM), jnp.float32)], axis=0)
    sin = jnp.concatenate([sin, jnp.zeros((CTX, HEAD_DIM), jnp.float32)], axis=0)
    order = _rope_order()
    return cos[:, order], sin[:, order]


def _rope_order():
    quarter = HEAD_DIM // 4
    blocks = [0, 2, 1, 3]
    return jnp.concatenate([jnp.arange(b * quarter, (b + 1) * quarter) for b in blocks])


def _permute_qk_heads(w_qkv, g_q, g_k):
    order = _rope_order()
    n_qk = (N_HEADS + N_KV) * HEAD_DIM
    qk = w_qkv[:, :, :n_qk].reshape(*w_qkv.shape[:2], N_HEADS + N_KV, HEAD_DIM)[..., order]
    w = jnp.concatenate([qk.reshape(*w_qkv.shape[:2], n_qk), w_qkv[:, :, n_qk:]], axis=-1)
    return w, g_q[:, order], g_k[:, order]


def kernel(x, c, ctx, c_ctx, w_mod, b_mod, g_ffn1, w_ffn1_gu, w_ffn1_down, g_mix, g_ffn2, w_ffn2_gu, w_ffn2_down, g_final, w_in_ab, g_v, w_s, b_s, w_out_ab, w_qkv, g_q, g_k, w_o):
    batch = x.shape[0]
    depth = w_mod.shape[0]
    assert x.shape == (batch, SEQ, D) and ctx.shape == (batch, CTX, D)
    assert depth % 2 == 0

    mod_rows = -(-(batch + 1) // 8) * 8
    cc = jnp.concatenate([c, c_ctx[None], jnp.zeros((mod_rows - batch - 1, D), jnp.float32)], axis=0)
    m = _modulation(cc, w_mod, b_mod)
    m = m[:, :batch + 1].reshape(depth, batch + 1, N_MOD, D)

    wgu1, wd1 = w_ffn1_gu.astype(jnp.bfloat16), w_ffn1_down.astype(jnp.bfloat16)
    wgu2, wd2 = w_ffn2_gu.astype(jnp.bfloat16), w_ffn2_down.astype(jnp.bfloat16)
    g1 = g_ffn1.reshape(depth, 1, D)
    g2 = g_ffn2.reshape(depth, 1, D)
    gm = g_mix.reshape(depth, 1, D)
    w_in = w_in_ab.astype(jnp.bfloat16)
    w_out = w_out_ab.astype(jnp.bfloat16)
    ws = w_s.astype(jnp.bfloat16)
    gv = g_v.reshape(-1, 1, D_SGU)
    bs_full = jnp.repeat(jnp.swapaxes(b_s, 1, 2), GROUP_CH, axis=2)
    w_qkv_p, g_q_p, g_k_p = _permute_qk_heads(w_qkv, g_q, g_k)
    wqkv = w_qkv_p.astype(jnp.bfloat16)
    wo = w_o.astype(jnp.bfloat16)
    gq = g_q_p.reshape(-1, 1, HEAD_DIM)
    gk = g_k_p.reshape(-1, 1, HEAD_DIM)
    dft_tables = _dft_tables()
    ccs = _channel_dft()
    cos_t, sin_t = _rope_tables()

    xs = (x, ctx)
    for l in range(depth):
        last = l == depth - 1
        even = l % 2 == 0
        mods = m[l]
        n_out = N_LAT_TILES if last else N_TILES
        if even:
            xc, pq, sgu = _layer_in(xs, mods, g1, wgu1, wd1, gm, l, True,
                                    (w_in, ccs, gv, ws, bs_full))
            ys, w_proj = [_dft(dft_tables, pq), sgu], w_out
        else:
            xc, q, kt, v = _layer_in(xs, mods, g1, wgu1, wd1, gm, l, False,
                                     (wqkv, gq, gk, cos_t, sin_t))
            ys, w_proj = [_attention_latent(q, kt, v)], wo
            if not last:
                ys.append(_attention_context(q, kt, v))
        xc = _layer_out(xc, mods, w_proj, ys, g2, wgu2, wd2, l, n_out,
                        g_final.reshape(1, D) if last else None,
                        split_y=not even and not last)
        xs = (xc,)
    return xc
```

```python
import functools
import math

import jax
import jax.numpy as jnp
from jax import lax
from jax.experimental import pallas as pl
from jax.experimental.pallas import tpu as pltpu

D = 1024
SEQ = 4096
CTX = 256
ROWS = SEQ + CTX
GRID_W = 64
D_FF = 2752
N_MOD = 9
EPS = 1e-6
HEAD_DIM = 128
N_HEADS = 8
N_KV = 2
GROUP = N_HEADS // N_KV
KV_W = N_KV * HEAD_DIM
ROPE_THETA = 10000.0
D_FOURIER = 512
D_SGU = 512
GROUP_CH = 128
N_GROUPS = 4
CHUNK = 128

MXU_DIM = 256
TM = 256
N_LAT_TILES = SEQ // TM
N_TILES = ROWS // TM
V7X_VMEM_BYTES = 64 * 1024 * 1024
SUBLANES = 8
F32, BF16 = 4, 2

assert TM == CTX and SEQ % TM == 0


def _cparams(n_axes, resident=0, streamed=0, scratch=0, temporaries=0):
    need = resident + 2 * streamed + scratch + temporaries
    assert need <= V7X_VMEM_BYTES, need
    return pltpu.CompilerParams(dimension_semantics=("arbitrary",) * n_axes,
                                vmem_limit_bytes=need)


def _resident(block_shape, index_map):
    return pl.BlockSpec(block_shape, index_map, pipeline_mode=pl.Buffered(1))


def _modnorm(x, g, shift, scale):
    ms = jnp.mean(x * x, axis=-1, keepdims=True)
    return x * lax.rsqrt(ms + EPS) * (g * (1.0 + scale)) + shift


def _ffn_norm(x, mod_ref, row0, g):
    h = _modnorm(x, g, mod_ref[row0:row0 + 1, :], mod_ref[row0 + 1:row0 + 2, :])
    return h.astype(jnp.bfloat16)


def _per_tile(fn, x, mod_refs):
    return jnp.concatenate([fn(x[i * TM:(i + 1) * TM], m) for i, m in enumerate(mod_refs)],
                           axis=0)


def _swiglu(h, wgu_ref, wd_ref):
    gu = jnp.dot(h, wgu_ref[...], preferred_element_type=jnp.float32)
    gt, up = gu[:, :D_FF], gu[:, D_FF:]
    a = (gt * jax.nn.sigmoid(gt) * up).astype(jnp.bfloat16)
    return jnp.dot(a, wd_ref[...], preferred_element_type=jnp.float32)


def _ffn_half_step(x, mod_refs, row0, g, wgu_ref, wd_ref, h=None):
    if h is None:
        h = _per_tile(lambda t, m: _ffn_norm(t, m, row0, g), x, mod_refs)
    y = _swiglu(h, wgu_ref, wd_ref)
    return x + _per_tile(lambda t, m: 0.5 * m[row0 + 2:row0 + 3, :] * t, y, mod_refs)


def _mod_kernel(cc_ref, w_ref, b_ref, o_ref):
    s = cc_ref[...]
    s = s * jax.nn.sigmoid(s)
    s_hi = s.astype(jnp.bfloat16)
    s_lo = (s - s_hi.astype(jnp.float32)).astype(jnp.bfloat16)
    w = w_ref[...]
    w_hi = w.astype(jnp.bfloat16)
    w_lo = (w - w_hi.astype(jnp.float32)).astype(jnp.bfloat16)
    rows = s.shape[0]
    both = jnp.dot(jnp.concatenate([s_hi, s_lo], axis=0), w_hi,
                   preferred_element_type=jnp.float32)
    cross = jnp.dot(s_hi, w_lo, preferred_element_type=jnp.float32)
    o_ref[...] = both[:rows] + both[rows:] + cross + b_ref[...]


def _modulation(cc, w_mod, b_mod):
    depth = w_mod.shape[0]
    rows = cc.shape[0]
    tn = D
    return pl.pallas_call(
        _mod_kernel,
        grid=(depth, N_MOD * D // tn),
        in_specs=[
            pl.BlockSpec((rows, D), lambda l, j: (0, 0)),
            pl.BlockSpec((None, D, tn), lambda l, j: (l, 0, j)),
            pl.BlockSpec((None, 1, tn), lambda l, j: (l, 0, j)),
        ],
        out_specs=pl.BlockSpec((None, rows, tn), lambda l, j: (l, 0, j)),
        out_shape=jax.ShapeDtypeStruct((depth, rows, N_MOD * D), jnp.float32),
        compiler_params=_cparams(2, streamed=(D + 2 * rows) * tn * F32,
                                 temporaries=2 * D * tn * BF16 + 4 * rows * tn * F32),
        name="modulation",
    )(cc, w_mod, b_mod.reshape(depth, 1, N_MOD * D))


def _even_in_stage(p, ccs_ref, gv_ref, ws_ref, bs_ref, pq_ref, sgu_ref):
    a = p[:, :D_FOURIER].astype(jnp.bfloat16)
    uv = jax.nn.gelu(p[:, D_FOURIER:], approximate=True)
    u = uv[:, :D_SGU]
    v = uv[:, D_SGU:]
    ccs = ccs_ref[...]
    ps, qs, gated = [], [], []
    n_chunks = p.shape[0] // CHUNK
    for grp in range(N_GROUPS):
        lo, hi = grp * GROUP_CH, (grp + 1) * GROUP_CH
        t = jnp.dot(a[:, lo:hi], ccs, preferred_element_type=jnp.float32)
        ps.append(t[:, :GROUP_CH])
        qs.append(t[:, GROUP_CH:])
        vg = v[:, lo:hi]
        ms = jnp.mean(vg * vg, axis=-1, keepdims=True)
        vh = (vg * lax.rsqrt(ms + EPS) * gv_ref[:, lo:hi]).astype(jnp.bfloat16)
        rhs = jnp.concatenate([vh[c * CHUNK:(c + 1) * CHUNK, :] for c in range(n_chunks)], axis=1)
        mixed = jnp.dot(ws_ref[grp], rhs, preferred_element_type=jnp.float32)
        mixed = jnp.concatenate(
            [mixed[:, c * GROUP_CH:(c + 1) * GROUP_CH] for c in range(n_chunks)], axis=0)
        bias = jnp.concatenate([bs_ref[:, lo:hi]] * n_chunks, axis=0)
        gated.append(u[:, lo:hi] * (mixed + bias))
    pq_ref[...] = jnp.concatenate(ps + qs, axis=1).astype(jnp.bfloat16)
    sgu_ref[...] = jnp.concatenate(gated, axis=1).astype(jnp.bfloat16)


def _rope(t, cos, sin_signed):
    return t * cos + pltpu.roll(t, HEAD_DIM // 2, 1) * sin_signed


def _qkv_stage(qkv, gq_ref, gk_ref, cos, sin, q_ref, kt_ref, v_ref):
    q_scale = (HEAD_DIM ** -0.5) * math.log2(math.e)

    def head(col, gain):
        t = qkv[:, col:col + HEAD_DIM]
        ms = jnp.mean(t * t, axis=-1, keepdims=True)
        return _rope(t * lax.rsqrt(ms + EPS) * gain, cos, sin)

    qs = [head(hd * HEAD_DIM, gq_ref[...]) * q_scale for hd in range(N_HEADS)]
    q_ref[...] = jnp.concatenate(qs, axis=1).astype(jnp.bfloat16)
    for hd in range(N_KV):
        kt_ref[hd] = head((N_HEADS + hd) * HEAD_DIM, gk_ref[...]).T.astype(jnp.bfloat16)
    v_ref[...] = qkv[:, (N_HEADS + N_KV) * HEAD_DIM:].astype(jnp.bfloat16)


PAIR = 2 * TM
_STAGE_COLS = D_FOURIER + 2 * D_SGU
assert _STAGE_COLS == D + 2 * KV_W
_FFN_WEIGHT_BYTES = 3 * D * D_FF * BF16
_FFN_TEMP_BYTES = PAIR * (2 * D_FF * F32 + D_FF * BF16 + D * BF16 + 2 * D * F32)


def _layer_in_kernel(*refs, split_in, even, steps):
    refs = list(refs)
    h_ref = refs.pop()
    t = pl.program_id(0)

    @pl.when(t == 0)
    def _():
        h_ref[...] = jnp.zeros(h_ref.shape, h_ref.dtype)

    if split_in:
        tile0 = 2 * jnp.minimum(t, steps - 2)
        x = jnp.concatenate(
            [jnp.where((tile0 + i) % N_TILES == N_LAT_TILES, refs[2 * i + 1][...], refs[2 * i][...])
             for i in range(2)], axis=0)
        del refs[:4]
    else:
        x = refs.pop(0)[...]
    mod_refs = refs[:2]
    g1_ref, wgu_ref, wd_ref, gm_ref, w_stage_ref = refs[2:7]
    n_out = 3 if even else 4
    stage_in = refs[7:-n_out]
    xo_ref, *stage_out = refs[-n_out:]
    proj = jnp.dot(h_ref[...], w_stage_ref[...], preferred_element_type=jnp.float32)
    if even:
        _even_in_stage(proj, *stage_in, *stage_out)
    else:
        gq_ref, gk_ref, cos0, sin0, cos1, sin1 = stage_in
        _qkv_stage(proj, gq_ref, gk_ref, jnp.concatenate([cos0[...], cos1[...]], axis=0),
                   jnp.concatenate([sin0[...], sin1[...]], axis=0), *stage_out)
    x = _ffn_half_step(x, mod_refs, 0, g1_ref[...], wgu_ref, wd_ref)
    xo_ref[...] = x
    gm = gm_ref[...]
    h_ref[...] = _per_tile(lambda r, m: _ffn_norm(r, m, 3, gm), x, mod_refs)


def _layer_in(xs, mods, g1, wgu, wd, gm, layer, even, stage_args):
    batch = xs[0].shape[0]
    j = layer // 2
    split_in = len(xs) == 2
    n_rows = batch * ROWS
    assert (batch * N_TILES) % 2 == 0
    steps = batch * N_TILES // 2 + 1

    def cur(t):
        return jnp.minimum(t, steps - 2)

    def prev(t):
        return jnp.maximum(t - 1, 0)

    def tile(pair, i):
        n = 2 * pair + i
        return n // N_TILES, n % N_TILES

    def rows_spec(width, which):
        return pl.BlockSpec((PAIR, width), lambda t: (which(t), 0))

    def const(block_shape, *index):
        return _resident(block_shape, lambda t: index)

    def mod_spec(i):
        def index(t):
            b, r = tile(cur(t), i)
            return jnp.where(r == N_LAT_TILES, batch, b), 0, 0
        return pl.BlockSpec((None, N_MOD, D), index)

    if split_in:
        x_arrs, x_specs = [], []
        for i in range(2):
            def lat(t, i=i):
                b, r = tile(cur(t), i)
                return b, jnp.minimum(r, N_LAT_TILES - 1), 0
            x_arrs += list(xs)
            x_specs += [pl.BlockSpec((None, TM, D), lat),
                        pl.BlockSpec((None, CTX, D), lambda t, i=i: (tile(cur(t), i)[0], 0, 0))]
    else:
        x_arrs, x_specs = [xs[0].reshape(n_rows, D)], [rows_spec(D, cur)]
    in_specs = x_specs + [
        mod_spec(0), mod_spec(1),
        const((None, 1, D), layer, 0, 0),
        const((None, D, 2 * D_FF), layer, 0, 0),
        const((None, D_FF, D), layer, 0, 0),
        const((None, 1, D), layer, 0, 0),
    ]
    out_specs = [rows_spec(D, cur)]
    out_shape = [jax.ShapeDtypeStruct((n_rows, D), jnp.float32)]
    if even:
        in_specs += [
            const((None, D, D_FOURIER + 2 * D_SGU), j, 0, 0),
            const((GROUP_CH, 2 * GROUP_CH), 0, 0),
            const((None, 1, D_SGU), j, 0, 0),
            const((None, N_GROUPS, CHUNK, CHUNK), j, 0, 0, 0),
            const((None, CHUNK, D_SGU), j, 0, 0),
        ]
        out_specs += [rows_spec(2 * D_FOURIER, prev), rows_spec(D_SGU, prev)]
        out_shape += [jax.ShapeDtypeStruct((n_rows, 2 * D_FOURIER), jnp.bfloat16),
                      jax.ShapeDtypeStruct((n_rows, D_SGU), jnp.bfloat16)]
        stage_arrs = list(stage_args)
    else:
        wqkv, gq, gk, cos_t, sin_t = stage_args

        def rope_spec(i):
            return pl.BlockSpec((TM, HEAD_DIM), lambda t: (tile(prev(t), i)[1], 0))

        in_specs += [
            const((None, D, D + 2 * KV_W), j, 0, 0),
            const((None, 1, HEAD_DIM), j, 0, 0),
            const((None, 1, HEAD_DIM), j, 0, 0),
            rope_spec(0), rope_spec(0), rope_spec(1), rope_spec(1),
        ]
        stage_arrs = [wqkv, gq, gk, cos_t, sin_t, cos_t, sin_t]
        out_specs += [
            rows_spec(D, prev),
            pl.BlockSpec((N_KV, HEAD_DIM, PAIR), lambda t: (0, 0, prev(t))),
            rows_spec(KV_W, prev),
        ]
        out_shape += [jax.ShapeDtypeStruct((n_rows, D), jnp.bfloat16),
                      jax.ShapeDtypeStruct((N_KV, HEAD_DIM, n_rows), jnp.bfloat16),
                      jax.ShapeDtypeStruct((n_rows, KV_W), jnp.bfloat16)]
    outs = pl.pallas_call(
        functools.partial(_layer_in_kernel, split_in=split_in, even=even, steps=steps),
        grid=(steps,),
        in_specs=in_specs,
        out_specs=out_specs,
        out_shape=out_shape,
        scratch_shapes=[pltpu.VMEM((PAIR, D), jnp.bfloat16)],
        compiler_params=_cparams(
            1, resident=_FFN_WEIGHT_BYTES + D * _STAGE_COLS * BF16,
            streamed=PAIR * (2 * D * F32 + (_STAGE_COLS + (0 if even else KV_W)) * BF16),
            scratch=PAIR * D * BF16,
            temporaries=_FFN_TEMP_BYTES + 3 * PAIR * _STAGE_COLS * F32),
        name="layer_in_even" if even else "layer_in_odd",
    )(*x_arrs, mods, mods, g1, wgu, wd, gm, *stage_arrs)
    return [o if o.shape[0] == N_KV and o.ndim == 3 else o.reshape(batch, ROWS, o.shape[-1])
            for o in outs]


def _layer_out_kernel(x_ref, *refs, n_mod, n_y, split_y, final):
    mod_refs = refs[:n_mod] * (2 // n_mod)
    wo_ref = refs[n_mod]
    y_refs = refs[n_mod + 1:n_mod + 1 + n_y]
    g2_ref, wgu_ref, wd_ref = refs[n_mod + 1 + n_y:n_mod + 4 + n_y]
    o_ref = refs[-1]
    if split_y:
        tile0 = 2 * pl.program_id(0)
        y = jnp.concatenate(
            [jnp.where((tile0 + i) % N_TILES == N_LAT_TILES, y_refs[2 * i + 1][...],
                       y_refs[2 * i][...]) for i in range(2)], axis=0)
    else:
        y = y_refs[0][...] if n_y == 1 else jnp.concatenate([r[...] for r in y_refs], axis=1)
    proj = jnp.dot(y, wo_ref[...], preferred_element_type=jnp.float32)
    x = x_ref[...] + _per_tile(lambda p, m: m[5:6, :] * p, proj, mod_refs)
    x = _ffn_half_step(x, mod_refs, 6, g2_ref[...], wgu_ref, wd_ref)
    if final:
        gf_ref = refs[-2]
        ms = jnp.mean(x * x, axis=-1, keepdims=True)
        x = x * lax.rsqrt(ms + EPS) * gf_ref[...]
    o_ref[...] = x


def _layer_out(xc, mods, wo, ys, g2, wgu, wd, layer, n_tiles, g_final=None, split_y=False):
    batch = xc.shape[0]
    j = layer // 2
    final = g_final is not None

    def const(block_shape, *index):
        return _resident(block_shape, lambda *_: index)

    if n_tiles == N_LAT_TILES:
        grid = (batch, N_LAT_TILES // 2)
        x_arr, x_spec = xc, pl.BlockSpec((None, PAIR, D), lambda b, k: (b, k, 0))
        mod_specs = [pl.BlockSpec((None, N_MOD, D), lambda b, k: (b, 0, 0))]
        y_arrs = list(ys)
        y_specs = [pl.BlockSpec((None, PAIR, y.shape[-1]), lambda b, k: (b, k, 0)) for y in ys]
        out_spec = pl.BlockSpec((None, PAIR, D), lambda b, k: (b, k, 0))
        out_shape = jax.ShapeDtypeStruct((batch, SEQ, D), jnp.float32)
    else:
        assert (batch * N_TILES) % 2 == 0
        grid = (batch * N_TILES // 2,)
        x_arr, x_spec = xc.reshape(batch * ROWS, D), pl.BlockSpec((PAIR, D), lambda k: (k, 0))

        def tile(k, i):
            t = 2 * k + i
            return t // N_TILES, t % N_TILES

        def mod_spec(i):
            def index(k):
                b, r = tile(k, i)
                return jnp.where(r == N_LAT_TILES, batch, b), 0, 0
            return pl.BlockSpec((None, N_MOD, D), index)

        mod_specs = [mod_spec(0), mod_spec(1)]
        if split_y:
            y_arrs, y_specs = [], []
            for i in range(2):
                def lat(k, i=i):
                    b, r = tile(k, i)
                    return b, jnp.minimum(r, N_LAT_TILES - 1), 0
                y_arrs += list(ys)
                y_specs += [pl.BlockSpec((None, TM, D), lat),
                            pl.BlockSpec((None, CTX, D), lambda k, i=i: (tile(k, i)[0], 0, 0))]
        else:
            y_arrs = [y.reshape(batch * ROWS, y.shape[-1]) for y in ys]
            y_specs = [pl.BlockSpec((PAIR, y.shape[-1]), lambda k: (k, 0)) for y in y_arrs]
        out_spec = pl.BlockSpec((PAIR, D), lambda k: (k, 0))
        out_shape = jax.ShapeDtypeStruct((batch * ROWS, D), jnp.float32)
    in_specs = [x_spec] + mod_specs + [const((None, D, D), j, 0, 0)] + y_specs + [
        const((None, 1, D), layer, 0, 0),
        const((None, D, 2 * D_FF), layer, 0, 0),
        const((None, D_FF, D), layer, 0, 0),
    ]
    args = [x_arr] + [mods] * len(mod_specs) + [wo] + y_arrs + [g2, wgu, wd]
    if final:
        in_specs.append(const((1, D), 0, 0))
        args.append(g_final)
    out = pl.pallas_call(
        functools.partial(_layer_out_kernel, n_mod=len(mod_specs), n_y=len(y_arrs),
                          split_y=split_y, final=final),
        grid=grid,
        in_specs=in_specs,
        out_specs=out_spec,
        out_shape=out_shape,
        compiler_params=_cparams(
            len(grid), resident=_FFN_WEIGHT_BYTES + D * D * BF16,
            streamed=PAIR * (2 * D * F32 + D * BF16),
            temporaries=_FFN_TEMP_BYTES + 2 * PAIR * D * F32),
        name="layer_out_final" if final else "layer_out",
    )(*args)
    return out.reshape(batch, -1, D)


HALF = SEQ // 2
FT = MXU_DIM
FW = MXU_DIM
N_FOLD_TILES = HALF // FT
DFT_SCALE = SEQ ** -0.5


def _dft_kernel(ch_ref, sh_ref, rev_ref, cc_ref, sc_ref, p_ref, q_ref, o_ref):
    rev = rev_ref[...]
    row0 = lax.broadcasted_iota(jnp.int32, (FT, FW), 0) == 0
    alt = (1 - 2 * (lax.broadcasted_iota(jnp.int32, (FT, FW), 0) & 1)).astype(jnp.float32)

    def reversed_upper(x_ref, t):
        src = HALF + FT * (N_FOLD_TILES - 1 - t)
        r = jnp.dot(rev, x_ref[src:src + FT, :], preferred_element_type=jnp.float32)
        if t == 0:
            return r
        first = HALF + FT * (N_FOLD_TILES - t)
        return jnp.where(row0, x_ref[first:first + 1, :].astype(jnp.float32), r)

    pfs, qfs, alt_sum = [], [], None
    for t in range(N_FOLD_TILES):
        rows = slice(t * FT, (t + 1) * FT)
        pf = p_ref[rows, :].astype(jnp.float32) + reversed_upper(p_ref, t)
        qf = q_ref[rows, :].astype(jnp.float32) - reversed_upper(q_ref, t)
        alt_sum = pf if alt_sum is None else alt_sum + pf
        pfs.append(pf.astype(jnp.bfloat16))
        qfs.append(qf.astype(jnp.bfloat16))
    p_mid = p_ref[HALF:HALF + 1, :].astype(jnp.float32)
    sign = jnp.concatenate([alt] * N_FOLD_TILES, axis=0)
    e = jnp.dot(ch_ref[...], jnp.concatenate(pfs, axis=0), preferred_element_type=jnp.float32)
    e = e + sign * (DFT_SCALE * p_mid)
    o = jnp.dot(sh_ref[...], jnp.concatenate(qfs, axis=0), preferred_element_type=jnp.float32)
    o_ref[0:HALF, :] = (e + o).astype(jnp.bfloat16)
    g = (e - o).astype(jnp.bfloat16)
    mid = DFT_SCALE * (jnp.sum(alt_sum * alt, axis=0, keepdims=True) + p_mid)
    for t in range(N_FOLD_TILES):
        src = FT * (N_FOLD_TILES - 1 - t)
        up = jnp.dot(rev, g[src:src + FT, :], preferred_element_type=jnp.float32)
        first = mid if t == 0 else g[src + FT:src + FT + 1, :].astype(jnp.float32)
        o_ref[HALF + t * FT:HALF + (t + 1) * FT, :] = jnp.where(row0, first, up).astype(jnp.bfloat16)
    ctx = jnp.dot(cc_ref[...], p_ref[SEQ:, :], preferred_element_type=jnp.float32)
    ctx = ctx + jnp.dot(sc_ref[...], q_ref[SEQ:, :], preferred_element_type=jnp.float32)
    o_ref[SEQ:, :] = ctx.astype(jnp.bfloat16)


def _dft(tables, pq):
    ch, sh, rev, cc, sc = tables
    batch = pq.shape[0]
    n_col = D_FOURIER // FW
    return pl.pallas_call(
        _dft_kernel,
        grid=(batch, n_col),
        in_specs=[
            _resident((HALF, HALF), lambda b, c: (0, 0)),
            _resident((HALF, HALF), lambda b, c: (0, 0)),
            _resident((FT, FT), lambda b, c: (0, 0)),
            _resident((CTX, CTX), lambda b, c: (0, 0)),
            _resident((CTX, CTX), lambda b, c: (0, 0)),
            pl.BlockSpec((None, ROWS, FW), lambda b, c: (b, 0, c)),
            pl.BlockSpec((None, ROWS, FW), lambda b, c: (b, 0, n_col + c)),
        ],
        out_specs=pl.BlockSpec((None, ROWS, FW), lambda b, c: (b, 0, c)),
        out_shape=jax.ShapeDtypeStruct((batch, ROWS, D_FOURIER), jnp.bfloat16),
        compiler_params=_cparams(
            2, resident=2 * HALF * HALF * BF16, streamed=3 * ROWS * FW * BF16,
            temporaries=HALF * FW * (2 * BF16 + 3 * F32) + 2 * ROWS * FW * F32),
        name="dft",
    )(ch, sh, rev, cc, sc, pq, pq)


KEY_TILE = MXU_DIM
N_KEY_TILES = ROWS // KEY_TILE


def _with_row_sums(v):
    return jnp.concatenate([v, jnp.ones_like(v)], axis=1)


def _normalised(pv):
    return (pv[:, :HEAD_DIM] / pv[:, HEAD_DIM:]).astype(jnp.bfloat16)


def _attn_kernel(q_ref, kt_ref, v_ref, o_ref, s_ref, m_ref):
    @pl.when(pl.program_id(0) == 0)
    def _():
        s_ref[...] = jnp.zeros(s_ref.shape, s_ref.dtype)
        m_ref[...] = jnp.zeros(m_ref.shape, m_ref.dtype)

    outs = []
    for hd in range(GROUP):
        q = q_ref[:, hd * HEAD_DIM:(hd + 1) * HEAD_DIM]
        m_prev = m_ref[hd]
        m_lane = None
        acc = None
        for j in range(N_KEY_TILES):
            keys = slice(j * KEY_TILE, (j + 1) * KEY_TILE)
            s_old = s_ref[hd, :, keys]
            p = jnp.exp2(s_old - jnp.concatenate([m_prev] * (KEY_TILE // HEAD_DIM), axis=1))
            pv = jnp.dot(p.astype(jnp.bfloat16), _with_row_sums(v_ref[keys, :]),
                         preferred_element_type=jnp.float32)
            acc = pv if acc is None else acc + pv
            s_new = jnp.dot(q, kt_ref[:, keys], preferred_element_type=jnp.float32)
            s_ref[hd, :, keys] = s_new
            for c in range(KEY_TILE // HEAD_DIM):
                part = s_new[:, c * HEAD_DIM:(c + 1) * HEAD_DIM]
                m_lane = part if m_lane is None else jnp.maximum(m_lane, part)
        outs.append(_normalised(acc))
        m_ref[hd] = jnp.broadcast_to(jnp.max(m_lane, axis=-1, keepdims=True), m_lane.shape)
    o_ref[...] = jnp.concatenate(outs, axis=1)


def _attention_latent(q, kt, v):
    batch = q.shape[0]
    gw = GROUP * HEAD_DIM
    per_sample = N_KV * N_LAT_TILES
    steps = batch * per_sample + 1

    def unflatten(c):
        return c // per_sample, (c % per_sample) // N_LAT_TILES, c % N_LAT_TILES

    def cur(t):
        return unflatten(jnp.minimum(t, steps - 2))

    def prev(t):
        return unflatten(jnp.maximum(t - 1, 0))

    def q_map(t):
        b, g, i = cur(t)
        return b, i, g

    def kt_map(t):
        b, g, _ = cur(t)
        return g, 0, b

    def v_map(t):
        b, g, _ = prev(t)
        return b, 0, g

    def o_map(t):
        b, g, i = prev(t)
        return b, i, g

    return pl.pallas_call(
        _attn_kernel,
        grid=(steps,),
        in_specs=[
            pl.BlockSpec((None, TM, gw), q_map),
            pl.BlockSpec((None, HEAD_DIM, ROWS), kt_map),
            pl.BlockSpec((None, ROWS, HEAD_DIM), v_map),
        ],
        out_specs=pl.BlockSpec((None, TM, gw), o_map),
        out_shape=jax.ShapeDtypeStruct((batch, SEQ, D), jnp.bfloat16),
        scratch_shapes=[pltpu.VMEM((GROUP, TM, ROWS), jnp.float32),
                        pltpu.VMEM((GROUP, TM, HEAD_DIM), jnp.float32)],
        compiler_params=_cparams(
            1, streamed=2 * (TM * gw + ROWS * HEAD_DIM) * BF16,
            scratch=GROUP * TM * (ROWS + HEAD_DIM) * F32,
            temporaries=GROUP * TM * 2 * HEAD_DIM * F32 + 4 * TM * KEY_TILE * F32),
        name="attention",
    )(q, kt, v)


def _attn_ctx_kernel(q_ref, kt_ref, v_ref, o_ref):
    v1 = _with_row_sums(v_ref[...])
    kt = kt_ref[...]
    outs = []
    for hd in range(GROUP):
        s = jnp.dot(q_ref[:, hd * HEAD_DIM:(hd + 1) * HEAD_DIM], kt,
                    preferred_element_type=jnp.float32)
        p = jnp.exp2(s - jnp.max(s, axis=-1, keepdims=True)).astype(jnp.bfloat16)
        outs.append(_normalised(jnp.dot(p, v1, preferred_element_type=jnp.float32)))
    o_ref[...] = jnp.concatenate(outs, axis=1)


def _attention_context(q, kt, v):
    batch = q.shape[0]
    gw = GROUP * HEAD_DIM
    return pl.pallas_call(
        _attn_ctx_kernel,
        grid=(batch, N_KV),
        in_specs=[
            pl.BlockSpec((None, CTX, gw), lambda b, g: (b, SEQ // CTX, g)),
            pl.BlockSpec((None, HEAD_DIM, CTX), lambda b, g: (g, 0, b * N_TILES + N_LAT_TILES)),
            pl.BlockSpec((None, CTX, HEAD_DIM), lambda b, g: (b, SEQ // CTX, g)),
        ],
        out_specs=pl.BlockSpec((None, CTX, gw), lambda b, g: (b, 0, g)),
        out_shape=jax.ShapeDtypeStruct((batch, CTX, D), jnp.bfloat16),
        compiler_params=_cparams(
            2, streamed=2 * CTX * (gw + HEAD_DIM) * BF16,
            temporaries=GROUP * CTX * (2 * CTX + 2 * HEAD_DIM) * F32),
        name="attention_ctx",
    )(q, kt, v)


def _cos_sin_products(n, size):
    r = 1 << (int(math.log2(size)) // 2)
    hi = size // r
    j = jnp.arange(size, dtype=jnp.int32)[:, None]
    ang_hi = ((j * jnp.arange(hi, dtype=jnp.int32)[None, :] * r) % n).astype(jnp.float32)
    ang_lo = ((j * jnp.arange(r, dtype=jnp.int32)[None, :]) % n).astype(jnp.float32)
    w = 2.0 * math.pi / n
    ca, sa = jnp.cos(ang_hi * w)[:, :, None], jnp.sin(ang_hi * w)[:, :, None]
    cb, sb = jnp.cos(ang_lo * w)[:, None, :], jnp.sin(ang_lo * w)[:, None, :]
    scale = n ** -0.5
    cos = ((ca * cb - sa * sb) * scale).reshape(size, size)
    sin = ((sa * cb + ca * sb) * scale).reshape(size, size)
    return cos, sin


def _dft_tables():
    ch, sh = _cos_sin_products(SEQ, HALF)
    cc, sc = _cos_sin_products(CTX, CTX)
    r = jnp.arange(FT, dtype=jnp.int32)
    rev = ((r[:, None] + r[None, :]) == FT).astype(jnp.bfloat16)
    return tuple(t.astype(jnp.bfloat16) for t in (ch, sh, rev, cc, sc))


def _channel_dft():
    c, s = _cos_sin_products(GROUP_CH, GROUP_CH)
    return jnp.concatenate([c, -s], axis=1).astype(jnp.bfloat16)


def _rope_tables():
    half = HEAD_DIM // 2
    rows = SEQ // GRID_W
    row = jnp.repeat(jnp.arange(rows, dtype=jnp.float32), GRID_W)
    col = jnp.tile(jnp.arange(GRID_W, dtype=jnp.float32), rows)
    inv_freq = ROPE_THETA ** (-jnp.arange(0, half, 2, dtype=jnp.float32) / half)
    ang_row = row[:, None] * inv_freq
    ang_col = col[:, None] * inv_freq
    cos = jnp.concatenate([jnp.cos(ang_row)] * 2 + [jnp.cos(ang_col)] * 2, axis=1)
    sin = jnp.concatenate([-jnp.sin(ang_row), jnp.sin(ang_row),
                           -jnp.sin(ang_col), jnp.sin(ang_col)], axis=1)
    cos = jnp.concatenate([cos, jnp.ones((CTX, HEAD_DIM), jnp.float32)], axis=0)
    sin = jnp.concatenate([sin, jnp.zeros((CTX, HEAD_DIM), jnp.float32)], axis=0)
    order = _rope_order()
    return cos[:, order], sin[:, order]


def _rope_order():
    quarter = HEAD_DIM // 4
    blocks = [0, 2, 1, 3]
    return jnp.concatenate([jnp.arange(b * quarter, (b + 1) * quarter) for b in blocks])


def _permute_qk_heads(w_qkv, g_q, g_k):
    order = _rope_order()
    n_qk = (N_HEADS + N_KV) * HEAD_DIM
    qk = w_qkv[:, :, :n_qk].reshape(*w_qkv.shape[:2], N_HEADS + N_KV, HEAD_DIM)[..., order]
    w = jnp.concatenate([qk.reshape(*w_qkv.shape[:2], n_qk), w_qkv[:, :, n_qk:]], axis=-1)
    return w, g_q[:, order], g_k[:, order]


def kernel(x, c, ctx, c_ctx, w_mod, b_mod, g_ffn1, w_ffn1_gu, w_ffn1_down, g_mix, g_ffn2, w_ffn2_gu, w_ffn2_down, g_final, w_in_ab, g_v, w_s, b_s, w_out_ab, w_qkv, g_q, g_k, w_o):
    batch = x.shape[0]
    depth = w_mod.shape[0]
    assert x.shape == (batch, SEQ, D) and ctx.shape == (batch, CTX, D)
    assert depth % 2 == 0

    mod_rows = -(-(batch + 1) // SUBLANES) * SUBLANES
    cc = jnp.concatenate([c, c_ctx[None], jnp.zeros((mod_rows - batch - 1, D), jnp.float32)], axis=0)
    m = _modulation(cc, w_mod, b_mod)
    m = m[:, :batch + 1].reshape(depth, batch + 1, N_MOD, D)

    wgu1, wd1 = w_ffn1_gu.astype(jnp.bfloat16), w_ffn1_down.astype(jnp.bfloat16)
    wgu2, wd2 = w_ffn2_gu.astype(jnp.bfloat16), w_ffn2_down.astype(jnp.bfloat16)
    g1 = g_ffn1.reshape(depth, 1, D)
    g2 = g_ffn2.reshape(depth, 1, D)
    gm = g_mix.reshape(depth, 1, D)
    w_in = w_in_ab.astype(jnp.bfloat16)
    w_out = w_out_ab.astype(jnp.bfloat16)
    ws = w_s.astype(jnp.bfloat16)
    gv = g_v.reshape(-1, 1, D_SGU)
    bs_full = jnp.repeat(jnp.swapaxes(b_s, 1, 2), GROUP_CH, axis=2)
    w_qkv_p, g_q_p, g_k_p = _permute_qk_heads(w_qkv, g_q, g_k)
    wqkv = w_qkv_p.astype(jnp.bfloat16)
    wo = w_o.astype(jnp.bfloat16)
    gq = g_q_p.reshape(-1, 1, HEAD_DIM)
    gk = g_k_p.reshape(-1, 1, HEAD_DIM)
    dft_tables = _dft_tables()
    ccs = _channel_dft()
    cos_t, sin_t = _rope_tables()

    xs = (x, ctx)
    for l in range(depth):
        last = l == depth - 1
        even = l % 2 == 0
        mods = m[l]
        n_out = N_LAT_TILES if last else N_TILES
        if even:
            xc, pq, sgu = _layer_in(xs, mods, g1, wgu1, wd1, gm, l, True,
                                    (w_in, ccs, gv, ws, bs_full))
            ys, w_proj = [_dft(dft_tables, pq), sgu], w_out
        else:
            xc, q, kt, v = _layer_in(xs, mods, g1, wgu1, wd1, gm, l, False,
                                     (wqkv, gq, gk, cos_t, sin_t))
            ys, w_proj = [_attention_latent(q, kt, v)], wo
            if not last:
                ys.append(_attention_context(q, kt, v))
        xc = _layer_out(xc, mods, w_proj, ys, g2, wgu2, wd2, l, n_out,
                        g_final.reshape(1, D) if last else None,
                        split_y=not even and not last)
        xs = (xc,)
    return xc
```

```python
import functools
import math

import jax
import jax.numpy as jnp
from jax import lax
from jax.experimental import pallas as pl
from jax.experimental.pallas import tpu as pltpu

D = 1024
SEQ = 4096
CTX = 256
ROWS = SEQ + CTX
GRID_W = 64
D_FF = 2752
N_MOD = 9
EPS = 1e-6
HEAD_DIM = 128
N_HEADS = 8
N_KV = 2
GROUP = N_HEADS // N_KV
KV_W = N_KV * HEAD_DIM
ROPE_THETA = 10000.0
D_FOURIER = 512
D_SGU = 512
GROUP_CH = 128
N_GROUPS = 4
CHUNK = 128

MXU_DIM = 256
TM = 256
N_LAT_TILES = SEQ // TM
N_TILES = ROWS // TM
V7X_VMEM_BYTES = 64 * 1024 * 1024
SUBLANES = 8
F32, BF16 = 4, 2

assert TM == CTX and SEQ % TM == 0


def _cparams(n_axes, resident=0, streamed=0, scratch=0, temporaries=0):
    need = resident + 2 * streamed + scratch + temporaries
    assert need <= V7X_VMEM_BYTES, need
    return pltpu.CompilerParams(dimension_semantics=("arbitrary",) * n_axes,
                                vmem_limit_bytes=need)


def _resident(block_shape, index_map):
    return pl.BlockSpec(block_shape, index_map, pipeline_mode=pl.Buffered(1))


def _modnorm(x, g, shift, scale):
    ms = jnp.mean(x * x, axis=-1, keepdims=True)
    return x * lax.rsqrt(ms + EPS) * (g * (1.0 + scale)) + shift


def _ffn_norm(x, mod_ref, row0, g):
    h = _modnorm(x, g, mod_ref[row0:row0 + 1, :], mod_ref[row0 + 1:row0 + 2, :])
    return h.astype(jnp.bfloat16)


def _per_tile(fn, x, mod_refs):
    return jnp.concatenate([fn(x[i * TM:(i + 1) * TM], m) for i, m in enumerate(mod_refs)],
                           axis=0)


def _swiglu(h, wgu_ref, wd_ref):
    gu = jnp.dot(h, wgu_ref[...], preferred_element_type=jnp.float32)
    gt, up = gu[:, :D_FF], gu[:, D_FF:]
    a = (gt * jax.nn.sigmoid(gt) * up).astype(jnp.bfloat16)
    return jnp.dot(a, wd_ref[...], preferred_element_type=jnp.float32)


def _ffn_half_step(x, mod_refs, row0, g, wgu_ref, wd_ref, h=None):
    if h is None:
        h = _per_tile(lambda t, m: _ffn_norm(t, m, row0, g), x, mod_refs)
    y = _swiglu(h, wgu_ref, wd_ref)
    return x + _per_tile(lambda t, m: 0.5 * m[row0 + 2:row0 + 3, :] * t, y, mod_refs)


def _mod_kernel(cc_ref, w_ref, b_ref, o_ref):
    s = cc_ref[...]
    s = s * jax.nn.sigmoid(s)
    s_hi = s.astype(jnp.bfloat16)
    s_lo = (s - s_hi.astype(jnp.float32)).astype(jnp.bfloat16)
    w = w_ref[...]
    w_hi = w.astype(jnp.bfloat16)
    w_lo = (w - w_hi.astype(jnp.float32)).astype(jnp.bfloat16)
    rows = s.shape[0]
    both = jnp.dot(jnp.concatenate([s_hi, s_lo], axis=0), w_hi,
                   preferred_element_type=jnp.float32)
    cross = jnp.dot(s_hi, w_lo, preferred_element_type=jnp.float32)
    o_ref[...] = both[:rows] + both[rows:] + cross + b_ref[...]


def _modulation(cc, w_mod, b_mod):
    depth = w_mod.shape[0]
    rows = cc.shape[0]
    tn = D
    return pl.pallas_call(
        _mod_kernel,
        grid=(depth, N_MOD * D // tn),
        in_specs=[
            pl.BlockSpec((rows, D), lambda l, j: (0, 0)),
            pl.BlockSpec((None, D, tn), lambda l, j: (l, 0, j)),
            pl.BlockSpec((None, 1, tn), lambda l, j: (l, 0, j)),
        ],
        out_specs=pl.BlockSpec((None, rows, tn), lambda l, j: (l, 0, j)),
        out_shape=jax.ShapeDtypeStruct((depth, rows, N_MOD * D), jnp.float32),
        compiler_params=_cparams(2, streamed=(D + 2 * rows) * tn * F32,
                                 temporaries=2 * D * tn * BF16 + 4 * rows * tn * F32),
        name="modulation",
    )(cc, w_mod, b_mod.reshape(depth, 1, N_MOD * D))


def _even_in_stage(p, ccs_ref, gv_ref, ws_ref, bs_ref, pq_ref, sgu_ref):
    a = p[:, :D_FOURIER].astype(jnp.bfloat16)
    uv = jax.nn.gelu(p[:, D_FOURIER:], approximate=True)
    u = uv[:, :D_SGU]
    v = uv[:, D_SGU:]
    ccs = ccs_ref[...]
    ps, qs, gated = [], [], []
    n_chunks = p.shape[0] // CHUNK
    for grp in range(N_GROUPS):
        lo, hi = grp * GROUP_CH, (grp + 1) * GROUP_CH
        t = jnp.dot(a[:, lo:hi], ccs, preferred_element_type=jnp.float32)
        ps.append(t[:, :GROUP_CH])
        qs.append(t[:, GROUP_CH:])
        vg = v[:, lo:hi]
        ms = jnp.mean(vg * vg, axis=-1, keepdims=True)
        vh = (vg * lax.rsqrt(ms + EPS) * gv_ref[:, lo:hi]).astype(jnp.bfloat16)
        rhs = jnp.concatenate([vh[c * CHUNK:(c + 1) * CHUNK, :] for c in range(n_chunks)], axis=1)
        mixed = jnp.dot(ws_ref[grp], rhs, preferred_element_type=jnp.float32)
        mixed = jnp.concatenate(
            [mixed[:, c * GROUP_CH:(c + 1) * GROUP_CH] for c in range(n_chunks)], axis=0)
        bias = jnp.concatenate([bs_ref[:, lo:hi]] * n_chunks, axis=0)
        gated.append(u[:, lo:hi] * (mixed + bias))
    pq_ref[...] = jnp.concatenate(ps + qs, axis=1).astype(jnp.bfloat16)
    sgu_ref[...] = jnp.concatenate(gated, axis=1).astype(jnp.bfloat16)


def _rope(t, cos, sin_signed):
    return t * cos + pltpu.roll(t, HEAD_DIM // 2, 1) * sin_signed


def _qkv_stage(qkv, gq_ref, gk_ref, cos, sin, q_ref, kt_ref, v_ref):
    q_scale = (HEAD_DIM ** -0.5) * math.log2(math.e)

    def head(col, gain):
        t = qkv[:, col:col + HEAD_DIM]
        ms = jnp.mean(t * t, axis=-1, keepdims=True)
        return _rope(t * lax.rsqrt(ms + EPS) * gain, cos, sin)

    qs = [head(hd * HEAD_DIM, gq_ref[...]) * q_scale for hd in range(N_HEADS)]
    q_ref[...] = jnp.concatenate(qs, axis=1).astype(jnp.bfloat16)
    for hd in range(N_KV):
        kt_ref[hd] = head((N_HEADS + hd) * HEAD_DIM, gk_ref[...]).T.astype(jnp.bfloat16)
    v_ref[...] = qkv[:, (N_HEADS + N_KV) * HEAD_DIM:].astype(jnp.bfloat16)


PAIR = 2 * TM
_STAGE_COLS = D_FOURIER + 2 * D_SGU
assert _STAGE_COLS == D + 2 * KV_W
_FFN_WEIGHT_BYTES = 3 * D * D_FF * BF16
_FFN_TEMP_BYTES = PAIR * (2 * D_FF * F32 + D_FF * BF16 + D * BF16 + 2 * D * F32)


def _layer_in_kernel(*refs, split_in, even, steps):
    refs = list(refs)
    h_ref = refs.pop()
    t = pl.program_id(0)

    @pl.when(t == 0)
    def _():
        h_ref[...] = jnp.zeros(h_ref.shape, h_ref.dtype)

    if split_in:
        tile0 = 2 * jnp.minimum(t, steps - 2)
        x = jnp.concatenate(
            [jnp.where((tile0 + i) % N_TILES == N_LAT_TILES, refs[2 * i + 1][...], refs[2 * i][...])
             for i in range(2)], axis=0)
        del refs[:4]
    else:
        x = refs.pop(0)[...]
    mod_refs = refs[:2]
    g1_ref, wgu_ref, wd_ref, gm_ref, w_stage_ref = refs[2:7]
    n_out = 3 if even else 4
    stage_in = refs[7:-n_out]
    xo_ref, *stage_out = refs[-n_out:]
    proj = jnp.dot(h_ref[...], w_stage_ref[...], preferred_element_type=jnp.float32)
    if even:
        _even_in_stage(proj, *stage_in, *stage_out)
    else:
        gq_ref, gk_ref, cos0, sin0, cos1, sin1 = stage_in
        _qkv_stage(proj, gq_ref, gk_ref, jnp.concatenate([cos0[...], cos1[...]], axis=0),
                   jnp.concatenate([sin0[...], sin1[...]], axis=0), *stage_out)
    x = _ffn_half_step(x, mod_refs, 0, g1_ref[...], wgu_ref, wd_ref)
    xo_ref[...] = x
    gm = gm_ref[...]
    h_ref[...] = _per_tile(lambda r, m: _ffn_norm(r, m, 3, gm), x, mod_refs)


def _layer_in(xs, mods, g1, wgu, wd, gm, layer, even, stage_args):
    batch = xs[0].shape[0]
    j = layer // 2
    split_in = len(xs) == 2
    n_rows = batch * ROWS
    assert (batch * N_TILES) % 2 == 0
    steps = batch * N_TILES // 2 + 1

    def cur(t):
        return jnp.minimum(t, steps - 2)

    def prev(t):
        return jnp.maximum(t - 1, 0)

    def tile(pair, i):
        n = 2 * pair + i
        return n // N_TILES, n % N_TILES

    def rows_spec(width, which):
        return pl.BlockSpec((PAIR, width), lambda t: (which(t), 0))

    def const(block_shape, *index):
        return _resident(block_shape, lambda t: index)

    def mod_spec(i):
        def index(t):
            b, r = tile(cur(t), i)
            return jnp.where(r == N_LAT_TILES, batch, b), 0, 0
        return pl.BlockSpec((None, N_MOD, D), index)

    if split_in:
        x_arrs, x_specs = [], []
        for i in range(2):
            def lat(t, i=i):
                b, r = tile(cur(t), i)
                return b, jnp.minimum(r, N_LAT_TILES - 1), 0
            x_arrs += list(xs)
            x_specs += [pl.BlockSpec((None, TM, D), lat),
                        pl.BlockSpec((None, CTX, D), lambda t, i=i: (tile(cur(t), i)[0], 0, 0))]
    else:
        x_arrs, x_specs = [xs[0].reshape(n_rows, D)], [rows_spec(D, cur)]
    in_specs = x_specs + [
        mod_spec(0), mod_spec(1),
        const((None, 1, D), layer, 0, 0),
        const((None, D, 2 * D_FF), layer, 0, 0),
        const((None, D_FF, D), layer, 0, 0),
        const((None, 1, D), layer, 0, 0),
    ]
    out_specs = [rows_spec(D, cur)]
    out_shape = [jax.ShapeDtypeStruct((n_rows, D), jnp.float32)]
    if even:
        in_specs += [
            const((None, D, D_FOURIER + 2 * D_SGU), j, 0, 0),
            const((GROUP_CH, 2 * GROUP_CH), 0, 0),
            const((None, 1, D_SGU), j, 0, 0),
            const((None, N_GROUPS, CHUNK, CHUNK), j, 0, 0, 0),
            const((None, CHUNK, D_SGU), j, 0, 0),
        ]
        out_specs += [rows_spec(2 * D_FOURIER, prev), rows_spec(D_SGU, prev)]
        out_shape += [jax.ShapeDtypeStruct((n_rows, 2 * D_FOURIER), jnp.bfloat16),
                      jax.ShapeDtypeStruct((n_rows, D_SGU), jnp.bfloat16)]
        stage_arrs = list(stage_args)
    else:
        wqkv, gq, gk, cos_t, sin_t = stage_args

        def rope_spec(i):
            return pl.BlockSpec((TM, HEAD_DIM), lambda t: (tile(prev(t), i)[1], 0))

        in_specs += [
            const((None, D, D + 2 * KV_W), j, 0, 0),
            const((None, 1, HEAD_DIM), j, 0, 0),
            const((None, 1, HEAD_DIM), j, 0, 0),
            rope_spec(0), rope_spec(0), rope_spec(1), rope_spec(1),
        ]
        stage_arrs = [wqkv, gq, gk, cos_t, sin_t, cos_t, sin_t]
        out_specs += [
            rows_spec(D, prev),
            pl.BlockSpec((N_KV, HEAD_DIM, PAIR), lambda t: (0, 0, prev(t))),
            rows_spec(KV_W, prev),
        ]
        out_shape += [jax.ShapeDtypeStruct((n_rows, D), jnp.bfloat16),
                      jax.ShapeDtypeStruct((N_KV, HEAD_DIM, n_rows), jnp.bfloat16),
                      jax.ShapeDtypeStruct((n_rows, KV_W), jnp.bfloat16)]
    outs = pl.pallas_call(
        functools.partial(_layer_in_kernel, split_in=split_in, even=even, steps=steps),
        grid=(steps,),
        in_specs=in_specs,
        out_specs=out_specs,
        out_shape=out_shape,
        scratch_shapes=[pltpu.VMEM((PAIR, D), jnp.bfloat16)],
        compiler_params=_cparams(
            1, resident=_FFN_WEIGHT_BYTES + D * _STAGE_COLS * BF16,
            streamed=PAIR * (2 * D * F32 + (_STAGE_COLS + (0 if even else KV_W)) * BF16),
            scratch=PAIR * D * BF16,
            temporaries=_FFN_TEMP_BYTES + 3 * PAIR * _STAGE_COLS * F32),
        name="layer_in_even" if even else "layer_in_odd",
    )(*x_arrs, mods, mods, g1, wgu, wd, gm, *stage_arrs)
    return [o if o.shape[0] == N_KV and o.ndim == 3 else o.reshape(batch, ROWS, o.shape[-1])
            for o in outs]


def _layer_out_kernel(x_ref, *refs, n_mod, n_y, split_y, final):
    mod_refs = refs[:n_mod] * (2 // n_mod)
    wo_ref = refs[n_mod]
    y_refs = refs[n_mod + 1:n_mod + 1 + n_y]
    g2_ref, wgu_ref, wd_ref = refs[n_mod + 1 + n_y:n_mod + 4 + n_y]
    o_ref = refs[-1]
    if split_y:
        tile0 = 2 * pl.program_id(0)
        y = jnp.concatenate(
            [jnp.where((tile0 + i) % N_TILES == N_LAT_TILES, y_refs[2 * i + 1][...],
                       y_refs[2 * i][...]) for i in range(2)], axis=0)
    else:
        y = y_refs[0][...] if n_y == 1 else jnp.concatenate([r[...] for r in y_refs], axis=1)
    proj = jnp.dot(y, wo_ref[...], preferred_element_type=jnp.float32)
    x = x_ref[...] + _per_tile(lambda p, m: m[5:6, :] * p, proj, mod_refs)
    x = _ffn_half_step(x, mod_refs, 6, g2_ref[...], wgu_ref, wd_ref)
    if final:
        gf_ref = refs[-2]
        ms = jnp.mean(x * x, axis=-1, keepdims=True)
        x = x * lax.rsqrt(ms + EPS) * gf_ref[...]
    o_ref[...] = x


def _layer_out(xc, mods, wo, ys, g2, wgu, wd, layer, n_tiles, g_final=None, split_y=False):
    batch = xc.shape[0]
    j = layer // 2
    final = g_final is not None

    def const(block_shape, *index):
        return _resident(block_shape, lambda *_: index)

    if n_tiles == N_LAT_TILES:
        grid = (batch, N_LAT_TILES // 2)
        x_arr, x_spec = xc, pl.BlockSpec((None, PAIR, D), lambda b, k: (b, k, 0))
        mod_specs = [pl.BlockSpec((None, N_MOD, D), lambda b, k: (b, 0, 0))]
        y_arrs = list(ys)
        y_specs = [pl.BlockSpec((None, PAIR, y.shape[-1]), lambda b, k: (b, k, 0)) for y in ys]
        out_spec = pl.BlockSpec((None, PAIR, D), lambda b, k: (b, k, 0))
        out_shape = jax.ShapeDtypeStruct((batch, SEQ, D), jnp.float32)
    else:
        assert (batch * N_TILES) % 2 == 0
        grid = (batch * N_TILES // 2,)
        x_arr, x_spec = xc.reshape(batch * ROWS, D), pl.BlockSpec((PAIR, D), lambda k: (k, 0))

        def tile(k, i):
            t = 2 * k + i
            return t // N_TILES, t % N_TILES

        def mod_spec(i):
            def index(k):
                b, r = tile(k, i)
                return jnp.where(r == N_LAT_TILES, batch, b), 0, 0
            return pl.BlockSpec((None, N_MOD, D), index)

        mod_specs = [mod_spec(0), mod_spec(1)]
        if split_y:
            y_arrs, y_specs = [], []
            for i in range(2):
                def lat(k, i=i):
                    b, r = tile(k, i)
                    return b, jnp.minimum(r, N_LAT_TILES - 1), 0
                y_arrs += list(ys)
                y_specs += [pl.BlockSpec((None, TM, D), lat),
                            pl.BlockSpec((None, CTX, D), lambda k, i=i: (tile(k, i)[0], 0, 0))]
        else:
            y_arrs = [y.reshape(batch * ROWS, y.shape[-1]) for y in ys]
            y_specs = [pl.BlockSpec((PAIR, y.shape[-1]), lambda k: (k, 0)) for y in y_arrs]
        out_spec = pl.BlockSpec((PAIR, D), lambda k: (k, 0))
        out_shape = jax.ShapeDtypeStruct((batch * ROWS, D), jnp.float32)
    in_specs = [x_spec] + mod_specs + [const((None, D, D), j, 0, 0)] + y_specs + [
        const((None, 1, D), layer, 0, 0),
        const((None, D, 2 * D_FF), layer, 0, 0),
        const((None, D_FF, D), layer, 0, 0),
    ]
    args = [x_arr] + [mods] * len(mod_specs) + [wo] + y_arrs + [g2, wgu, wd]
    if final:
        in_specs.append(const((1, D), 0, 0))
        args.append(g_final)
    out = pl.pallas_call(
        functools.partial(_layer_out_kernel, n_mod=len(mod_specs), n_y=len(y_arrs),
                          split_y=split_y, final=final),
        grid=grid,
        in_specs=in_specs,
        out_specs=out_spec,
        out_shape=out_shape,
        compiler_params=_cparams(
            len(grid), resident=_FFN_WEIGHT_BYTES + D * D * BF16,
            streamed=PAIR * (2 * D * F32 + D * BF16),
            temporaries=_FFN_TEMP_BYTES + 2 * PAIR * D * F32),
        name="layer_out_final" if final else "layer_out",
    )(*args)
    return out.reshape(batch, -1, D)


HALF = SEQ // 2
FT = MXU_DIM
FW = MXU_DIM
N_FOLD_TILES = HALF // FT
DFT_SCALE = SEQ ** -0.5


def _dft_kernel(ch_ref, sh_ref, rev_ref, cc_ref, sc_ref, p_ref, q_ref, o_ref):
    rev = rev_ref[...]
    row0 = lax.broadcasted_iota(jnp.int32, (FT, FW), 0) == 0
    alt = (1 - 2 * (lax.broadcasted_iota(jnp.int32, (FT, FW), 0) & 1)).astype(jnp.float32)

    def reversed_upper(x_ref, t):
        src = HALF + FT * (N_FOLD_TILES - 1 - t)
        r = jnp.dot(rev, x_ref[src:src + FT, :], preferred_element_type=jnp.float32)
        if t == 0:
            return r
        first = HALF + FT * (N_FOLD_TILES - t)
        return jnp.where(row0, x_ref[first:first + 1, :].astype(jnp.float32), r)

    pfs, qfs, alt_sum = [], [], None
    for t in range(N_FOLD_TILES):
        rows = slice(t * FT, (t + 1) * FT)
        pf = p_ref[rows, :].astype(jnp.float32) + reversed_upper(p_ref, t)
        qf = q_ref[rows, :].astype(jnp.float32) - reversed_upper(q_ref, t)
        alt_sum = pf if alt_sum is None else alt_sum + pf
        pfs.append(pf.astype(jnp.bfloat16))
        qfs.append(qf.astype(jnp.bfloat16))
    p_mid = p_ref[HALF:HALF + 1, :].astype(jnp.float32)
    sign = jnp.concatenate([alt] * N_FOLD_TILES, axis=0)
    e = jnp.dot(ch_ref[...], jnp.concatenate(pfs, axis=0), preferred_element_type=jnp.float32)
    e = e + sign * (DFT_SCALE * p_mid)
    o = jnp.dot(sh_ref[...], jnp.concatenate(qfs, axis=0), preferred_element_type=jnp.float32)
    o_ref[0:HALF, :] = (e + o).astype(jnp.bfloat16)
    g = (e - o).astype(jnp.bfloat16)
    mid = DFT_SCALE * (jnp.sum(alt_sum * alt, axis=0, keepdims=True) + p_mid)
    for t in range(N_FOLD_TILES):
        src = FT * (N_FOLD_TILES - 1 - t)
        up = jnp.dot(rev, g[src:src + FT, :], preferred_element_type=jnp.float32)
        first = mid if t == 0 else g[src + FT:src + FT + 1, :].astype(jnp.float32)
        o_ref[HALF + t * FT:HALF + (t + 1) * FT, :] = jnp.where(row0, first, up).astype(jnp.bfloat16)
    ctx = jnp.dot(cc_ref[...], p_ref[SEQ:, :], preferred_element_type=jnp.float32)
    ctx = ctx + jnp.dot(sc_ref[...], q_ref[SEQ:, :], preferred_element_type=jnp.float32)
    o_ref[SEQ:, :] = ctx.astype(jnp.bfloat16)


def _dft(tables, pq):
    ch, sh, rev, cc, sc = tables
    batch = pq.shape[0]
    n_col = D_FOURIER // FW
    return pl.pallas_call(
        _dft_kernel,
        grid=(batch, n_col),
        in_specs=[
            _resident((HALF, HALF), lambda b, c: (0, 0)),
            _resident((HALF, HALF), lambda b, c: (0, 0)),
            _resident((FT, FT), lambda b, c: (0, 0)),
            _resident((CTX, CTX), lambda b, c: (0, 0)),
            _resident((CTX, CTX), lambda b, c: (0, 0)),
            pl.BlockSpec((None, ROWS, FW), lambda b, c: (b, 0, c)),
            pl.BlockSpec((None, ROWS, FW), lambda b, c: (b, 0, n_col + c)),
        ],
        out_specs=pl.BlockSpec((None, ROWS, FW), lambda b, c: (b, 0, c)),
        out_shape=jax.ShapeDtypeStruct((batch, ROWS, D_FOURIER), jnp.bfloat16),
        compiler_params=_cparams(
            2, resident=2 * HALF * HALF * BF16, streamed=3 * ROWS * FW * BF16,
            temporaries=HALF * FW * (2 * BF16 + 3 * F32) + 2 * ROWS * FW * F32),
        name="dft",
    )(ch, sh, rev, cc, sc, pq, pq)


KEY_TILE = MXU_DIM
N_KEY_TILES = ROWS // KEY_TILE


def _with_row_sums(v):
    return jnp.concatenate([v, jnp.ones_like(v)], axis=1)


def _normalised(pv):
    return (pv[:, :HEAD_DIM] / pv[:, HEAD_DIM:]).astype(jnp.bfloat16)


def _attn_kernel(q_ref, kt_ref, v_ref, o_ref, s_ref, m_ref):
    @pl.when(pl.program_id(0) == 0)
    def _():
        s_ref[...] = jnp.zeros(s_ref.shape, s_ref.dtype)
        m_ref[...] = jnp.zeros(m_ref.shape, m_ref.dtype)

    outs = []
    for hd in range(GROUP):
        q = q_ref[:, hd * HEAD_DIM:(hd + 1) * HEAD_DIM]
        m_prev = m_ref[hd]
        m_lane = None
        acc = None
        for j in range(N_KEY_TILES):
            keys = slice(j * KEY_TILE, (j + 1) * KEY_TILE)
            s_old = s_ref[hd, :, keys]
            p = jnp.exp2(s_old - jnp.concatenate([m_prev] * (KEY_TILE // HEAD_DIM), axis=1))
            pv = jnp.dot(p.astype(jnp.bfloat16), _with_row_sums(v_ref[keys, :]),
                         preferred_element_type=jnp.float32)
            acc = pv if acc is None else acc + pv
            s_new = jnp.dot(q, kt_ref[:, keys], preferred_element_type=jnp.float32)
            s_ref[hd, :, keys] = s_new
            for c in range(KEY_TILE // HEAD_DIM):
                part = s_new[:, c * HEAD_DIM:(c + 1) * HEAD_DIM]
                m_lane = part if m_lane is None else jnp.maximum(m_lane, part)
        outs.append(_normalised(acc))
        m_ref[hd] = jnp.broadcast_to(jnp.max(m_lane, axis=-1, keepdims=True), m_lane.shape)
    o_ref[...] = jnp.concatenate(outs, axis=1)


def _attention_latent(q, kt, v):
    batch = q.shape[0]
    gw = GROUP * HEAD_DIM
    per_sample = N_KV * N_LAT_TILES
    steps = batch * per_sample + 1

    def unflatten(c):
        return c // per_sample, (c % per_sample) // N_LAT_TILES, c % N_LAT_TILES

    def cur(t):
        return unflatten(jnp.minimum(t, steps - 2))

    def prev(t):
        return unflatten(jnp.maximum(t - 1, 0))

    def q_map(t):
        b, g, i = cur(t)
        return b, i, g

    def kt_map(t):
        b, g, _ = cur(t)
        return g, 0, b

    def v_map(t):
        b, g, _ = prev(t)
        return b, 0, g

    def o_map(t):
        b, g, i = prev(t)
        return b, i, g

    return pl.pallas_call(
        _attn_kernel,
        grid=(steps,),
        in_specs=[
            pl.BlockSpec((None, TM, gw), q_map),
            pl.BlockSpec((None, HEAD_DIM, ROWS), kt_map),
            pl.BlockSpec((None, ROWS, HEAD_DIM), v_map),
        ],
        out_specs=pl.BlockSpec((None, TM, gw), o_map),
        out_shape=jax.ShapeDtypeStruct((batch, SEQ, D), jnp.bfloat16),
        scratch_shapes=[pltpu.VMEM((GROUP, TM, ROWS), jnp.float32),
                        pltpu.VMEM((GROUP, TM, HEAD_DIM), jnp.float32)],
        compiler_params=_cparams(
            1, streamed=2 * (TM * gw + ROWS * HEAD_DIM) * BF16,
            scratch=GROUP * TM * (ROWS + HEAD_DIM) * F32,
            temporaries=GROUP * TM * 2 * HEAD_DIM * F32 + 4 * TM * KEY_TILE * F32),
        name="attention",
    )(q, kt, v)


def _attn_ctx_kernel(q_ref, kt_ref, v_ref, o_ref):
    v1 = _with_row_sums(v_ref[...])
    kt = kt_ref[...]
    outs = []
    for hd in range(GROUP):
        s = jnp.dot(q_ref[:, hd * HEAD_DIM:(hd + 1) * HEAD_DIM], kt,
                    preferred_element_type=jnp.float32)
        p = jnp.exp2(s - jnp.max(s, axis=-1, keepdims=True)).astype(jnp.bfloat16)
        outs.append(_normalised(jnp.dot(p, v1, preferred_element_type=jnp.float32)))
    o_ref[...] = jnp.concatenate(outs, axis=1)


def _attention_context(q, kt, v):
    batch = q.shape[0]
    gw = GROUP * HEAD_DIM
    return pl.pallas_call(
        _attn_ctx_kernel,
        grid=(batch, N_KV),
        in_specs=[
            pl.BlockSpec((None, CTX, gw), lambda b, g: (b, SEQ // CTX, g)),
            pl.BlockSpec((None, HEAD_DIM, CTX), lambda b, g: (g, 0, b * N_TILES + N_LAT_TILES)),
            pl.BlockSpec((None, CTX, HEAD_DIM), lambda b, g: (b, SEQ // CTX, g)),
        ],
        out_specs=pl.BlockSpec((None, CTX, gw), lambda b, g: (b, 0, g)),
        out_shape=jax.ShapeDtypeStruct((batch, CTX, D), jnp.bfloat16),
        compiler_params=_cparams(
            2, streamed=2 * CTX * (gw + HEAD_DIM) * BF16,
            temporaries=GROUP * CTX * (2 * CTX + 2 * HEAD_DIM) * F32),
        name="attention_ctx",
    )(q, kt, v)


def _cos_sin_products(n, size):
    r = 1 << (int(math.log2(size)) // 2)
    hi = size // r
    assert r % SUBLANES == 0
    k = jnp.arange(size, dtype=jnp.int32)[None, :]
    ang_hi = ((jnp.arange(hi, dtype=jnp.int32)[:, None] * r * k) % n).astype(jnp.float32)
    ang_lo = ((jnp.arange(r, dtype=jnp.int32)[:, None] * k) % n).astype(jnp.float32)
    w = 2.0 * math.pi / n
    ca, sa = jnp.cos(ang_hi * w)[:, None, :], jnp.sin(ang_hi * w)[:, None, :]
    cb, sb = jnp.cos(ang_lo * w)[None, :, :], jnp.sin(ang_lo * w)[None, :, :]
    scale = n ** -0.5
    cos = ((ca * cb - sa * sb) * scale).reshape(size, size)
    sin = ((sa * cb + ca * sb) * scale).reshape(size, size)
    return cos, sin


def _dft_tables():
    ch, sh = _cos_sin_products(SEQ, HALF)
    cc, sc = _cos_sin_products(CTX, CTX)
    r = jnp.arange(FT, dtype=jnp.int32)
    rev = ((r[:, None] + r[None, :]) == FT).astype(jnp.bfloat16)
    return tuple(t.astype(jnp.bfloat16) for t in (ch, sh, rev, cc, sc))


def _channel_dft():
    c, s = _cos_sin_products(GROUP_CH, GROUP_CH)
    return jnp.concatenate([c, -s], axis=1).astype(jnp.bfloat16)


def _rope_tables():
    half = HEAD_DIM // 2
    rows = SEQ // GRID_W
    row = jnp.repeat(jnp.arange(rows, dtype=jnp.float32), GRID_W)
    col = jnp.tile(jnp.arange(GRID_W, dtype=jnp.float32), rows)
    inv_freq = ROPE_THETA ** (-jnp.arange(0, half, 2, dtype=jnp.float32) / half)
    ang_row = row[:, None] * inv_freq
    ang_col = col[:, None] * inv_freq
    cos = jnp.concatenate([jnp.cos(ang_row)] * 2 + [jnp.cos(ang_col)] * 2, axis=1)
    sin = jnp.concatenate([-jnp.sin(ang_row), jnp.sin(ang_row),
                           -jnp.sin(ang_col), jnp.sin(ang_col)], axis=1)
    cos = jnp.concatenate([cos, jnp.ones((CTX, HEAD_DIM), jnp.float32)], axis=0)
    sin = jnp.concatenate([sin, jnp.zeros((CTX, HEAD_DIM), jnp.float32)], axis=0)
    order = _rope_order()
    return cos[:, order], sin[:, order]


def _rope_order():
    quarter = HEAD_DIM // 4
    blocks = [0, 2, 1, 3]
    return jnp.concatenate([jnp.arange(b * quarter, (b + 1) * quarter) for b in blocks])


def _permute_qk_heads(w_qkv, g_q, g_k):
    order = _rope_order()
    n_cols = w_qkv.shape[-1]
    n_qk = (N_HEADS + N_KV) * HEAD_DIM
    heads = jnp.arange(n_qk, dtype=jnp.int32) // HEAD_DIM * HEAD_DIM
    src = jnp.concatenate([heads + jnp.tile(order, N_HEADS + N_KV),
                           jnp.arange(n_qk, n_cols, dtype=jnp.int32)])
    select = (jnp.arange(n_cols, dtype=jnp.int32)[:, None] == src[None, :]).astype(jnp.bfloat16)
    w = jnp.einsum("lik,kj->lij", w_qkv.astype(jnp.bfloat16), select,
                   preferred_element_type=jnp.float32).astype(jnp.bfloat16)
    return w, g_q[:, order], g_k[:, order]


def kernel(x, c, ctx, c_ctx, w_mod, b_mod, g_ffn1, w_ffn1_gu, w_ffn1_down, g_mix, g_ffn2, w_ffn2_gu, w_ffn2_down, g_final, w_in_ab, g_v, w_s, b_s, w_out_ab, w_qkv, g_q, g_k, w_o):
    batch = x.shape[0]
    depth = w_mod.shape[0]
    assert x.shape == (batch, SEQ, D) and ctx.shape == (batch, CTX, D)
    assert depth % 2 == 0

    mod_rows = -(-(batch + 1) // SUBLANES) * SUBLANES
    cc = jnp.concatenate([c, c_ctx[None], jnp.zeros((mod_rows - batch - 1, D), jnp.float32)], axis=0)
    m = _modulation(cc, w_mod, b_mod)
    m = m[:, :batch + 1].reshape(depth, batch + 1, N_MOD, D)

    wgu1, wd1 = w_ffn1_gu.astype(jnp.bfloat16), w_ffn1_down.astype(jnp.bfloat16)
    wgu2, wd2 = w_ffn2_gu.astype(jnp.bfloat16), w_ffn2_down.astype(jnp.bfloat16)
    g1 = g_ffn1.reshape(depth, 1, D)
    g2 = g_ffn2.reshape(depth, 1, D)
    gm = g_mix.reshape(depth, 1, D)
    w_in = w_in_ab.astype(jnp.bfloat16)
    w_out = w_out_ab.astype(jnp.bfloat16)
    ws = w_s.astype(jnp.bfloat16)
    gv = g_v.reshape(-1, 1, D_SGU)
    bs_full = jnp.repeat(jnp.swapaxes(b_s, 1, 2), GROUP_CH, axis=2)
    wqkv, g_q_p, g_k_p = _permute_qk_heads(w_qkv, g_q, g_k)
    wo = w_o.astype(jnp.bfloat16)
    gq = g_q_p.reshape(-1, 1, HEAD_DIM)
    gk = g_k_p.reshape(-1, 1, HEAD_DIM)
    dft_tables = _dft_tables()
    ccs = _channel_dft()
    cos_t, sin_t = _rope_tables()

    xs = (x, ctx)
    for l in range(depth):
        last = l == depth - 1
        even = l % 2 == 0
        mods = m[l]
        n_out = N_LAT_TILES if last else N_TILES
        if even:
            xc, pq, sgu = _layer_in(xs, mods, g1, wgu1, wd1, gm, l, True,
                                    (w_in, ccs, gv, ws, bs_full))
            ys, w_proj = [_dft(dft_tables, pq), sgu], w_out
        else:
            xc, q, kt, v = _layer_in(xs, mods, g1, wgu1, wd1, gm, l, False,
                                     (wqkv, gq, gk, cos_t, sin_t))
            ys, w_proj = [_attention_latent(q, kt, v)], wo
            if not last:
                ys.append(_attention_context(q, kt, v))
        xc = _layer_out(xc, mods, w_proj, ys, g2, wgu2, wd2, l, n_out,
                        g_final.reshape(1, D) if last else None,
                        split_y=not even and not last)
        xs = (xc,)
    return xc
```

```python
import functools
import math

import jax
import jax.numpy as jnp
from jax import lax
from jax.experimental import pallas as pl
from jax.experimental.pallas import tpu as pltpu

D = 1024
SEQ = 4096
CTX = 256
ROWS = SEQ + CTX
GRID_W = 64
D_FF = 2752
N_MOD = 9
EPS = 1e-6
HEAD_DIM = 128
N_HEADS = 8
N_KV = 2
GROUP = N_HEADS // N_KV
KV_W = N_KV * HEAD_DIM
ROPE_THETA = 10000.0
D_FOURIER = 512
D_SGU = 512
GROUP_CH = 128
N_GROUPS = 4
CHUNK = 128

MXU_DIM = 256
TM = 256
N_LAT_TILES = SEQ // TM
N_TILES = ROWS // TM
V7X_VMEM_BYTES = 64 * 1024 * 1024
SUBLANES = 8
F32, BF16 = 4, 2

assert TM == CTX and SEQ % TM == 0


def _cparams(n_axes, resident=0, streamed=0, scratch=0, temporaries=0):
    need = resident + 2 * streamed + scratch + temporaries
    assert need <= V7X_VMEM_BYTES, need
    return pltpu.CompilerParams(dimension_semantics=("arbitrary",) * n_axes,
                                vmem_limit_bytes=need)


def _resident(block_shape, index_map):
    return pl.BlockSpec(block_shape, index_map, pipeline_mode=pl.Buffered(1))


def _modnorm(x, g, shift, scale):
    ms = jnp.mean(x * x, axis=-1, keepdims=True)
    return x * lax.rsqrt(ms + EPS) * (g * (1.0 + scale)) + shift


def _ffn_norm(x, mod_ref, row0, g):
    h = _modnorm(x, g, mod_ref[row0:row0 + 1, :], mod_ref[row0 + 1:row0 + 2, :])
    return h.astype(jnp.bfloat16)


def _per_tile(fn, x, mod_refs):
    return jnp.concatenate([fn(x[i * TM:(i + 1) * TM], m) for i, m in enumerate(mod_refs)],
                           axis=0)


def _swiglu(h, wgu_ref, wd_ref):
    gu = jnp.dot(h, wgu_ref[...], preferred_element_type=jnp.float32)
    gt, up = gu[:, :D_FF], gu[:, D_FF:]
    a = (gt * jax.nn.sigmoid(gt) * up).astype(jnp.bfloat16)
    return jnp.dot(a, wd_ref[...], preferred_element_type=jnp.float32)


def _ffn_half_step(x, mod_refs, row0, g, wgu_ref, wd_ref, h=None):
    if h is None:
        h = _per_tile(lambda t, m: _ffn_norm(t, m, row0, g), x, mod_refs)
    y = _swiglu(h, wgu_ref, wd_ref)
    return x + _per_tile(lambda t, m: 0.5 * m[row0 + 2:row0 + 3, :] * t, y, mod_refs)


def _mod_kernel(cc_ref, w_ref, b_ref, o_ref):
    s = cc_ref[...]
    s = s * jax.nn.sigmoid(s)
    s_hi = s.astype(jnp.bfloat16)
    s_lo = (s - s_hi.astype(jnp.float32)).astype(jnp.bfloat16)
    w = w_ref[...]
    w_hi = w.astype(jnp.bfloat16)
    w_lo = (w - w_hi.astype(jnp.float32)).astype(jnp.bfloat16)
    rows = s.shape[0]
    both = jnp.dot(jnp.concatenate([s_hi, s_lo], axis=0), w_hi,
                   preferred_element_type=jnp.float32)
    cross = jnp.dot(s_hi, w_lo, preferred_element_type=jnp.float32)
    o_ref[...] = both[:rows] + both[rows:] + cross + b_ref[...]


def _modulation(cc, w_mod, b_mod):
    depth = w_mod.shape[0]
    rows = cc.shape[0]
    tn = D
    return pl.pallas_call(
        _mod_kernel,
        grid=(depth, N_MOD * D // tn),
        in_specs=[
            pl.BlockSpec((rows, D), lambda l, j: (0, 0)),
            pl.BlockSpec((None, D, tn), lambda l, j: (l, 0, j)),
            pl.BlockSpec((None, 1, tn), lambda l, j: (l, 0, j)),
        ],
        out_specs=pl.BlockSpec((None, rows, tn), lambda l, j: (l, 0, j)),
        out_shape=jax.ShapeDtypeStruct((depth, rows, N_MOD * D), jnp.float32),
        compiler_params=_cparams(2, streamed=(D + 2 * rows) * tn * F32,
                                 temporaries=2 * D * tn * BF16 + 4 * rows * tn * F32),
        name="modulation",
    )(cc, w_mod, b_mod.reshape(depth, 1, N_MOD * D))


def _even_in_stage(p, ccs_ref, gv_ref, ws_ref, bs_ref, pq_ref, sgu_ref):
    a = p[:, :D_FOURIER].astype(jnp.bfloat16)
    uv = jax.nn.gelu(p[:, D_FOURIER:], approximate=True)
    u = uv[:, :D_SGU]
    v = uv[:, D_SGU:]
    ccs = ccs_ref[...]
    ps, qs, gated = [], [], []
    n_chunks = p.shape[0] // CHUNK
    for grp in range(N_GROUPS):
        lo, hi = grp * GROUP_CH, (grp + 1) * GROUP_CH
        t = jnp.dot(a[:, lo:hi], ccs, preferred_element_type=jnp.float32)
        ps.append(t[:, :GROUP_CH])
        qs.append(t[:, GROUP_CH:])
        vg = v[:, lo:hi]
        ms = jnp.mean(vg * vg, axis=-1, keepdims=True)
        vh = (vg * lax.rsqrt(ms + EPS) * gv_ref[:, lo:hi]).astype(jnp.bfloat16)
        rhs = jnp.concatenate([vh[c * CHUNK:(c + 1) * CHUNK, :] for c in range(n_chunks)], axis=1)
        mixed = jnp.dot(ws_ref[grp], rhs, preferred_element_type=jnp.float32)
        mixed = jnp.concatenate(
            [mixed[:, c * GROUP_CH:(c + 1) * GROUP_CH] for c in range(n_chunks)], axis=0)
        bias = jnp.concatenate([bs_ref[:, lo:hi]] * n_chunks, axis=0)
        gated.append(u[:, lo:hi] * (mixed + bias))
    pq_ref[...] = jnp.concatenate(ps + qs, axis=1).astype(jnp.bfloat16)
    sgu_ref[...] = jnp.concatenate(gated, axis=1).astype(jnp.bfloat16)


def _rope(t, cos, sin_signed):
    return t * cos + pltpu.roll(t, HEAD_DIM // 2, 1) * sin_signed


def _qkv_stage(qkv, gq_ref, gk_ref, cos, sin, q_ref, kt_ref, v_ref):
    q_scale = (HEAD_DIM ** -0.5) * math.log2(math.e)

    def head(col, gain):
        t = qkv[:, col:col + HEAD_DIM]
        ms = jnp.mean(t * t, axis=-1, keepdims=True)
        return _rope(t * lax.rsqrt(ms + EPS) * gain, cos, sin)

    qs = [head(hd * HEAD_DIM, gq_ref[...]) * q_scale for hd in range(N_HEADS)]
    q_ref[...] = jnp.concatenate(qs, axis=1).astype(jnp.bfloat16)
    for hd in range(N_KV):
        kt_ref[hd] = head((N_HEADS + hd) * HEAD_DIM, gk_ref[...]).T.astype(jnp.bfloat16)
    v_ref[...] = qkv[:, (N_HEADS + N_KV) * HEAD_DIM:].astype(jnp.bfloat16)


PAIR = 2 * TM
_STAGE_COLS = D_FOURIER + 2 * D_SGU
assert _STAGE_COLS == D + 2 * KV_W
_FFN_WEIGHT_BYTES = 3 * D * D_FF * BF16
_FFN_TEMP_BYTES = PAIR * (2 * D_FF * F32 + D_FF * BF16 + D * BF16 + 2 * D * F32)


def _layer_in_kernel(*refs, split_in, even, steps):
    refs = list(refs)
    h_ref = refs.pop()
    t = pl.program_id(0)

    @pl.when(t == 0)
    def _():
        h_ref[...] = jnp.zeros(h_ref.shape, h_ref.dtype)

    if split_in:
        tile0 = 2 * jnp.minimum(t, steps - 2)
        x = jnp.concatenate(
            [jnp.where((tile0 + i) % N_TILES == N_LAT_TILES, refs[2 * i + 1][...], refs[2 * i][...])
             for i in range(2)], axis=0)
        del refs[:4]
    else:
        x = refs.pop(0)[...]
    mod_refs = refs[:2]
    g1_ref, wgu_ref, wd_ref, gm_ref, w_stage_ref = refs[2:7]
    n_out = 3 if even else 4
    stage_in = refs[7:-n_out]
    xo_ref, *stage_out = refs[-n_out:]
    proj = jnp.dot(h_ref[...], w_stage_ref[...], preferred_element_type=jnp.float32)
    if even:
        _even_in_stage(proj, *stage_in, *stage_out)
    else:
        gq_ref, gk_ref, cos0, sin0, cos1, sin1 = stage_in
        _qkv_stage(proj, gq_ref, gk_ref, jnp.concatenate([cos0[...], cos1[...]], axis=0),
                   jnp.concatenate([sin0[...], sin1[...]], axis=0), *stage_out)
    x = _ffn_half_step(x, mod_refs, 0, g1_ref[...], wgu_ref, wd_ref)
    xo_ref[...] = x
    gm = gm_ref[...]
    h_ref[...] = _per_tile(lambda r, m: _ffn_norm(r, m, 3, gm), x, mod_refs)


def _layer_in(xs, mods, g1, wgu, wd, gm, layer, even, stage_args):
    batch = xs[0].shape[0]
    j = layer // 2
    split_in = len(xs) == 2
    n_rows = batch * ROWS
    assert (batch * N_TILES) % 2 == 0
    steps = batch * N_TILES // 2 + 1

    def cur(t):
        return jnp.minimum(t, steps - 2)

    def prev(t):
        return jnp.maximum(t - 1, 0)

    def tile(pair, i):
        n = 2 * pair + i
        return n // N_TILES, n % N_TILES

    def rows_spec(width, which):
        return pl.BlockSpec((PAIR, width), lambda t: (which(t), 0))

    def const(block_shape, *index):
        return _resident(block_shape, lambda t: index)

    def mod_spec(i):
        def index(t):
            b, r = tile(cur(t), i)
            return jnp.where(r == N_LAT_TILES, batch, b), 0, 0
        return pl.BlockSpec((None, N_MOD, D), index)

    if split_in:
        x_arrs, x_specs = [], []
        for i in range(2):
            def lat(t, i=i):
                b, r = tile(cur(t), i)
                return b, jnp.minimum(r, N_LAT_TILES - 1), 0
            x_arrs += list(xs)
            x_specs += [pl.BlockSpec((None, TM, D), lat),
                        pl.BlockSpec((None, CTX, D), lambda t, i=i: (tile(cur(t), i)[0], 0, 0))]
    else:
        x_arrs, x_specs = [xs[0].reshape(n_rows, D)], [rows_spec(D, cur)]
    in_specs = x_specs + [
        mod_spec(0), mod_spec(1),
        const((None, 1, D), layer, 0, 0),
        const((None, D, 2 * D_FF), layer, 0, 0),
        const((None, D_FF, D), layer, 0, 0),
        const((None, 1, D), layer, 0, 0),
    ]
    out_specs = [rows_spec(D, cur)]
    out_shape = [jax.ShapeDtypeStruct((n_rows, D), jnp.float32)]
    if even:
        in_specs += [
            const((None, D, D_FOURIER + 2 * D_SGU), j, 0, 0),
            const((GROUP_CH, 2 * GROUP_CH), 0, 0),
            const((None, 1, D_SGU), j, 0, 0),
            const((None, N_GROUPS, CHUNK, CHUNK), j, 0, 0, 0),
            const((None, CHUNK, D_SGU), j, 0, 0),
        ]
        out_specs += [rows_spec(2 * D_FOURIER, prev), rows_spec(D_SGU, prev)]
        out_shape += [jax.ShapeDtypeStruct((n_rows, 2 * D_FOURIER), jnp.bfloat16),
                      jax.ShapeDtypeStruct((n_rows, D_SGU), jnp.bfloat16)]
        stage_arrs = list(stage_args)
    else:
        wqkv, gq, gk, cos_t, sin_t = stage_args

        def rope_spec(i):
            return pl.BlockSpec((TM, HEAD_DIM), lambda t: (tile(prev(t), i)[1], 0))

        in_specs += [
            const((None, D, D + 2 * KV_W), j, 0, 0),
            const((None, 1, HEAD_DIM), j, 0, 0),
            const((None, 1, HEAD_DIM), j, 0, 0),
            rope_spec(0), rope_spec(0), rope_spec(1), rope_spec(1),
        ]
        stage_arrs = [wqkv, gq, gk, cos_t, sin_t, cos_t, sin_t]
        out_specs += [
            rows_spec(D, prev),
            pl.BlockSpec((N_KV, HEAD_DIM, PAIR), lambda t: (0, 0, prev(t))),
            rows_spec(KV_W, prev),
        ]
        out_shape += [jax.ShapeDtypeStruct((n_rows, D), jnp.bfloat16),
                      jax.ShapeDtypeStruct((N_KV, HEAD_DIM, n_rows), jnp.bfloat16),
                      jax.ShapeDtypeStruct((n_rows, KV_W), jnp.bfloat16)]
    outs = pl.pallas_call(
        functools.partial(_layer_in_kernel, split_in=split_in, even=even, steps=steps),
        grid=(steps,),
        in_specs=in_specs,
        out_specs=out_specs,
        out_shape=out_shape,
        scratch_shapes=[pltpu.VMEM((PAIR, D), jnp.bfloat16)],
        compiler_params=_cparams(
            1, resident=_FFN_WEIGHT_BYTES + D * _STAGE_COLS * BF16,
            streamed=PAIR * (2 * D * F32 + (_STAGE_COLS + (0 if even else KV_W)) * BF16),
            scratch=PAIR * D * BF16,
            temporaries=_FFN_TEMP_BYTES + 3 * PAIR * _STAGE_COLS * F32),
        name="layer_in_even" if even else "layer_in_odd",
    )(*x_arrs, mods, mods, g1, wgu, wd, gm, *stage_arrs)
    return [o if o.shape[0] == N_KV and o.ndim == 3 else o.reshape(batch, ROWS, o.shape[-1])
            for o in outs]


def _layer_out_kernel(x_ref, *refs, n_mod, n_y, split_y, final):
    mod_refs = refs[:n_mod] * (2 // n_mod)
    wo_ref = refs[n_mod]
    y_refs = refs[n_mod + 1:n_mod + 1 + n_y]
    g2_ref, wgu_ref, wd_ref = refs[n_mod + 1 + n_y:n_mod + 4 + n_y]
    o_ref = refs[-1]
    if split_y and n_y == 2:
        y = jnp.concatenate([r[...] for r in y_refs], axis=0)
    elif split_y:
        tile0 = 2 * pl.program_id(0)

        def pick(i):
            r = (tile0 + i) % N_TILES
            even, odd, ctx = (y_ref[...] for y_ref in y_refs[3 * i:3 * i + 3])
            return jnp.where(r == N_LAT_TILES, ctx, jnp.where(r % 2 == 1, odd, even))

        y = jnp.concatenate([pick(0), pick(1)], axis=0)
    else:
        y = jnp.concatenate([r[...] for r in y_refs], axis=1)
    proj = jnp.dot(y, wo_ref[...], preferred_element_type=jnp.float32)
    x = x_ref[...] + _per_tile(lambda p, m: m[5:6, :] * p, proj, mod_refs)
    x = _ffn_half_step(x, mod_refs, 6, g2_ref[...], wgu_ref, wd_ref)
    if final:
        gf_ref = refs[-2]
        ms = jnp.mean(x * x, axis=-1, keepdims=True)
        x = x * lax.rsqrt(ms + EPS) * gf_ref[...]
    o_ref[...] = x


def _layer_out(xc, mods, wo, ys, g2, wgu, wd, layer, n_tiles, g_final=None, split_y=False):
    batch = xc.shape[0]
    j = layer // 2
    final = g_final is not None

    def const(block_shape, *index):
        return _resident(block_shape, lambda *_: index)

    if n_tiles == N_LAT_TILES:
        grid = (batch, N_LAT_TILES // 2)
        x_arr, x_spec = xc, pl.BlockSpec((None, PAIR, D), lambda b, k: (b, k, 0))
        mod_specs = [pl.BlockSpec((None, N_MOD, D), lambda b, k: (b, 0, 0))]
        y_arrs = list(ys)
        y_rows = TM if split_y else PAIR
        y_specs = [pl.BlockSpec((None, y_rows, y.shape[-1]), lambda b, k: (b, k, 0)) for y in ys]
        out_spec = pl.BlockSpec((None, PAIR, D), lambda b, k: (b, k, 0))
        out_shape = jax.ShapeDtypeStruct((batch, SEQ, D), jnp.float32)
    else:
        assert (batch * N_TILES) % 2 == 0
        grid = (batch * N_TILES // 2,)
        x_arr, x_spec = xc.reshape(batch * ROWS, D), pl.BlockSpec((PAIR, D), lambda k: (k, 0))

        def tile(k, i):
            t = 2 * k + i
            return t // N_TILES, t % N_TILES

        def mod_spec(i):
            def index(k):
                b, r = tile(k, i)
                return jnp.where(r == N_LAT_TILES, batch, b), 0, 0
            return pl.BlockSpec((None, N_MOD, D), index)

        mod_specs = [mod_spec(0), mod_spec(1)]
        if split_y:
            y_arrs, y_specs = [], []
            for i in range(2):
                def lat(k, i=i):
                    b, r = tile(k, i)
                    return b, jnp.minimum(r, N_LAT_TILES - 1) // 2, 0
                y_arrs += list(ys)
                y_specs += [pl.BlockSpec((None, TM, D), lat), pl.BlockSpec((None, TM, D), lat),
                            pl.BlockSpec((None, CTX, D), lambda k, i=i: (tile(k, i)[0], 0, 0))]
        else:
            y_arrs = [y.reshape(batch * ROWS, y.shape[-1]) for y in ys]
            y_specs = [pl.BlockSpec((PAIR, y.shape[-1]), lambda k: (k, 0)) for y in y_arrs]
        out_spec = pl.BlockSpec((PAIR, D), lambda k: (k, 0))
        out_shape = jax.ShapeDtypeStruct((batch * ROWS, D), jnp.float32)
    in_specs = [x_spec] + mod_specs + [const((None, D, D), j, 0, 0)] + y_specs + [
        const((None, 1, D), layer, 0, 0),
        const((None, D, 2 * D_FF), layer, 0, 0),
        const((None, D_FF, D), layer, 0, 0),
    ]
    args = [x_arr] + [mods] * len(mod_specs) + [wo] + y_arrs + [g2, wgu, wd]
    if final:
        in_specs.append(const((1, D), 0, 0))
        args.append(g_final)
    out = pl.pallas_call(
        functools.partial(_layer_out_kernel, n_mod=len(mod_specs), n_y=len(y_arrs),
                          split_y=split_y, final=final),
        grid=grid,
        in_specs=in_specs,
        out_specs=out_spec,
        out_shape=out_shape,
        compiler_params=_cparams(
            len(grid), resident=_FFN_WEIGHT_BYTES + D * D * BF16,
            streamed=PAIR * (2 * D * F32 + D * BF16),
            temporaries=_FFN_TEMP_BYTES + 2 * PAIR * D * F32),
        name="layer_out_final" if final else "layer_out",
    )(*args)
    return out.reshape(batch, -1, D)


HALF = SEQ // 2
FT = MXU_DIM
FW = MXU_DIM
N_FOLD_TILES = HALF // FT
DFT_SCALE = SEQ ** -0.5


def _dft_kernel(ch_ref, sh_ref, rev_ref, cc_ref, sc_ref, p_ref, q_ref, o_ref):
    rev = rev_ref[...]
    row0 = lax.broadcasted_iota(jnp.int32, (FT, FW), 0) == 0
    alt = (1 - 2 * (lax.broadcasted_iota(jnp.int32, (FT, FW), 0) & 1)).astype(jnp.float32)

    def reversed_upper(x_ref, t):
        src = HALF + FT * (N_FOLD_TILES - 1 - t)
        r = jnp.dot(rev, x_ref[src:src + FT, :], preferred_element_type=jnp.float32)
        if t == 0:
            return r
        first = HALF + FT * (N_FOLD_TILES - t)
        return jnp.where(row0, x_ref[first:first + 1, :].astype(jnp.float32), r)

    pfs, qfs, alt_sum = [], [], None
    for t in range(N_FOLD_TILES):
        rows = slice(t * FT, (t + 1) * FT)
        pf = p_ref[rows, :].astype(jnp.float32) + reversed_upper(p_ref, t)
        qf = q_ref[rows, :].astype(jnp.float32) - reversed_upper(q_ref, t)
        alt_sum = pf if alt_sum is None else alt_sum + pf
        pfs.append(pf.astype(jnp.bfloat16))
        qfs.append(qf.astype(jnp.bfloat16))
    p_mid = p_ref[HALF:HALF + 1, :].astype(jnp.float32)
    sign = jnp.concatenate([alt] * N_FOLD_TILES, axis=0)
    e = jnp.dot(ch_ref[...], jnp.concatenate(pfs, axis=0), preferred_element_type=jnp.float32)
    e = e + sign * (DFT_SCALE * p_mid)
    o = jnp.dot(sh_ref[...], jnp.concatenate(qfs, axis=0), preferred_element_type=jnp.float32)
    o_ref[0:HALF, :] = (e + o).astype(jnp.bfloat16)
    g = (e - o).astype(jnp.bfloat16)
    mid = DFT_SCALE * (jnp.sum(alt_sum * alt, axis=0, keepdims=True) + p_mid)
    for t in range(N_FOLD_TILES):
        src = FT * (N_FOLD_TILES - 1 - t)
        up = jnp.dot(rev, g[src:src + FT, :], preferred_element_type=jnp.float32)
        first = mid if t == 0 else g[src + FT:src + FT + 1, :].astype(jnp.float32)
        o_ref[HALF + t * FT:HALF + (t + 1) * FT, :] = jnp.where(row0, first, up).astype(jnp.bfloat16)
    ctx = jnp.dot(cc_ref[...], p_ref[SEQ:, :], preferred_element_type=jnp.float32)
    ctx = ctx + jnp.dot(sc_ref[...], q_ref[SEQ:, :], preferred_element_type=jnp.float32)
    o_ref[SEQ:, :] = ctx.astype(jnp.bfloat16)


def _dft(tables, pq):
    ch, sh, rev, cc, sc = tables
    batch = pq.shape[0]
    n_col = D_FOURIER // FW
    return pl.pallas_call(
        _dft_kernel,
        grid=(batch, n_col),
        in_specs=[
            _resident((HALF, HALF), lambda b, c: (0, 0)),
            _resident((HALF, HALF), lambda b, c: (0, 0)),
            _resident((FT, FT), lambda b, c: (0, 0)),
            _resident((CTX, CTX), lambda b, c: (0, 0)),
            _resident((CTX, CTX), lambda b, c: (0, 0)),
            pl.BlockSpec((None, ROWS, FW), lambda b, c: (b, 0, c)),
            pl.BlockSpec((None, ROWS, FW), lambda b, c: (b, 0, n_col + c)),
        ],
        out_specs=pl.BlockSpec((None, ROWS, FW), lambda b, c: (b, 0, c)),
        out_shape=jax.ShapeDtypeStruct((batch, ROWS, D_FOURIER), jnp.bfloat16),
        compiler_params=_cparams(
            2, resident=2 * HALF * HALF * BF16, streamed=3 * ROWS * FW * BF16,
            temporaries=HALF * FW * (2 * BF16 + 3 * F32) + 2 * ROWS * FW * F32),
        name="dft",
    )(ch, sh, rev, cc, sc, pq, pq)


KEY_TILE = MXU_DIM
N_KEY_TILES = ROWS // KEY_TILE


def _with_row_sums(v):
    return jnp.concatenate([v, jnp.ones_like(v)], axis=1)


def _normalised(pv):
    return (pv[:, :HEAD_DIM] / pv[:, HEAD_DIM:]).astype(jnp.bfloat16)


def _attn_kernel(q_ref, kt_ref, v_prev_ref, v_ref, o_prev_ref, o_ref, s_ref, m_ref):
    @pl.when(pl.program_id(0) == 0)
    def _():
        s_ref[...] = jnp.zeros(s_ref.shape, s_ref.dtype)
        m_ref[...] = jnp.zeros(m_ref.shape, m_ref.dtype)

    for half, (vals_ref, out_ref) in enumerate(((v_prev_ref, o_prev_ref), (v_ref, o_ref))):
        outs = []
        for hd in range(GROUP):
            q = q_ref[half * TM:(half + 1) * TM, hd * HEAD_DIM:(hd + 1) * HEAD_DIM]
            m_prev = m_ref[hd]
            m_lane = None
            acc = None
            for j in range(N_KEY_TILES):
                keys = slice(j * KEY_TILE, (j + 1) * KEY_TILE)
                s_old = s_ref[hd, :, keys]
                p = jnp.exp2(s_old - jnp.concatenate([m_prev] * (KEY_TILE // HEAD_DIM), axis=1))
                pv = jnp.dot(p.astype(jnp.bfloat16), _with_row_sums(vals_ref[keys, :]),
                             preferred_element_type=jnp.float32)
                acc = pv if acc is None else acc + pv
                s_new = jnp.dot(q, kt_ref[:, keys], preferred_element_type=jnp.float32)
                s_ref[hd, :, keys] = s_new
                for c in range(KEY_TILE // HEAD_DIM):
                    part = s_new[:, c * HEAD_DIM:(c + 1) * HEAD_DIM]
                    m_lane = part if m_lane is None else jnp.maximum(m_lane, part)
            outs.append(_normalised(acc))
            m_ref[hd] = jnp.broadcast_to(jnp.max(m_lane, axis=-1, keepdims=True), m_lane.shape)
        out_ref[...] = jnp.concatenate(outs, axis=1)


def _attention_latent(q, kt, v):
    batch = q.shape[0]
    gw = GROUP * HEAD_DIM
    n_pairs = N_LAT_TILES // 2
    per_sample = N_KV * n_pairs
    steps = batch * per_sample + 1

    def unflatten(c):
        return c // per_sample, (c % per_sample) // n_pairs, c % n_pairs

    def cur(t):
        return unflatten(jnp.minimum(t, steps - 2))

    def prev(t):
        return unflatten(jnp.maximum(t - 1, 0))

    def tile_map(which):
        def index(t):
            b, g, i = which(t)
            return b, i, g
        return index

    def kt_map(t):
        b, g, _ = cur(t)
        return g, 0, b

    def v_map(which):
        def index(t):
            b, g, _ = which(t)
            return b, 0, g
        return index

    half_shape = jax.ShapeDtypeStruct((batch, SEQ // 2, D), jnp.bfloat16)
    odd, even = pl.pallas_call(
        _attn_kernel,
        grid=(steps,),
        in_specs=[
            pl.BlockSpec((None, PAIR, gw), tile_map(cur)),
            pl.BlockSpec((None, HEAD_DIM, ROWS), kt_map),
            pl.BlockSpec((None, ROWS, HEAD_DIM), v_map(prev)),
            pl.BlockSpec((None, ROWS, HEAD_DIM), v_map(cur)),
        ],
        out_specs=[pl.BlockSpec((None, TM, gw), tile_map(prev)),
                   pl.BlockSpec((None, TM, gw), tile_map(cur))],
        out_shape=[half_shape, half_shape],
        scratch_shapes=[pltpu.VMEM((GROUP, TM, ROWS), jnp.float32),
                        pltpu.VMEM((GROUP, TM, HEAD_DIM), jnp.float32)],
        compiler_params=_cparams(
            1, streamed=(3 * TM * gw + 3 * ROWS * HEAD_DIM) * BF16 + TM * gw * BF16,
            scratch=GROUP * TM * (ROWS + HEAD_DIM) * F32,
            temporaries=2 * GROUP * TM * 2 * HEAD_DIM * F32 + 4 * TM * KEY_TILE * F32),
        name="attention",
    )(q, kt, v, v)
    return even, odd


def _attn_ctx_kernel(q_ref, kt_ref, v_ref, o_ref):
    v1 = _with_row_sums(v_ref[...])
    kt = kt_ref[...]
    outs = []
    for hd in range(GROUP):
        s = jnp.dot(q_ref[:, hd * HEAD_DIM:(hd + 1) * HEAD_DIM], kt,
                    preferred_element_type=jnp.float32)
        p = jnp.exp2(s - jnp.max(s, axis=-1, keepdims=True)).astype(jnp.bfloat16)
        outs.append(_normalised(jnp.dot(p, v1, preferred_element_type=jnp.float32)))
    o_ref[...] = jnp.concatenate(outs, axis=1)


def _attention_context(q, kt, v):
    batch = q.shape[0]
    gw = GROUP * HEAD_DIM
    return pl.pallas_call(
        _attn_ctx_kernel,
        grid=(batch, N_KV),
        in_specs=[
            pl.BlockSpec((None, CTX, gw), lambda b, g: (b, SEQ // CTX, g)),
            pl.BlockSpec((None, HEAD_DIM, CTX), lambda b, g: (g, 0, b * N_TILES + N_LAT_TILES)),
            pl.BlockSpec((None, CTX, HEAD_DIM), lambda b, g: (b, SEQ // CTX, g)),
        ],
        out_specs=pl.BlockSpec((None, CTX, gw), lambda b, g: (b, 0, g)),
        out_shape=jax.ShapeDtypeStruct((batch, CTX, D), jnp.bfloat16),
        compiler_params=_cparams(
            2, streamed=2 * CTX * (gw + HEAD_DIM) * BF16,
            temporaries=GROUP * CTX * (2 * CTX + 2 * HEAD_DIM) * F32),
        name="attention_ctx",
    )(q, kt, v)


def _cos_sin_products(n, size):
    r = 1 << (int(math.log2(size)) // 2)
    hi = size // r
    assert r % SUBLANES == 0
    k = jnp.arange(size, dtype=jnp.int32)[None, :]
    ang_hi = ((jnp.arange(hi, dtype=jnp.int32)[:, None] * r * k) % n).astype(jnp.float32)
    ang_lo = ((jnp.arange(r, dtype=jnp.int32)[:, None] * k) % n).astype(jnp.float32)
    w = 2.0 * math.pi / n
    ca, sa = jnp.cos(ang_hi * w)[:, None, :], jnp.sin(ang_hi * w)[:, None, :]
    cb, sb = jnp.cos(ang_lo * w)[None, :, :], jnp.sin(ang_lo * w)[None, :, :]
    scale = n ** -0.5
    cos = ((ca * cb - sa * sb) * scale).reshape(size, size)
    sin = ((sa * cb + ca * sb) * scale).reshape(size, size)
    return cos, sin


def _dft_tables():
    ch, sh = _cos_sin_products(SEQ, HALF)
    cc, sc = _cos_sin_products(CTX, CTX)
    r = jnp.arange(FT, dtype=jnp.int32)
    rev = ((r[:, None] + r[None, :]) == FT).astype(jnp.bfloat16)
    return tuple(t.astype(jnp.bfloat16) for t in (ch, sh, rev, cc, sc))


def _channel_dft():
    c, s = _cos_sin_products(GROUP_CH, GROUP_CH)
    return jnp.concatenate([c, -s], axis=1).astype(jnp.bfloat16)


def _rope_tables():
    half = HEAD_DIM // 2
    rows = SEQ // GRID_W
    row = jnp.repeat(jnp.arange(rows, dtype=jnp.float32), GRID_W)
    col = jnp.tile(jnp.arange(GRID_W, dtype=jnp.float32), rows)
    inv_freq = ROPE_THETA ** (-jnp.arange(0, half, 2, dtype=jnp.float32) / half)
    ang_row = row[:, None] * inv_freq
    ang_col = col[:, None] * inv_freq
    cos = jnp.concatenate([jnp.cos(ang_row)] * 2 + [jnp.cos(ang_col)] * 2, axis=1)
    sin = jnp.concatenate([-jnp.sin(ang_row), jnp.sin(ang_row),
                           -jnp.sin(ang_col), jnp.sin(ang_col)], axis=1)
    cos = jnp.concatenate([cos, jnp.ones((CTX, HEAD_DIM), jnp.float32)], axis=0)
    sin = jnp.concatenate([sin, jnp.zeros((CTX, HEAD_DIM), jnp.float32)], axis=0)
    order = _rope_order()
    return cos[:, order], sin[:, order]


def _rope_order():
    quarter = HEAD_DIM // 4
    blocks = [0, 2, 1, 3]
    return jnp.concatenate([jnp.arange(b * quarter, (b + 1) * quarter) for b in blocks])


def _permute_qk_heads(w_qkv, g_q, g_k):
    order = _rope_order()
    n_cols = w_qkv.shape[-1]
    n_qk = (N_HEADS + N_KV) * HEAD_DIM
    heads = jnp.arange(n_qk, dtype=jnp.int32) // HEAD_DIM * HEAD_DIM
    src = jnp.concatenate([heads + jnp.tile(order, N_HEADS + N_KV),
                           jnp.arange(n_qk, n_cols, dtype=jnp.int32)])
    select = (jnp.arange(n_cols, dtype=jnp.int32)[:, None] == src[None, :]).astype(jnp.bfloat16)
    w = jnp.einsum("lik,kj->lij", w_qkv.astype(jnp.bfloat16), select,
                   preferred_element_type=jnp.float32).astype(jnp.bfloat16)
    return w, g_q[:, order], g_k[:, order]


def kernel(x, c, ctx, c_ctx, w_mod, b_mod, g_ffn1, w_ffn1_gu, w_ffn1_down, g_mix, g_ffn2, w_ffn2_gu, w_ffn2_down, g_final, w_in_ab, g_v, w_s, b_s, w_out_ab, w_qkv, g_q, g_k, w_o):
    batch = x.shape[0]
    depth = w_mod.shape[0]
    assert x.shape == (batch, SEQ, D) and ctx.shape == (batch, CTX, D)
    assert depth % 2 == 0

    mod_rows = -(-(batch + 1) // SUBLANES) * SUBLANES
    cc = jnp.concatenate([c, c_ctx[None], jnp.zeros((mod_rows - batch - 1, D), jnp.float32)], axis=0)
    m = _modulation(cc, w_mod, b_mod)
    m = m[:, :batch + 1].reshape(depth, batch + 1, N_MOD, D)

    wgu1, wd1 = w_ffn1_gu.astype(jnp.bfloat16), w_ffn1_down.astype(jnp.bfloat16)
    wgu2, wd2 = w_ffn2_gu.astype(jnp.bfloat16), w_ffn2_down.astype(jnp.bfloat16)
    g1 = g_ffn1.reshape(depth, 1, D)
    g2 = g_ffn2.reshape(depth, 1, D)
    gm = g_mix.reshape(depth, 1, D)
    w_in = w_in_ab.astype(jnp.bfloat16)
    w_out = w_out_ab.astype(jnp.bfloat16)
    ws = w_s.astype(jnp.bfloat16)
    gv = g_v.reshape(-1, 1, D_SGU)
    bs_full = jnp.repeat(jnp.swapaxes(b_s, 1, 2), GROUP_CH, axis=2)
    wqkv, g_q_p, g_k_p = _permute_qk_heads(w_qkv, g_q, g_k)
    wo = w_o.astype(jnp.bfloat16)
    gq = g_q_p.reshape(-1, 1, HEAD_DIM)
    gk = g_k_p.reshape(-1, 1, HEAD_DIM)
    dft_tables = _dft_tables()
    ccs = _channel_dft()
    cos_t, sin_t = _rope_tables()

    xs = (x, ctx)
    for l in range(depth):
        last = l == depth - 1
        even = l % 2 == 0
        mods = m[l]
        n_out = N_LAT_TILES if last else N_TILES
        if even:
            xc, pq, sgu = _layer_in(xs, mods, g1, wgu1, wd1, gm, l, True,
                                    (w_in, ccs, gv, ws, bs_full))
            ys, w_proj = [_dft(dft_tables, pq), sgu], w_out
        else:
            xc, q, kt, v = _layer_in(xs, mods, g1, wgu1, wd1, gm, l, False,
                                     (wqkv, gq, gk, cos_t, sin_t))
            ys, w_proj = list(_attention_latent(q, kt, v)), wo
            if not last:
                ys.append(_attention_context(q, kt, v))
        xc = _layer_out(xc, mods, w_proj, ys, g2, wgu2, wd2, l, n_out,
                        g_final.reshape(1, D) if last else None,
                        split_y=not even)
        xs = (xc,)
    return xc
```

```python
import functools
import math

import jax
import jax.numpy as jnp
from jax import lax
from jax.experimental import pallas as pl
from jax.experimental.pallas import tpu as pltpu

D = 1024
SEQ = 4096
CTX = 256
ROWS = SEQ + CTX
GRID_W = 64
D_FF = 2752
N_MOD = 9
EPS = 1e-6
HEAD_DIM = 128
N_HEADS = 8
N_KV = 2
GROUP = N_HEADS // N_KV
KV_W = N_KV * HEAD_DIM
ROPE_THETA = 10000.0
D_FOURIER = 512
D_SGU = 512
GROUP_CH = 128
N_GROUPS = 4
CHUNK = 128

MXU_DIM = 256
TM = 256
N_LAT_TILES = SEQ // TM
N_TILES = ROWS // TM
V7X_VMEM_BYTES = 64 * 1024 * 1024
SUBLANES = 8
F32, BF16 = 4, 2

assert TM == CTX and SEQ % TM == 0


def _cparams(n_axes, resident=0, streamed=0, scratch=0, temporaries=0):
    need = resident + 2 * streamed + scratch + temporaries
    assert need <= V7X_VMEM_BYTES, need
    return pltpu.CompilerParams(dimension_semantics=("arbitrary",) * n_axes,
                                vmem_limit_bytes=need)


def _resident(block_shape, index_map):
    return pl.BlockSpec(block_shape, index_map, pipeline_mode=pl.Buffered(1))


def _modnorm(x, g, shift, scale):
    ms = jnp.mean(x * x, axis=-1, keepdims=True)
    return x * lax.rsqrt(ms + EPS) * (g * (1.0 + scale)) + shift


def _ffn_norm(x, mod_ref, row0, g):
    h = _modnorm(x, g, mod_ref[row0:row0 + 1, :], mod_ref[row0 + 1:row0 + 2, :])
    return h.astype(jnp.bfloat16)


def _per_tile(fn, x, mod_refs):
    return jnp.concatenate([fn(x[i * TM:(i + 1) * TM], m) for i, m in enumerate(mod_refs)],
                           axis=0)


def _swiglu(h, wgu_ref, wd_ref):
    gu = jnp.dot(h, wgu_ref[...], preferred_element_type=jnp.float32)
    gt, up = gu[:, :D_FF], gu[:, D_FF:]
    a = (gt * jax.nn.sigmoid(gt) * up).astype(jnp.bfloat16)
    return jnp.dot(a, wd_ref[...], preferred_element_type=jnp.float32)


def _ffn_half_step(x, mod_refs, row0, g, wgu_ref, wd_ref, h=None):
    if h is None:
        h = _per_tile(lambda t, m: _ffn_norm(t, m, row0, g), x, mod_refs)
    y = _swiglu(h, wgu_ref, wd_ref)
    return x + _per_tile(lambda t, m: 0.5 * m[row0 + 2:row0 + 3, :] * t, y, mod_refs)


def _mod_kernel(cc_ref, w_ref, b_ref, o_ref):
    s = cc_ref[...]
    s = s * jax.nn.sigmoid(s)
    s_hi = s.astype(jnp.bfloat16)
    s_lo = (s - s_hi.astype(jnp.float32)).astype(jnp.bfloat16)
    w = w_ref[...]
    w_hi = w.astype(jnp.bfloat16)
    w_lo = (w - w_hi.astype(jnp.float32)).astype(jnp.bfloat16)
    rows = s.shape[0]
    both = jnp.dot(jnp.concatenate([s_hi, s_lo], axis=0), w_hi,
                   preferred_element_type=jnp.float32)
    cross = jnp.dot(s_hi, w_lo, preferred_element_type=jnp.float32)
    o_ref[...] = both[:rows] + both[rows:] + cross + b_ref[...]


def _modulation(cc, w_mod, b_mod):
    depth = w_mod.shape[0]
    rows = cc.shape[0]
    tn = D
    return pl.pallas_call(
        _mod_kernel,
        grid=(depth, N_MOD * D // tn),
        in_specs=[
            pl.BlockSpec((rows, D), lambda l, j: (0, 0)),
            pl.BlockSpec((None, D, tn), lambda l, j: (l, 0, j)),
            pl.BlockSpec((None, 1, tn), lambda l, j: (l, 0, j)),
        ],
        out_specs=pl.BlockSpec((None, rows, tn), lambda l, j: (l, 0, j)),
        out_shape=jax.ShapeDtypeStruct((depth, rows, N_MOD * D), jnp.float32),
        compiler_params=_cparams(2, streamed=(D + 2 * rows) * tn * F32,
                                 temporaries=2 * D * tn * BF16 + 4 * rows * tn * F32),
        name="modulation",
    )(cc, w_mod, b_mod.reshape(depth, 1, N_MOD * D))


def _even_in_stage(p, ccs_ref, gv_ref, ws_ref, bs_ref, pq_ref, sgu_ref):
    a = p[:, :D_FOURIER].astype(jnp.bfloat16)
    uv = jax.nn.gelu(p[:, D_FOURIER:], approximate=True)
    u = uv[:, :D_SGU]
    v = uv[:, D_SGU:]
    ccs = ccs_ref[...]
    ps, qs, gated = [], [], []
    n_chunks = p.shape[0] // CHUNK
    for grp in range(N_GROUPS):
        lo, hi = grp * GROUP_CH, (grp + 1) * GROUP_CH
        t = jnp.dot(a[:, lo:hi], ccs, preferred_element_type=jnp.float32)
        ps.append(t[:, :GROUP_CH])
        qs.append(t[:, GROUP_CH:])
        vg = v[:, lo:hi]
        ms = jnp.mean(vg * vg, axis=-1, keepdims=True)
        vh = (vg * lax.rsqrt(ms + EPS) * gv_ref[:, lo:hi]).astype(jnp.bfloat16)
        rhs = jnp.concatenate([vh[c * CHUNK:(c + 1) * CHUNK, :] for c in range(n_chunks)], axis=1)
        mixed = jnp.dot(ws_ref[grp], rhs, preferred_element_type=jnp.float32)
        mixed = jnp.concatenate(
            [mixed[:, c * GROUP_CH:(c + 1) * GROUP_CH] for c in range(n_chunks)], axis=0)
        bias = jnp.concatenate([bs_ref[:, lo:hi]] * n_chunks, axis=0)
        gated.append(u[:, lo:hi] * (mixed + bias))
    pq_ref[...] = jnp.concatenate(ps + qs, axis=1).astype(jnp.bfloat16)
    sgu_ref[...] = jnp.concatenate(gated, axis=1).astype(jnp.bfloat16)


def _rope(t, cos, sin_signed):
    return t * cos + pltpu.roll(t, HEAD_DIM // 2, 1) * sin_signed


def _qkv_stage(qkv, gq_ref, gk_ref, cos, sin, q_ref, kt_ref, v_ref):
    q_scale = (HEAD_DIM ** -0.5) * math.log2(math.e)

    def head(col, gain):
        t = qkv[:, col:col + HEAD_DIM]
        ms = jnp.mean(t * t, axis=-1, keepdims=True)
        return _rope(t * lax.rsqrt(ms + EPS) * gain, cos, sin)

    qs = [head(hd * HEAD_DIM, gq_ref[...]) * q_scale for hd in range(N_HEADS)]
    q_ref[...] = jnp.concatenate(qs, axis=1).astype(jnp.bfloat16)
    for hd in range(N_KV):
        kt_ref[hd] = head((N_HEADS + hd) * HEAD_DIM, gk_ref[...]).T.astype(jnp.bfloat16)
    v_ref[...] = qkv[:, (N_HEADS + N_KV) * HEAD_DIM:].astype(jnp.bfloat16)


PAIR = 2 * TM
_STAGE_COLS = D_FOURIER + 2 * D_SGU
assert _STAGE_COLS == D + 2 * KV_W
_FFN_WEIGHT_BYTES = 3 * D * D_FF * BF16
_FFN_TEMP_BYTES = PAIR * (2 * D_FF * F32 + D_FF * BF16 + D * BF16 + 2 * D * F32)


def _layer_in_kernel(*refs, split_in, even, steps):
    refs = list(refs)
    h_ref = refs.pop()
    t = pl.program_id(0)

    @pl.when(t == 0)
    def _():
        h_ref[...] = jnp.zeros(h_ref.shape, h_ref.dtype)

    if split_in:
        tile0 = 2 * jnp.minimum(t, steps - 2)
        x = jnp.concatenate(
            [jnp.where((tile0 + i) % N_TILES == N_LAT_TILES, refs[2 * i + 1][...], refs[2 * i][...])
             for i in range(2)], axis=0)
        del refs[:4]
    else:
        x = refs.pop(0)[...]
    mod_refs = refs[:2]
    g1_ref, wgu_ref, wd_ref, gm_ref, w_stage_ref = refs[2:7]
    n_out = 3 if even else 4
    stage_in = refs[7:-n_out]
    xo_ref, *stage_out = refs[-n_out:]
    proj = jnp.dot(h_ref[...], w_stage_ref[...], preferred_element_type=jnp.float32)
    gm = gm_ref[...]
    if even:
        x = _ffn_half_step(x, mod_refs, 0, g1_ref[...], wgu_ref, wd_ref)
        xo_ref[...] = x
        h_ref[...] = _per_tile(lambda r, m: _ffn_norm(r, m, 3, gm), x, mod_refs)
        _even_in_stage(proj, *stage_in, *stage_out)
    else:
        gq_ref, gk_ref, cos0, sin0, cos1, sin1 = stage_in
        _qkv_stage(proj, gq_ref, gk_ref, jnp.concatenate([cos0[...], cos1[...]], axis=0),
                   jnp.concatenate([sin0[...], sin1[...]], axis=0), *stage_out)
        x = _ffn_half_step(x, mod_refs, 0, g1_ref[...], wgu_ref, wd_ref)
        xo_ref[...] = x
        h_ref[...] = _per_tile(lambda r, m: _ffn_norm(r, m, 3, gm), x, mod_refs)


def _layer_in(xs, mods, g1, wgu, wd, gm, layer, even, stage_args):
    batch = xs[0].shape[0]
    j = layer // 2
    split_in = len(xs) == 2
    n_rows = batch * ROWS
    assert (batch * N_TILES) % 2 == 0
    steps = batch * N_TILES // 2 + 1

    def cur(t):
        return jnp.minimum(t, steps - 2)

    def prev(t):
        return jnp.maximum(t - 1, 0)

    def tile(pair, i):
        n = 2 * pair + i
        return n // N_TILES, n % N_TILES

    def rows_spec(width, which):
        return pl.BlockSpec((PAIR, width), lambda t: (which(t), 0))

    def const(block_shape, *index):
        return _resident(block_shape, lambda t: index)

    def mod_spec(i):
        def index(t):
            b, r = tile(cur(t), i)
            return jnp.where(r == N_LAT_TILES, batch, b), 0, 0
        return pl.BlockSpec((None, N_MOD, D), index)

    if split_in:
        x_arrs, x_specs = [], []
        for i in range(2):
            def lat(t, i=i):
                b, r = tile(cur(t), i)
                return b, jnp.minimum(r, N_LAT_TILES - 1), 0
            x_arrs += list(xs)
            x_specs += [pl.BlockSpec((None, TM, D), lat),
                        pl.BlockSpec((None, CTX, D), lambda t, i=i: (tile(cur(t), i)[0], 0, 0))]
    else:
        x_arrs, x_specs = [xs[0].reshape(n_rows, D)], [rows_spec(D, cur)]
    in_specs = x_specs + [
        mod_spec(0), mod_spec(1),
        const((None, 1, D), layer, 0, 0),
        const((None, D, 2 * D_FF), layer, 0, 0),
        const((None, D_FF, D), layer, 0, 0),
        const((None, 1, D), layer, 0, 0),
    ]
    out_specs = [rows_spec(D, cur)]
    out_shape = [jax.ShapeDtypeStruct((n_rows, D), jnp.float32)]
    if even:
        in_specs += [
            const((None, D, D_FOURIER + 2 * D_SGU), j, 0, 0),
            const((GROUP_CH, 2 * GROUP_CH), 0, 0),
            const((None, 1, D_SGU), j, 0, 0),
            const((None, N_GROUPS, CHUNK, CHUNK), j, 0, 0, 0),
            const((None, CHUNK, D_SGU), j, 0, 0),
        ]
        out_specs += [rows_spec(2 * D_FOURIER, prev), rows_spec(D_SGU, prev)]
        out_shape += [jax.ShapeDtypeStruct((n_rows, 2 * D_FOURIER), jnp.bfloat16),
                      jax.ShapeDtypeStruct((n_rows, D_SGU), jnp.bfloat16)]
        stage_arrs = list(stage_args)
    else:
        wqkv, gq, gk, cos_t, sin_t = stage_args

        def rope_spec(i):
            return pl.BlockSpec((TM, HEAD_DIM), lambda t: (tile(prev(t), i)[1], 0))

        in_specs += [
            const((None, D, D + 2 * KV_W), j, 0, 0),
            const((None, 1, HEAD_DIM), j, 0, 0),
            const((None, 1, HEAD_DIM), j, 0, 0),
            rope_spec(0), rope_spec(0), rope_spec(1), rope_spec(1),
        ]
        stage_arrs = [wqkv, gq, gk, cos_t, sin_t, cos_t, sin_t]
        out_specs += [
            rows_spec(D, prev),
            pl.BlockSpec((N_KV, HEAD_DIM, PAIR), lambda t: (0, 0, prev(t))),
            rows_spec(KV_W, prev),
        ]
        out_shape += [jax.ShapeDtypeStruct((n_rows, D), jnp.bfloat16),
                      jax.ShapeDtypeStruct((N_KV, HEAD_DIM, n_rows), jnp.bfloat16),
                      jax.ShapeDtypeStruct((n_rows, KV_W), jnp.bfloat16)]
    outs = pl.pallas_call(
        functools.partial(_layer_in_kernel, split_in=split_in, even=even, steps=steps),
        grid=(steps,),
        in_specs=in_specs,
        out_specs=out_specs,
        out_shape=out_shape,
        scratch_shapes=[pltpu.VMEM((PAIR, D), jnp.bfloat16)],
        compiler_params=_cparams(
            1, resident=_FFN_WEIGHT_BYTES + D * _STAGE_COLS * BF16,
            streamed=PAIR * (2 * D * F32 + (_STAGE_COLS + (0 if even else KV_W)) * BF16),
            scratch=PAIR * D * BF16,
            temporaries=_FFN_TEMP_BYTES + 3 * PAIR * _STAGE_COLS * F32),
        name="layer_in_even" if even else "layer_in_odd",
    )(*x_arrs, mods, mods, g1, wgu, wd, gm, *stage_arrs)
    return [o if o.shape[0] == N_KV and o.ndim == 3 else o.reshape(batch, ROWS, o.shape[-1])
            for o in outs]


def _layer_out_kernel(x_ref, *refs, n_mod, n_y, split_y, final):
    mod_refs = refs[:n_mod] * (2 // n_mod)
    wo_ref = refs[n_mod]
    y_refs = refs[n_mod + 1:n_mod + 1 + n_y]
    g2_ref, wgu_ref, wd_ref = refs[n_mod + 1 + n_y:n_mod + 4 + n_y]
    o_ref = refs[-1]
    if split_y and n_y == 2:
        y = jnp.concatenate([r[...] for r in y_refs], axis=0)
    elif split_y:
        tile0 = 2 * pl.program_id(0)

        def pick(i):
            r = (tile0 + i) % N_TILES
            even, odd, ctx = (y_ref[...] for y_ref in y_refs[3 * i:3 * i + 3])
            return jnp.where(r == N_LAT_TILES, ctx, jnp.where(r % 2 == 1, odd, even))

        y = jnp.concatenate([pick(0), pick(1)], axis=0)
    else:
        y = jnp.concatenate([r[...] for r in y_refs], axis=1)
    proj = jnp.dot(y, wo_ref[...], preferred_element_type=jnp.float32)
    x = x_ref[...] + _per_tile(lambda p, m: m[5:6, :] * p, proj, mod_refs)
    x = _ffn_half_step(x, mod_refs, 6, g2_ref[...], wgu_ref, wd_ref)
    if final:
        gf_ref = refs[-2]
        ms = jnp.mean(x * x, axis=-1, keepdims=True)
        x = x * lax.rsqrt(ms + EPS) * gf_ref[...]
    o_ref[...] = x


def _layer_out(xc, mods, wo, ys, g2, wgu, wd, layer, n_tiles, g_final=None, split_y=False):
    batch = xc.shape[0]
    j = layer // 2
    final = g_final is not None

    def const(block_shape, *index):
        return _resident(block_shape, lambda *_: index)

    if n_tiles == N_LAT_TILES:
        grid = (batch, N_LAT_TILES // 2)
        x_arr, x_spec = xc, pl.BlockSpec((None, PAIR, D), lambda b, k: (b, k, 0))
        mod_specs = [pl.BlockSpec((None, N_MOD, D), lambda b, k: (b, 0, 0))]
        y_arrs = list(ys)
        y_rows = TM if split_y else PAIR
        y_specs = [pl.BlockSpec((None, y_rows, y.shape[-1]), lambda b, k: (b, k, 0)) for y in ys]
        out_spec = pl.BlockSpec((None, PAIR, D), lambda b, k: (b, k, 0))
        out_shape = jax.ShapeDtypeStruct((batch, SEQ, D), jnp.float32)
    else:
        assert (batch * N_TILES) % 2 == 0
        grid = (batch * N_TILES // 2,)
        x_arr, x_spec = xc.reshape(batch * ROWS, D), pl.BlockSpec((PAIR, D), lambda k: (k, 0))

        def tile(k, i):
            t = 2 * k + i
            return t // N_TILES, t % N_TILES

        def mod_spec(i):
            def index(k):
                b, r = tile(k, i)
                return jnp.where(r == N_LAT_TILES, batch, b), 0, 0
            return pl.BlockSpec((None, N_MOD, D), index)

        mod_specs = [mod_spec(0), mod_spec(1)]
        if split_y:
            y_arrs, y_specs = [], []
            for i in range(2):
                def lat(k, i=i):
                    b, r = tile(k, i)
                    return b, jnp.minimum(r, N_LAT_TILES - 1) // 2, 0
                y_arrs += list(ys)
                y_specs += [pl.BlockSpec((None, TM, D), lat), pl.BlockSpec((None, TM, D), lat),
                            pl.BlockSpec((None, CTX, D), lambda k, i=i: (tile(k, i)[0], 0, 0))]
        else:
            y_arrs = [y.reshape(batch * ROWS, y.shape[-1]) for y in ys]
            y_specs = [pl.BlockSpec((PAIR, y.shape[-1]), lambda k: (k, 0)) for y in y_arrs]
        out_spec = pl.BlockSpec((PAIR, D), lambda k: (k, 0))
        out_shape = jax.ShapeDtypeStruct((batch * ROWS, D), jnp.float32)
    in_specs = [x_spec] + mod_specs + [const((None, D, D), j, 0, 0)] + y_specs + [
        const((None, 1, D), layer, 0, 0),
        const((None, D, 2 * D_FF), layer, 0, 0),
        const((None, D_FF, D), layer, 0, 0),
    ]
    args = [x_arr] + [mods] * len(mod_specs) + [wo] + y_arrs + [g2, wgu, wd]
    if final:
        in_specs.append(const((1, D), 0, 0))
        args.append(g_final)
    out = pl.pallas_call(
        functools.partial(_layer_out_kernel, n_mod=len(mod_specs), n_y=len(y_arrs),
                          split_y=split_y, final=final),
        grid=grid,
        in_specs=in_specs,
        out_specs=out_spec,
        out_shape=out_shape,
        compiler_params=_cparams(
            len(grid), resident=_FFN_WEIGHT_BYTES + D * D * BF16,
            streamed=PAIR * (2 * D * F32 + D * BF16),
            temporaries=_FFN_TEMP_BYTES + 2 * PAIR * D * F32),
        name="layer_out_final" if final else "layer_out",
    )(*args)
    return out.reshape(batch, -1, D)


HALF = SEQ // 2
FT = MXU_DIM
FW = MXU_DIM
N_FOLD_TILES = HALF // FT
DFT_SCALE = SEQ ** -0.5


def _dft_kernel(ch_ref, sh_ref, rev_ref, cc_ref, sc_ref, p_ref, q_ref, o_ref):
    rev = rev_ref[...]
    row0 = lax.broadcasted_iota(jnp.int32, (FT, FW), 0) == 0
    alt = (1 - 2 * (lax.broadcasted_iota(jnp.int32, (FT, FW), 0) & 1)).astype(jnp.float32)

    def reversed_upper(x_ref, t):
        src = HALF + FT * (N_FOLD_TILES - 1 - t)
        r = jnp.dot(rev, x_ref[src:src + FT, :], preferred_element_type=jnp.float32)
        if t == 0:
            return r
        first = HALF + FT * (N_FOLD_TILES - t)
        return jnp.where(row0, x_ref[first:first + 1, :].astype(jnp.float32), r)

    pfs, qfs, alt_sum = [], [], None
    for t in range(N_FOLD_TILES):
        rows = slice(t * FT, (t + 1) * FT)
        pf = p_ref[rows, :].astype(jnp.float32) + reversed_upper(p_ref, t)
        qf = q_ref[rows, :].astype(jnp.float32) - reversed_upper(q_ref, t)
        alt_sum = pf if alt_sum is None else alt_sum + pf
        pfs.append(pf.astype(jnp.bfloat16))
        qfs.append(qf.astype(jnp.bfloat16))
    p_mid = p_ref[HALF:HALF + 1, :].astype(jnp.float32)
    sign = jnp.concatenate([alt] * N_FOLD_TILES, axis=0)
    e = jnp.dot(ch_ref[...], jnp.concatenate(pfs, axis=0), preferred_element_type=jnp.float32)
    e = e + sign * (DFT_SCALE * p_mid)
    o = jnp.dot(sh_ref[...], jnp.concatenate(qfs, axis=0), preferred_element_type=jnp.float32)
    o_ref[0:HALF, :] = (e + o).astype(jnp.bfloat16)
    g = (e - o).astype(jnp.bfloat16)
    mid = DFT_SCALE * (jnp.sum(alt_sum * alt, axis=0, keepdims=True) + p_mid)
    for t in range(N_FOLD_TILES):
        src = FT * (N_FOLD_TILES - 1 - t)
        up = jnp.dot(rev, g[src:src + FT, :], preferred_element_type=jnp.float32)
        first = mid if t == 0 else g[src + FT:src + FT + 1, :].astype(jnp.float32)
        o_ref[HALF + t * FT:HALF + (t + 1) * FT, :] = jnp.where(row0, first, up).astype(jnp.bfloat16)
    ctx = jnp.dot(cc_ref[...], p_ref[SEQ:, :], preferred_element_type=jnp.float32)
    ctx = ctx + jnp.dot(sc_ref[...], q_ref[SEQ:, :], preferred_element_type=jnp.float32)
    o_ref[SEQ:, :] = ctx.astype(jnp.bfloat16)


def _dft(tables, pq):
    ch, sh, rev, cc, sc = tables
    batch = pq.shape[0]
    n_col = D_FOURIER // FW
    return pl.pallas_call(
        _dft_kernel,
        grid=(batch, n_col),
        in_specs=[
            _resident((HALF, HALF), lambda b, c: (0, 0)),
            _resident((HALF, HALF), lambda b, c: (0, 0)),
            _resident((FT, FT), lambda b, c: (0, 0)),
            _resident((CTX, CTX), lambda b, c: (0, 0)),
            _resident((CTX, CTX), lambda b, c: (0, 0)),
            pl.BlockSpec((None, ROWS, FW), lambda b, c: (b, 0, c)),
            pl.BlockSpec((None, ROWS, FW), lambda b, c: (b, 0, n_col + c)),
        ],
        out_specs=pl.BlockSpec((None, ROWS, FW), lambda b, c: (b, 0, c)),
        out_shape=jax.ShapeDtypeStruct((batch, ROWS, D_FOURIER), jnp.bfloat16),
        compiler_params=_cparams(
            2, resident=2 * HALF * HALF * BF16, streamed=3 * ROWS * FW * BF16,
            temporaries=HALF * FW * (2 * BF16 + 3 * F32) + 2 * ROWS * FW * F32),
        name="dft",
    )(ch, sh, rev, cc, sc, pq, pq)


KEY_TILE = MXU_DIM
N_KEY_TILES = ROWS // KEY_TILE


def _with_row_sums(v):
    return jnp.concatenate([v, jnp.ones_like(v)], axis=1)


def _normalised(pv):
    return (pv[:, :HEAD_DIM] / pv[:, HEAD_DIM:]).astype(jnp.bfloat16)


def _attn_kernel(q_ref, kt_ref, v_prev_ref, v_ref, o_prev_ref, o_ref, s_ref, m_ref):
    @pl.when(pl.program_id(0) == 0)
    def _():
        s_ref[...] = jnp.zeros(s_ref.shape, s_ref.dtype)
        m_ref[...] = jnp.zeros(m_ref.shape, m_ref.dtype)

    for half, (vals_ref, out_ref) in enumerate(((v_prev_ref, o_prev_ref), (v_ref, o_ref))):
        outs = []
        for hd in range(GROUP):
            q = q_ref[half * TM:(half + 1) * TM, hd * HEAD_DIM:(hd + 1) * HEAD_DIM]
            m_prev = m_ref[hd]
            m_lane = None
            acc = None
            for j in range(N_KEY_TILES):
                keys = slice(j * KEY_TILE, (j + 1) * KEY_TILE)
                s_old = s_ref[hd, :, keys]
                p = jnp.exp2(s_old - jnp.concatenate([m_prev] * (KEY_TILE // HEAD_DIM), axis=1))
                pv = jnp.dot(p.astype(jnp.bfloat16), _with_row_sums(vals_ref[keys, :]),
                             preferred_element_type=jnp.float32)
                acc = pv if acc is None else acc + pv
                s_new = jnp.dot(q, kt_ref[:, keys], preferred_element_type=jnp.float32)
                s_ref[hd, :, keys] = s_new
                for c in range(KEY_TILE // HEAD_DIM):
                    part = s_new[:, c * HEAD_DIM:(c + 1) * HEAD_DIM]
                    m_lane = part if m_lane is None else jnp.maximum(m_lane, part)
            outs.append(_normalised(acc))
            m_ref[hd] = jnp.broadcast_to(jnp.max(m_lane, axis=-1, keepdims=True), m_lane.shape)
        out_ref[...] = jnp.concatenate(outs, axis=1)


def _attention_latent(q, kt, v):
    batch = q.shape[0]
    gw = GROUP * HEAD_DIM
    n_pairs = N_LAT_TILES // 2
    per_sample = N_KV * n_pairs
    steps = batch * per_sample + 1

    def unflatten(c):
        return c // per_sample, (c % per_sample) // n_pairs, c % n_pairs

    def cur(t):
        return unflatten(jnp.minimum(t, steps - 2))

    def prev(t):
        return unflatten(jnp.maximum(t - 1, 0))

    def tile_map(which):
        def index(t):
            b, g, i = which(t)
            return b, i, g
        return index

    def kt_map(t):
        b, g, _ = cur(t)
        return g, 0, b

    def v_map(which):
        def index(t):
            b, g, _ = which(t)
            return b, 0, g
        return index

    half_shape = jax.ShapeDtypeStruct((batch, SEQ // 2, D), jnp.bfloat16)
    odd, even = pl.pallas_call(
        _attn_kernel,
        grid=(steps,),
        in_specs=[
            pl.BlockSpec((None, PAIR, gw), tile_map(cur)),
            pl.BlockSpec((None, HEAD_DIM, ROWS), kt_map),
            pl.BlockSpec((None, ROWS, HEAD_DIM), v_map(prev)),
            pl.BlockSpec((None, ROWS, HEAD_DIM), v_map(cur)),
        ],
        out_specs=[pl.BlockSpec((None, TM, gw), tile_map(prev)),
                   pl.BlockSpec((None, TM, gw), tile_map(cur))],
        out_shape=[half_shape, half_shape],
        scratch_shapes=[pltpu.VMEM((GROUP, TM, ROWS), jnp.float32),
                        pltpu.VMEM((GROUP, TM, HEAD_DIM), jnp.float32)],
        compiler_params=_cparams(
            1, streamed=(3 * TM * gw + 3 * ROWS * HEAD_DIM) * BF16 + TM * gw * BF16,
            scratch=GROUP * TM * (ROWS + HEAD_DIM) * F32,
            temporaries=2 * GROUP * TM * 2 * HEAD_DIM * F32 + 4 * TM * KEY_TILE * F32),
        name="attention",
    )(q, kt, v, v)
    return even, odd


def _attn_ctx_kernel(q_ref, kt_ref, v_ref, o_ref):
    v1 = _with_row_sums(v_ref[...])
    kt = kt_ref[...]
    outs = []
    for hd in range(GROUP):
        s = jnp.dot(q_ref[:, hd * HEAD_DIM:(hd + 1) * HEAD_DIM], kt,
                    preferred_element_type=jnp.float32)
        p = jnp.exp2(s - jnp.max(s, axis=-1, keepdims=True)).astype(jnp.bfloat16)
        outs.append(_normalised(jnp.dot(p, v1, preferred_element_type=jnp.float32)))
    o_ref[...] = jnp.concatenate(outs, axis=1)


def _attention_context(q, kt, v):
    batch = q.shape[0]
    gw = GROUP * HEAD_DIM
    return pl.pallas_call(
        _attn_ctx_kernel,
        grid=(batch, N_KV),
        in_specs=[
            pl.BlockSpec((None, CTX, gw), lambda b, g: (b, SEQ // CTX, g)),
            pl.BlockSpec((None, HEAD_DIM, CTX), lambda b, g: (g, 0, b * N_TILES + N_LAT_TILES)),
            pl.BlockSpec((None, CTX, HEAD_DIM), lambda b, g: (b, SEQ // CTX, g)),
        ],
        out_specs=pl.BlockSpec((None, CTX, gw), lambda b, g: (b, 0, g)),
        out_shape=jax.ShapeDtypeStruct((batch, CTX, D), jnp.bfloat16),
        compiler_params=_cparams(
            2, streamed=2 * CTX * (gw + HEAD_DIM) * BF16,
            temporaries=GROUP * CTX * (2 * CTX + 2 * HEAD_DIM) * F32),
        name="attention_ctx",
    )(q, kt, v)


def _cos_sin_products(n, size):
    r = 1 << (int(math.log2(size)) // 2)
    hi = size // r
    assert r % SUBLANES == 0
    k = jnp.arange(size, dtype=jnp.int32)[None, :]
    ang_hi = ((jnp.arange(hi, dtype=jnp.int32)[:, None] * r * k) % n).astype(jnp.float32)
    ang_lo = ((jnp.arange(r, dtype=jnp.int32)[:, None] * k) % n).astype(jnp.float32)
    w = 2.0 * math.pi / n
    ca, sa = jnp.cos(ang_hi * w)[:, None, :], jnp.sin(ang_hi * w)[:, None, :]
    cb, sb = jnp.cos(ang_lo * w)[None, :, :], jnp.sin(ang_lo * w)[None, :, :]
    scale = n ** -0.5
    cos = ((ca * cb - sa * sb) * scale).reshape(size, size)
    sin = ((sa * cb + ca * sb) * scale).reshape(size, size)
    return cos, sin


def _dft_tables():
    ch, sh = _cos_sin_products(SEQ, HALF)
    cc, sc = _cos_sin_products(CTX, CTX)
    r = jnp.arange(FT, dtype=jnp.int32)
    rev = ((r[:, None] + r[None, :]) == FT).astype(jnp.bfloat16)
    return tuple(t.astype(jnp.bfloat16) for t in (ch, sh, rev, cc, sc))


def _channel_dft():
    c, s = _cos_sin_products(GROUP_CH, GROUP_CH)
    return jnp.concatenate([c, -s], axis=1).astype(jnp.bfloat16)


def _rope_tables():
    half = HEAD_DIM // 2
    rows = SEQ // GRID_W
    row = jnp.repeat(jnp.arange(rows, dtype=jnp.float32), GRID_W)
    col = jnp.tile(jnp.arange(GRID_W, dtype=jnp.float32), rows)
    inv_freq = ROPE_THETA ** (-jnp.arange(0, half, 2, dtype=jnp.float32) / half)
    ang_row = row[:, None] * inv_freq
    ang_col = col[:, None] * inv_freq
    cos = jnp.concatenate([jnp.cos(ang_row)] * 2 + [jnp.cos(ang_col)] * 2, axis=1)
    sin = jnp.concatenate([-jnp.sin(ang_row), jnp.sin(ang_row),
                           -jnp.sin(ang_col), jnp.sin(ang_col)], axis=1)
    cos = jnp.concatenate([cos, jnp.ones((CTX, HEAD_DIM), jnp.float32)], axis=0)
    sin = jnp.concatenate([sin, jnp.zeros((CTX, HEAD_DIM), jnp.float32)], axis=0)
    order = _rope_order()
    return cos[:, order], sin[:, order]


def _rope_order():
    quarter = HEAD_DIM // 4
    blocks = [0, 2, 1, 3]
    return jnp.concatenate([jnp.arange(b * quarter, (b + 1) * quarter) for b in blocks])


def _permute_qk_heads(w_qkv, g_q, g_k):
    order = _rope_order()
    n_cols = w_qkv.shape[-1]
    n_qk = (N_HEADS + N_KV) * HEAD_DIM
    heads = jnp.arange(n_qk, dtype=jnp.int32) // HEAD_DIM * HEAD_DIM
    src = jnp.concatenate([heads + jnp.tile(order, N_HEADS + N_KV),
                           jnp.arange(n_qk, n_cols, dtype=jnp.int32)])
    select = (jnp.arange(n_cols, dtype=jnp.int32)[:, None] == src[None, :]).astype(jnp.bfloat16)
    w = jnp.einsum("lik,kj->lij", w_qkv.astype(jnp.bfloat16), select,
                   preferred_element_type=jnp.float32).astype(jnp.bfloat16)
    return w, g_q[:, order], g_k[:, order]


def kernel(x, c, ctx, c_ctx, w_mod, b_mod, g_ffn1, w_ffn1_gu, w_ffn1_down, g_mix, g_ffn2, w_ffn2_gu, w_ffn2_down, g_final, w_in_ab, g_v, w_s, b_s, w_out_ab, w_qkv, g_q, g_k, w_o):
    batch = x.shape[0]
    depth = w_mod.shape[0]
    assert x.shape == (batch, SEQ, D) and ctx.shape == (batch, CTX, D)
    assert depth % 2 == 0

    mod_rows = -(-(batch + 1) // SUBLANES) * SUBLANES
    cc = jnp.concatenate([c, c_ctx[None], jnp.zeros((mod_rows - batch - 1, D), jnp.float32)], axis=0)
    m = _modulation(cc, w_mod, b_mod)
    m = m[:, :batch + 1].reshape(depth, batch + 1, N_MOD, D)

    wgu1, wd1 = w_ffn1_gu.astype(jnp.bfloat16), w_ffn1_down.astype(jnp.bfloat16)
    wgu2, wd2 = w_ffn2_gu.astype(jnp.bfloat16), w_ffn2_down.astype(jnp.bfloat16)
    g1 = g_ffn1.reshape(depth, 1, D)
    g2 = g_ffn2.reshape(depth, 1, D)
    gm = g_mix.reshape(depth, 1, D)
    w_in = w_in_ab.astype(jnp.bfloat16)
    w_out = w_out_ab.astype(jnp.bfloat16)
    ws = w_s.astype(jnp.bfloat16)
    gv = g_v.reshape(-1, 1, D_SGU)
    bs_full = jnp.repeat(jnp.swapaxes(b_s, 1, 2), GROUP_CH, axis=2)
    wqkv, g_q_p, g_k_p = _permute_qk_heads(w_qkv, g_q, g_k)
    wo = w_o.astype(jnp.bfloat16)
    gq = g_q_p.reshape(-1, 1, HEAD_DIM)
    gk = g_k_p.reshape(-1, 1, HEAD_DIM)
    dft_tables = _dft_tables()
    ccs = _channel_dft()
    cos_t, sin_t = _rope_tables()

    xs = (x, ctx)
    for l in range(depth):
        last = l == depth - 1
        even = l % 2 == 0
        mods = m[l]
        n_out = N_LAT_TILES if last else N_TILES
        if even:
            xc, pq, sgu = _layer_in(xs, mods, g1, wgu1, wd1, gm, l, True,
                                    (w_in, ccs, gv, ws, bs_full))
            ys, w_proj = [_dft(dft_tables, pq), sgu], w_out
        else:
            xc, q, kt, v = _layer_in(xs, mods, g1, wgu1, wd1, gm, l, False,
                                     (wqkv, gq, gk, cos_t, sin_t))
            ys, w_proj = list(_attention_latent(q, kt, v)), wo
            if not last:
                ys.append(_attention_context(q, kt, v))
        xc = _layer_out(xc, mods, w_proj, ys, g2, wgu2, wd2, l, n_out,
                        g_final.reshape(1, D) if last else None,
                        split_y=not even)
        xs = (xc,)
    return xc
```

```python
import functools
import math

import jax
import jax.numpy as jnp
from jax import lax
from jax.experimental import pallas as pl
from jax.experimental.pallas import tpu as pltpu

D = 1024
SEQ = 4096
CTX = 256
ROWS = SEQ + CTX
GRID_W = 64
D_FF = 2752
N_MOD = 9
EPS = 1e-6
HEAD_DIM = 128
N_HEADS = 8
N_KV = 2
GROUP = N_HEADS // N_KV
KV_W = N_KV * HEAD_DIM
ROPE_THETA = 10000.0
D_FOURIER = 512
D_SGU = 512
GROUP_CH = 128
N_GROUPS = 4
CHUNK = 128

MXU_DIM = 256
TM = 256
N_LAT_TILES = SEQ // TM
N_TILES = ROWS // TM
V7X_VMEM_BYTES = 64 * 1024 * 1024
SUBLANES = 8
F32, BF16 = 4, 2

assert TM == CTX and SEQ % TM == 0


def _cparams(n_axes, resident=0, streamed=0, scratch=0, temporaries=0):
    need = resident + 2 * streamed + scratch + temporaries
    assert need <= V7X_VMEM_BYTES, need
    return pltpu.CompilerParams(dimension_semantics=("arbitrary",) * n_axes,
                                vmem_limit_bytes=need)


def _resident(block_shape, index_map):
    return pl.BlockSpec(block_shape, index_map, pipeline_mode=pl.Buffered(1))


def _modnorm(x, g, shift, scale):
    ms = jnp.mean(x * x, axis=-1, keepdims=True)
    return x * lax.rsqrt(ms + EPS) * (g * (1.0 + scale)) + shift


def _ffn_norm(x, mod_ref, row0, g):
    h = _modnorm(x, g, mod_ref[row0:row0 + 1, :], mod_ref[row0 + 1:row0 + 2, :])
    return h.astype(jnp.bfloat16)


def _per_tile(fn, x, mod_refs):
    return jnp.concatenate([fn(x[i * TM:(i + 1) * TM], m) for i, m in enumerate(mod_refs)],
                           axis=0)


def _swiglu(h, wgu_ref, wd_ref):
    gu = jnp.dot(h, wgu_ref[...], preferred_element_type=jnp.float32)
    gt, up = gu[:, :D_FF], gu[:, D_FF:]
    a = (gt * jax.nn.sigmoid(gt) * up).astype(jnp.bfloat16)
    return jnp.dot(a, wd_ref[...], preferred_element_type=jnp.float32)


def _ffn_half_step(x, mod_refs, row0, g, wgu_ref, wd_ref, h=None):
    if h is None:
        h = _per_tile(lambda t, m: _ffn_norm(t, m, row0, g), x, mod_refs)
    y = _swiglu(h, wgu_ref, wd_ref)
    return x + _per_tile(lambda t, m: 0.5 * m[row0 + 2:row0 + 3, :] * t, y, mod_refs)


def _mod_kernel(cc_ref, w_ref, b_ref, o_ref):
    s = cc_ref[...]
    s = s * jax.nn.sigmoid(s)
    s_hi = s.astype(jnp.bfloat16)
    s_lo = (s - s_hi.astype(jnp.float32)).astype(jnp.bfloat16)
    w = w_ref[...]
    w_hi = w.astype(jnp.bfloat16)
    w_lo = (w - w_hi.astype(jnp.float32)).astype(jnp.bfloat16)
    rows = s.shape[0]
    both = jnp.dot(jnp.concatenate([s_hi, s_lo], axis=0), w_hi,
                   preferred_element_type=jnp.float32)
    cross = jnp.dot(s_hi, w_lo, preferred_element_type=jnp.float32)
    o_ref[...] = both[:rows] + both[rows:] + cross + b_ref[...]


def _modulation(cc, w_mod, b_mod):
    depth = w_mod.shape[0]
    rows = cc.shape[0]
    tn = D
    return pl.pallas_call(
        _mod_kernel,
        grid=(depth, N_MOD * D // tn),
        in_specs=[
            pl.BlockSpec((rows, D), lambda l, j: (0, 0)),
            pl.BlockSpec((None, D, tn), lambda l, j: (l, 0, j)),
            pl.BlockSpec((None, 1, tn), lambda l, j: (l, 0, j)),
        ],
        out_specs=pl.BlockSpec((None, rows, tn), lambda l, j: (l, 0, j)),
        out_shape=jax.ShapeDtypeStruct((depth, rows, N_MOD * D), jnp.float32),
        compiler_params=_cparams(2, streamed=(D + 2 * rows) * tn * F32,
                                 temporaries=2 * D * tn * BF16 + 4 * rows * tn * F32),
        name="modulation",
    )(cc, w_mod, b_mod.reshape(depth, 1, N_MOD * D))


def _even_in_stage(p, ccs_ref, gv_ref, ws_ref, bs_ref, pq_ref, sgu_ref):
    a = p[:, :D_FOURIER].astype(jnp.bfloat16)
    uv = jax.nn.gelu(p[:, D_FOURIER:], approximate=True)
    u = uv[:, :D_SGU]
    v = uv[:, D_SGU:]
    ccs = ccs_ref[...]
    ps, qs, gated = [], [], []
    n_chunks = p.shape[0] // CHUNK
    for grp in range(N_GROUPS):
        lo, hi = grp * GROUP_CH, (grp + 1) * GROUP_CH
        t = jnp.dot(a[:, lo:hi], ccs, preferred_element_type=jnp.float32)
        ps.append(t[:, :GROUP_CH])
        qs.append(t[:, GROUP_CH:])
        vg = v[:, lo:hi]
        ms = jnp.mean(vg * vg, axis=-1, keepdims=True)
        vh = (vg * lax.rsqrt(ms + EPS) * gv_ref[:, lo:hi]).astype(jnp.bfloat16)
        rhs = jnp.concatenate([vh[c * CHUNK:(c + 1) * CHUNK, :] for c in range(n_chunks)], axis=1)
        mixed = jnp.dot(ws_ref[grp], rhs, preferred_element_type=jnp.float32)
        mixed = jnp.concatenate(
            [mixed[:, c * GROUP_CH:(c + 1) * GROUP_CH] for c in range(n_chunks)], axis=0)
        bias = jnp.concatenate([bs_ref[:, lo:hi]] * n_chunks, axis=0)
        gated.append(u[:, lo:hi] * (mixed + bias))
    pq_ref[...] = jnp.concatenate(ps + qs, axis=1).astype(jnp.bfloat16)
    sgu_ref[...] = jnp.concatenate(gated, axis=1).astype(jnp.bfloat16)


def _rope(t, cos, sin_signed):
    return t * cos + pltpu.roll(t, HEAD_DIM // 2, 1) * sin_signed


def _qkv_stage(qkv, gq_ref, gk_ref, cos, sin, q_ref, kt_ref, v_ref):
    q_scale = (HEAD_DIM ** -0.5) * math.log2(math.e)

    def head(col, gain):
        t = qkv[:, col:col + HEAD_DIM]
        ms = jnp.mean(t * t, axis=-1, keepdims=True)
        return _rope(t * lax.rsqrt(ms + EPS) * gain, cos, sin)

    qs = [head(hd * HEAD_DIM, gq_ref[...]) * q_scale for hd in range(N_HEADS)]
    q_ref[...] = jnp.concatenate(qs, axis=1).astype(jnp.bfloat16)
    for hd in range(N_KV):
        kt_ref[hd] = head((N_HEADS + hd) * HEAD_DIM, gk_ref[...]).T.astype(jnp.bfloat16)
    v_ref[...] = qkv[:, (N_HEADS + N_KV) * HEAD_DIM:].astype(jnp.bfloat16)


PAIR = 2 * TM
_STAGE_COLS = D_FOURIER + 2 * D_SGU
assert _STAGE_COLS == D + 2 * KV_W
_FFN_WEIGHT_BYTES = 3 * D * D_FF * BF16
_FFN_TEMP_BYTES = PAIR * (2 * D_FF * F32 + D_FF * BF16 + D * BF16 + 2 * D * F32)


def _layer_in_kernel(*refs, split_in, even, steps):
    refs = list(refs)
    h_ref = refs.pop()
    t = pl.program_id(0)

    @pl.when(t == 0)
    def _():
        h_ref[...] = jnp.zeros(h_ref.shape, h_ref.dtype)

    if split_in:
        tile0 = 2 * jnp.minimum(t, steps - 2)
        x = jnp.concatenate(
            [jnp.where((tile0 + i) % N_TILES == N_LAT_TILES, refs[2 * i + 1][...], refs[2 * i][...])
             for i in range(2)], axis=0)
        del refs[:4]
    else:
        x = refs.pop(0)[...]
    mod_refs = refs[:2]
    g1_ref, wgu_ref, wd_ref, gm_ref, w_stage_ref = refs[2:7]
    n_out = 3 if even else 4
    stage_in = refs[7:-n_out]
    xo_ref, *stage_out = refs[-n_out:]
    proj = jnp.dot(h_ref[...], w_stage_ref[...], preferred_element_type=jnp.float32)
    gm = gm_ref[...]
    if even:
        x = _ffn_half_step(x, mod_refs, 0, g1_ref[...], wgu_ref, wd_ref)
        xo_ref[...] = x
        h_ref[...] = _per_tile(lambda r, m: _ffn_norm(r, m, 3, gm), x, mod_refs)
        _even_in_stage(proj, *stage_in, *stage_out)
    else:
        gq_ref, gk_ref, cos0, sin0, cos1, sin1 = stage_in
        _qkv_stage(proj, gq_ref, gk_ref, jnp.concatenate([cos0[...], cos1[...]], axis=0),
                   jnp.concatenate([sin0[...], sin1[...]], axis=0), *stage_out)
        x = _ffn_half_step(x, mod_refs, 0, g1_ref[...], wgu_ref, wd_ref)
        xo_ref[...] = x
        h_ref[...] = _per_tile(lambda r, m: _ffn_norm(r, m, 3, gm), x, mod_refs)


def _layer_in(xs, mods, g1, wgu, wd, gm, layer, even, stage_args):
    batch = xs[0].shape[0]
    j = layer // 2
    split_in = len(xs) == 2
    n_rows = batch * ROWS
    assert (batch * N_TILES) % 2 == 0
    steps = batch * N_TILES // 2 + 1

    def cur(t):
        return jnp.minimum(t, steps - 2)

    def prev(t):
        return jnp.maximum(t - 1, 0)

    def tile(pair, i):
        n = 2 * pair + i
        return n // N_TILES, n % N_TILES

    def rows_spec(width, which):
        return pl.BlockSpec((PAIR, width), lambda t: (which(t), 0))

    def const(block_shape, *index):
        return _resident(block_shape, lambda t: index)

    def mod_spec(i):
        def index(t):
            b, r = tile(cur(t), i)
            return jnp.where(r == N_LAT_TILES, batch, b), 0, 0
        return pl.BlockSpec((None, N_MOD, D), index)

    if split_in:
        x_arrs, x_specs = [], []
        for i in range(2):
            def lat(t, i=i):
                b, r = tile(cur(t), i)
                return b, jnp.minimum(r, N_LAT_TILES - 1), 0
            x_arrs += list(xs)
            x_specs += [pl.BlockSpec((None, TM, D), lat),
                        pl.BlockSpec((None, CTX, D), lambda t, i=i: (tile(cur(t), i)[0], 0, 0))]
    else:
        x_arrs, x_specs = [xs[0].reshape(n_rows, D)], [rows_spec(D, cur)]
    in_specs = x_specs + [
        mod_spec(0), mod_spec(1),
        const((None, 1, D), layer, 0, 0),
        const((None, D, 2 * D_FF), layer, 0, 0),
        const((None, D_FF, D), layer, 0, 0),
        const((None, 1, D), layer, 0, 0),
    ]
    out_specs = [rows_spec(D, cur)]
    out_shape = [jax.ShapeDtypeStruct((n_rows, D), jnp.float32)]
    if even:
        in_specs += [
            const((None, D, D_FOURIER + 2 * D_SGU), j, 0, 0),
            const((GROUP_CH, 2 * GROUP_CH), 0, 0),
            const((None, 1, D_SGU), j, 0, 0),
            const((None, N_GROUPS, CHUNK, CHUNK), j, 0, 0, 0),
            const((None, CHUNK, D_SGU), j, 0, 0),
        ]
        out_specs += [rows_spec(2 * D_FOURIER, prev), rows_spec(D_SGU, prev)]
        out_shape += [jax.ShapeDtypeStruct((n_rows, 2 * D_FOURIER), jnp.bfloat16),
                      jax.ShapeDtypeStruct((n_rows, D_SGU), jnp.bfloat16)]
        stage_arrs = list(stage_args)
    else:
        wqkv, gq, gk, cos_t, sin_t = stage_args

        def rope_spec(i):
            return pl.BlockSpec((TM, HEAD_DIM), lambda t: (tile(prev(t), i)[1], 0))

        in_specs += [
            const((None, D, D + 2 * KV_W), j, 0, 0),
            const((None, 1, HEAD_DIM), j, 0, 0),
            const((None, 1, HEAD_DIM), j, 0, 0),
            rope_spec(0), rope_spec(0), rope_spec(1), rope_spec(1),
        ]
        stage_arrs = [wqkv, gq, gk, cos_t, sin_t, cos_t, sin_t]
        out_specs += [
            rows_spec(D, prev),
            pl.BlockSpec((N_KV, HEAD_DIM, PAIR), lambda t: (0, 0, prev(t))),
            rows_spec(KV_W, prev),
        ]
        out_shape += [jax.ShapeDtypeStruct((n_rows, D), jnp.bfloat16),
                      jax.ShapeDtypeStruct((N_KV, HEAD_DIM, n_rows), jnp.bfloat16),
                      jax.ShapeDtypeStruct((n_rows, KV_W), jnp.bfloat16)]
    outs = pl.pallas_call(
        functools.partial(_layer_in_kernel, split_in=split_in, even=even, steps=steps),
        grid=(steps,),
        in_specs=in_specs,
        out_specs=out_specs,
        out_shape=out_shape,
        scratch_shapes=[pltpu.VMEM((PAIR, D), jnp.bfloat16)],
        compiler_params=_cparams(
            1, resident=_FFN_WEIGHT_BYTES + D * _STAGE_COLS * BF16,
            streamed=PAIR * (2 * D * F32 + (_STAGE_COLS + (0 if even else KV_W)) * BF16),
            scratch=PAIR * D * BF16,
            temporaries=_FFN_TEMP_BYTES + 3 * PAIR * _STAGE_COLS * F32),
        name="layer_in_even" if even else "layer_in_odd",
    )(*x_arrs, mods, mods, g1, wgu, wd, gm, *stage_arrs)
    return [o if o.shape[0] == N_KV and o.ndim == 3 else o.reshape(batch, ROWS, o.shape[-1])
            for o in outs]


def _layer_out_kernel(x_ref, *refs, n_mod, n_y, split_y, final):
    mod_refs = refs[:n_mod] * (2 // n_mod)
    wo_ref = refs[n_mod]
    y_refs = refs[n_mod + 1:n_mod + 1 + n_y]
    g2_ref, wgu_ref, wd_ref = refs[n_mod + 1 + n_y:n_mod + 4 + n_y]
    o_ref = refs[-1]
    if split_y and n_y == 2:
        y = jnp.concatenate([r[...] for r in y_refs], axis=0)
    elif split_y:
        tile0 = 2 * pl.program_id(0)

        def pick(i):
            r = (tile0 + i) % N_TILES
            even, odd, ctx = (y_ref[...] for y_ref in y_refs[3 * i:3 * i + 3])
            return jnp.where(r == N_LAT_TILES, ctx, jnp.where(r % 2 == 1, odd, even))

        y = jnp.concatenate([pick(0), pick(1)], axis=0)
    else:
        y = jnp.concatenate([r[...] for r in y_refs], axis=1)
    wo = wo_ref[...]
    x = x_ref[...] + _per_tile(
        lambda t, m: m[5:6, :] * jnp.dot(t, wo, preferred_element_type=jnp.float32), y, mod_refs)
    x = _ffn_half_step(x, mod_refs, 6, g2_ref[...], wgu_ref, wd_ref)
    if final:
        gf_ref = refs[-2]
        ms = jnp.mean(x * x, axis=-1, keepdims=True)
        x = x * lax.rsqrt(ms + EPS) * gf_ref[...]
    o_ref[...] = x


def _layer_out(xc, mods, wo, ys, g2, wgu, wd, layer, n_tiles, g_final=None, split_y=False):
    batch = xc.shape[0]
    j = layer // 2
    final = g_final is not None

    def const(block_shape, *index):
        return _resident(block_shape, lambda *_: index)

    if n_tiles == N_LAT_TILES:
        grid = (batch, N_LAT_TILES // 2)
        x_arr, x_spec = xc, pl.BlockSpec((None, PAIR, D), lambda b, k: (b, k, 0))
        mod_specs = [pl.BlockSpec((None, N_MOD, D), lambda b, k: (b, 0, 0))]
        y_arrs = list(ys)
        y_rows = TM if split_y else PAIR
        y_specs = [pl.BlockSpec((None, y_rows, y.shape[-1]), lambda b, k: (b, k, 0)) for y in ys]
        out_spec = pl.BlockSpec((None, PAIR, D), lambda b, k: (b, k, 0))
        out_shape = jax.ShapeDtypeStruct((batch, SEQ, D), jnp.float32)
    else:
        assert (batch * N_TILES) % 2 == 0
        grid = (batch * N_TILES // 2,)
        x_arr, x_spec = xc.reshape(batch * ROWS, D), pl.BlockSpec((PAIR, D), lambda k: (k, 0))

        def tile(k, i):
            t = 2 * k + i
            return t // N_TILES, t % N_TILES

        def mod_spec(i):
            def index(k):
                b, r = tile(k, i)
                return jnp.where(r == N_LAT_TILES, batch, b), 0, 0
            return pl.BlockSpec((None, N_MOD, D), index)

        mod_specs = [mod_spec(0), mod_spec(1)]
        if split_y:
            y_arrs, y_specs = [], []
            for i in range(2):
                def lat(k, i=i):
                    b, r = tile(k, i)
                    return b, jnp.minimum(r, N_LAT_TILES - 1) // 2, 0
                y_arrs += list(ys)
                y_specs += [pl.BlockSpec((None, TM, D), lat), pl.BlockSpec((None, TM, D), lat),
                            pl.BlockSpec((None, CTX, D), lambda k, i=i: (tile(k, i)[0], 0, 0))]
        else:
            y_arrs = [y.reshape(batch * ROWS, y.shape[-1]) for y in ys]
            y_specs = [pl.BlockSpec((PAIR, y.shape[-1]), lambda k: (k, 0)) for y in y_arrs]
        out_spec = pl.BlockSpec((PAIR, D), lambda k: (k, 0))
        out_shape = jax.ShapeDtypeStruct((batch * ROWS, D), jnp.float32)
    in_specs = [x_spec] + mod_specs + [const((None, D, D), j, 0, 0)] + y_specs + [
        const((None, 1, D), layer, 0, 0),
        const((None, D, 2 * D_FF), layer, 0, 0),
        const((None, D_FF, D), layer, 0, 0),
    ]
    args = [x_arr] + [mods] * len(mod_specs) + [wo] + y_arrs + [g2, wgu, wd]
    if final:
        in_specs.append(const((1, D), 0, 0))
        args.append(g_final)
    out = pl.pallas_call(
        functools.partial(_layer_out_kernel, n_mod=len(mod_specs), n_y=len(y_arrs),
                          split_y=split_y, final=final),
        grid=grid,
        in_specs=in_specs,
        out_specs=out_spec,
        out_shape=out_shape,
        compiler_params=_cparams(
            len(grid), resident=_FFN_WEIGHT_BYTES + D * D * BF16,
            streamed=PAIR * (2 * D * F32 + D * BF16),
            temporaries=_FFN_TEMP_BYTES + 2 * PAIR * D * F32),
        name="layer_out_final" if final else "layer_out",
    )(*args)
    return out.reshape(batch, -1, D)


HALF = SEQ // 2
FT = MXU_DIM
FW = MXU_DIM
N_FOLD_TILES = HALF // FT
DFT_SCALE = SEQ ** -0.5


def _dft_kernel(ch_ref, sh_ref, rev_ref, cc_ref, sc_ref, p_ref, q_ref, o_ref):
    rev = rev_ref[...]
    row0 = lax.broadcasted_iota(jnp.int32, (FT, FW), 0) == 0
    alt = (1 - 2 * (lax.broadcasted_iota(jnp.int32, (FT, FW), 0) & 1)).astype(jnp.float32)

    def reversed_upper(x_ref, t):
        src = HALF + FT * (N_FOLD_TILES - 1 - t)
        r = jnp.dot(rev, x_ref[src:src + FT, :], preferred_element_type=jnp.float32)
        if t == 0:
            return r
        first = HALF + FT * (N_FOLD_TILES - t)
        return jnp.where(row0, x_ref[first:first + 1, :].astype(jnp.float32), r)

    pfs, qfs, alt_sum = [], [], None
    for t in range(N_FOLD_TILES):
        rows = slice(t * FT, (t + 1) * FT)
        pf = p_ref[rows, :].astype(jnp.float32) + reversed_upper(p_ref, t)
        qf = q_ref[rows, :].astype(jnp.float32) - reversed_upper(q_ref, t)
        alt_sum = pf if alt_sum is None else alt_sum + pf
        pfs.append(pf.astype(jnp.bfloat16))
        qfs.append(qf.astype(jnp.bfloat16))
    p_mid = p_ref[HALF:HALF + 1, :].astype(jnp.float32)
    sign = jnp.concatenate([alt] * N_FOLD_TILES, axis=0)
    e = jnp.dot(ch_ref[...], jnp.concatenate(pfs, axis=0), preferred_element_type=jnp.float32)
    e = e + sign * (DFT_SCALE * p_mid)
    o = jnp.dot(sh_ref[...], jnp.concatenate(qfs, axis=0), preferred_element_type=jnp.float32)
    o_ref[0:HALF, :] = (e + o).astype(jnp.bfloat16)
    g = (e - o).astype(jnp.bfloat16)
    mid = DFT_SCALE * (jnp.sum(alt_sum * alt, axis=0, keepdims=True) + p_mid)
    for t in range(N_FOLD_TILES):
        src = FT * (N_FOLD_TILES - 1 - t)
        up = jnp.dot(rev, g[src:src + FT, :], preferred_element_type=jnp.float32)
        first = mid if t == 0 else g[src + FT:src + FT + 1, :].astype(jnp.float32)
        o_ref[HALF + t * FT:HALF + (t + 1) * FT, :] = jnp.where(row0, first, up).astype(jnp.bfloat16)
    ctx = jnp.dot(cc_ref[...], p_ref[SEQ:, :], preferred_element_type=jnp.float32)
    ctx = ctx + jnp.dot(sc_ref[...], q_ref[SEQ:, :], preferred_element_type=jnp.float32)
    o_ref[SEQ:, :] = ctx.astype(jnp.bfloat16)


def _dft(tables, pq):
    ch, sh, rev, cc, sc = tables
    batch = pq.shape[0]
    n_col = D_FOURIER // FW
    return pl.pallas_call(
        _dft_kernel,
        grid=(batch, n_col),
        in_specs=[
            _resident((HALF, HALF), lambda b, c: (0, 0)),
            _resident((HALF, HALF), lambda b, c: (0, 0)),
            _resident((FT, FT), lambda b, c: (0, 0)),
            _resident((CTX, CTX), lambda b, c: (0, 0)),
            _resident((CTX, CTX), lambda b, c: (0, 0)),
            pl.BlockSpec((None, ROWS, FW), lambda b, c: (b, 0, c)),
            pl.BlockSpec((None, ROWS, FW), lambda b, c: (b, 0, n_col + c)),
        ],
        out_specs=pl.BlockSpec((None, ROWS, FW), lambda b, c: (b, 0, c)),
        out_shape=jax.ShapeDtypeStruct((batch, ROWS, D_FOURIER), jnp.bfloat16),
        compiler_params=_cparams(
            2, resident=2 * HALF * HALF * BF16, streamed=3 * ROWS * FW * BF16,
            temporaries=HALF * FW * (2 * BF16 + 3 * F32) + 2 * ROWS * FW * F32),
        name="dft",
    )(ch, sh, rev, cc, sc, pq, pq)


KEY_TILE = MXU_DIM
N_KEY_TILES = ROWS // KEY_TILE


def _with_row_sums(v):
    return jnp.concatenate([v, jnp.ones_like(v)], axis=1)


def _normalised(pv):
    return (pv[:, :HEAD_DIM] / pv[:, HEAD_DIM:]).astype(jnp.bfloat16)


def _attn_kernel(q_ref, kt_ref, v_prev_ref, v_ref, o_prev_ref, o_ref, s_ref, m_ref):
    @pl.when(pl.program_id(0) == 0)
    def _():
        s_ref[...] = jnp.zeros(s_ref.shape, s_ref.dtype)
        m_ref[...] = jnp.zeros(m_ref.shape, m_ref.dtype)

    for half, (vals_ref, out_ref) in enumerate(((v_prev_ref, o_prev_ref), (v_ref, o_ref))):
        outs = []
        for hd in range(GROUP):
            q = q_ref[half * TM:(half + 1) * TM, hd * HEAD_DIM:(hd + 1) * HEAD_DIM]
            m_prev = m_ref[hd]
            m_lane = None
            acc = None
            for j in range(N_KEY_TILES):
                keys = slice(j * KEY_TILE, (j + 1) * KEY_TILE)
                s_old = s_ref[hd, :, keys]
                p = jnp.exp2(s_old - jnp.concatenate([m_prev] * (KEY_TILE // HEAD_DIM), axis=1))
                pv = jnp.dot(p.astype(jnp.bfloat16), _with_row_sums(vals_ref[keys, :]),
                             preferred_element_type=jnp.float32)
                acc = pv if acc is None else acc + pv
                s_new = jnp.dot(q, kt_ref[:, keys], preferred_element_type=jnp.float32)
                s_ref[hd, :, keys] = s_new
                for c in range(KEY_TILE // HEAD_DIM):
                    part = s_new[:, c * HEAD_DIM:(c + 1) * HEAD_DIM]
                    m_lane = part if m_lane is None else jnp.maximum(m_lane, part)
            outs.append(_normalised(acc))
            m_ref[hd] = jnp.broadcast_to(jnp.max(m_lane, axis=-1, keepdims=True), m_lane.shape)
        out_ref[...] = jnp.concatenate(outs, axis=1)


def _attention_latent(q, kt, v):
    batch = q.shape[0]
    gw = GROUP * HEAD_DIM
    n_pairs = N_LAT_TILES // 2
    per_sample = N_KV * n_pairs
    steps = batch * per_sample + 1

    def unflatten(c):
        return c // per_sample, (c % per_sample) // n_pairs, c % n_pairs

    def cur(t):
        return unflatten(jnp.minimum(t, steps - 2))

    def prev(t):
        return unflatten(jnp.maximum(t - 1, 0))

    def tile_map(which):
        def index(t):
            b, g, i = which(t)
            return b, i, g
        return index

    def kt_map(t):
        b, g, _ = cur(t)
        return g, 0, b

    def v_map(which):
        def index(t):
            b, g, _ = which(t)
            return b, 0, g
        return index

    half_shape = jax.ShapeDtypeStruct((batch, SEQ // 2, D), jnp.bfloat16)
    odd, even = pl.pallas_call(
        _attn_kernel,
        grid=(steps,),
        in_specs=[
            pl.BlockSpec((None, PAIR, gw), tile_map(cur)),
            pl.BlockSpec((None, HEAD_DIM, ROWS), kt_map),
            pl.BlockSpec((None, ROWS, HEAD_DIM), v_map(prev)),
            pl.BlockSpec((None, ROWS, HEAD_DIM), v_map(cur)),
        ],
        out_specs=[pl.BlockSpec((None, TM, gw), tile_map(prev)),
                   pl.BlockSpec((None, TM, gw), tile_map(cur))],
        out_shape=[half_shape, half_shape],
        scratch_shapes=[pltpu.VMEM((GROUP, TM, ROWS), jnp.float32),
                        pltpu.VMEM((GROUP, TM, HEAD_DIM), jnp.float32)],
        compiler_params=_cparams(
            1, streamed=(3 * TM * gw + 3 * ROWS * HEAD_DIM) * BF16 + TM * gw * BF16,
            scratch=GROUP * TM * (ROWS + HEAD_DIM) * F32,
            temporaries=2 * GROUP * TM * 2 * HEAD_DIM * F32 + 4 * TM * KEY_TILE * F32),
        name="attention",
    )(q, kt, v, v)
    return even, odd


def _attn_ctx_kernel(q_ref, kt_ref, v_ref, o_ref):
    v1 = _with_row_sums(v_ref[...])
    kt = kt_ref[...]
    outs = []
    for hd in range(GROUP):
        s = jnp.dot(q_ref[:, hd * HEAD_DIM:(hd + 1) * HEAD_DIM], kt,
                    preferred_element_type=jnp.float32)
        p = jnp.exp2(s - jnp.max(s, axis=-1, keepdims=True)).astype(jnp.bfloat16)
        outs.append(_normalised(jnp.dot(p, v1, preferred_element_type=jnp.float32)))
    o_ref[...] = jnp.concatenate(outs, axis=1)


def _attention_context(q, kt, v):
    batch = q.shape[0]
    gw = GROUP * HEAD_DIM
    return pl.pallas_call(
        _attn_ctx_kernel,
        grid=(batch, N_KV),
        in_specs=[
            pl.BlockSpec((None, CTX, gw), lambda b, g: (b, SEQ // CTX, g)),
            pl.BlockSpec((None, HEAD_DIM, CTX), lambda b, g: (g, 0, b * N_TILES + N_LAT_TILES)),
            pl.BlockSpec((None, CTX, HEAD_DIM), lambda b, g: (b, SEQ // CTX, g)),
        ],
        out_specs=pl.BlockSpec((None, CTX, gw), lambda b, g: (b, 0, g)),
        out_shape=jax.ShapeDtypeStruct((batch, CTX, D), jnp.bfloat16),
        compiler_params=_cparams(
            2, streamed=2 * CTX * (gw + HEAD_DIM) * BF16,
            temporaries=GROUP * CTX * (2 * CTX + 2 * HEAD_DIM) * F32),
        name="attention_ctx",
    )(q, kt, v)


def _cos_sin_products(n, size):
    r = 1 << (int(math.log2(size)) // 2)
    hi = size // r
    assert r % SUBLANES == 0
    k = jnp.arange(size, dtype=jnp.int32)[None, :]
    ang_hi = ((jnp.arange(hi, dtype=jnp.int32)[:, None] * r * k) % n).astype(jnp.float32)
    ang_lo = ((jnp.arange(r, dtype=jnp.int32)[:, None] * k) % n).astype(jnp.float32)
    w = 2.0 * math.pi / n
    ca, sa = jnp.cos(ang_hi * w)[:, None, :], jnp.sin(ang_hi * w)[:, None, :]
    cb, sb = jnp.cos(ang_lo * w)[None, :, :], jnp.sin(ang_lo * w)[None, :, :]
    scale = n ** -0.5
    cos = ((ca * cb - sa * sb) * scale).reshape(size, size)
    sin = ((sa * cb + ca * sb) * scale).reshape(size, size)
    return cos, sin


def _dft_tables():
    ch, sh = _cos_sin_products(SEQ, HALF)
    cc, sc = _cos_sin_products(CTX, CTX)
    r = jnp.arange(FT, dtype=jnp.int32)
    rev = ((r[:, None] + r[None, :]) == FT).astype(jnp.bfloat16)
    return tuple(t.astype(jnp.bfloat16) for t in (ch, sh, rev, cc, sc))


def _channel_dft():
    c, s = _cos_sin_products(GROUP_CH, GROUP_CH)
    return jnp.concatenate([c, -s], axis=1).astype(jnp.bfloat16)


def _rope_tables():
    half = HEAD_DIM // 2
    rows = SEQ // GRID_W
    row = jnp.repeat(jnp.arange(rows, dtype=jnp.float32), GRID_W)
    col = jnp.tile(jnp.arange(GRID_W, dtype=jnp.float32), rows)
    inv_freq = ROPE_THETA ** (-jnp.arange(0, half, 2, dtype=jnp.float32) / half)
    ang_row = row[:, None] * inv_freq
    ang_col = col[:, None] * inv_freq
    cos = jnp.concatenate([jnp.cos(ang_row)] * 2 + [jnp.cos(ang_col)] * 2, axis=1)
    sin = jnp.concatenate([-jnp.sin(ang_row), jnp.sin(ang_row),
                           -jnp.sin(ang_col), jnp.sin(ang_col)], axis=1)
    cos = jnp.concatenate([cos, jnp.ones((CTX, HEAD_DIM), jnp.float32)], axis=0)
    sin = jnp.concatenate([sin, jnp.zeros((CTX, HEAD_DIM), jnp.float32)], axis=0)
    order = _rope_order()
    return cos[:, order], sin[:, order]


def _rope_order():
    quarter = HEAD_DIM // 4
    blocks = [0, 2, 1, 3]
    return jnp.concatenate([jnp.arange(b * quarter, (b + 1) * quarter) for b in blocks])


def _permute_qk_heads(w_qkv, g_q, g_k):
    order = _rope_order()
    n_cols = w_qkv.shape[-1]
    n_qk = (N_HEADS + N_KV) * HEAD_DIM
    heads = jnp.arange(n_qk, dtype=jnp.int32) // HEAD_DIM * HEAD_DIM
    src = jnp.concatenate([heads + jnp.tile(order, N_HEADS + N_KV),
                           jnp.arange(n_qk, n_cols, dtype=jnp.int32)])
    select = (jnp.arange(n_cols, dtype=jnp.int32)[:, None] == src[None, :]).astype(jnp.bfloat16)
    w = jnp.einsum("lik,kj->lij", w_qkv.astype(jnp.bfloat16), select,
                   preferred_element_type=jnp.float32).astype(jnp.bfloat16)
    return w, g_q[:, order], g_k[:, order]


def kernel(x, c, ctx, c_ctx, w_mod, b_mod, g_ffn1, w_ffn1_gu, w_ffn1_down, g_mix, g_ffn2, w_ffn2_gu, w_ffn2_down, g_final, w_in_ab, g_v, w_s, b_s, w_out_ab, w_qkv, g_q, g_k, w_o):
    batch = x.shape[0]
    depth = w_mod.shape[0]
    assert x.shape == (batch, SEQ, D) and ctx.shape == (batch, CTX, D)
    assert depth % 2 == 0

    mod_rows = -(-(batch + 1) // SUBLANES) * SUBLANES
    cc = jnp.concatenate([c, c_ctx[None], jnp.zeros((mod_rows - batch - 1, D), jnp.float32)], axis=0)
    m = _modulation(cc, w_mod, b_mod)
    m = m[:, :batch + 1].reshape(depth, batch + 1, N_MOD, D)

    wgu1, wd1 = w_ffn1_gu.astype(jnp.bfloat16), w_ffn1_down.astype(jnp.bfloat16)
    wgu2, wd2 = w_ffn2_gu.astype(jnp.bfloat16), w_ffn2_down.astype(jnp.bfloat16)
    g1 = g_ffn1.reshape(depth, 1, D)
    g2 = g_ffn2.reshape(depth, 1, D)
    gm = g_mix.reshape(depth, 1, D)
    w_in = w_in_ab.astype(jnp.bfloat16)
    w_out = w_out_ab.astype(jnp.bfloat16)
    ws = w_s.astype(jnp.bfloat16)
    gv = g_v.reshape(-1, 1, D_SGU)
    bs_full = jnp.repeat(jnp.swapaxes(b_s, 1, 2), GROUP_CH, axis=2)
    wqkv, g_q_p, g_k_p = _permute_qk_heads(w_qkv, g_q, g_k)
    wo = w_o.astype(jnp.bfloat16)
    gq = g_q_p.reshape(-1, 1, HEAD_DIM)
    gk = g_k_p.reshape(-1, 1, HEAD_DIM)
    dft_tables = _dft_tables()
    ccs = _channel_dft()
    cos_t, sin_t = _rope_tables()

    xs = (x, ctx)
    for l in range(depth):
        last = l == depth - 1
        even = l % 2 == 0
        mods = m[l]
        n_out = N_LAT_TILES if last else N_TILES
        if even:
            xc, pq, sgu = _layer_in(xs, mods, g1, wgu1, wd1, gm, l, True,
                                    (w_in, ccs, gv, ws, bs_full))
            ys, w_proj = [_dft(dft_tables, pq), sgu], w_out
        else:
            xc, q, kt, v = _layer_in(xs, mods, g1, wgu1, wd1, gm, l, False,
                                     (wqkv, gq, gk, cos_t, sin_t))
            ys, w_proj = list(_attention_latent(q, kt, v)), wo
            if not last:
                ys.append(_attention_context(q, kt, v))
        xc = _layer_out(xc, mods, w_proj, ys, g2, wgu2, wd2, l, n_out,
                        g_final.reshape(1, D) if last else None,
                        split_y=not even)
        xs = (xc,)
    return xc
```

```python
import functools
import math

import jax
import jax.numpy as jnp
from jax import lax
from jax.experimental import pallas as pl
from jax.experimental.pallas import tpu as pltpu

D = 1024
SEQ = 4096
CTX = 256
ROWS = SEQ + CTX
GRID_W = 64
D_FF = 2752
N_MOD = 9
EPS = 1e-6
HEAD_DIM = 128
N_HEADS = 8
N_KV = 2
GROUP = N_HEADS // N_KV
KV_W = N_KV * HEAD_DIM
ROPE_THETA = 10000.0
D_FOURIER = 512
D_SGU = 512
GROUP_CH = 128
N_GROUPS = 4
CHUNK = 128

MXU_DIM = 256
TM = 256
N_LAT_TILES = SEQ // TM
N_TILES = ROWS // TM
V7X_VMEM_BYTES = 64 * 1024 * 1024
SUBLANES = 8
F32, BF16 = 4, 2

assert TM == CTX and SEQ % TM == 0


def _cparams(n_axes, resident=0, streamed=0, scratch=0, temporaries=0):
    need = resident + 2 * streamed + scratch + temporaries
    assert need <= V7X_VMEM_BYTES, need
    return pltpu.CompilerParams(dimension_semantics=("arbitrary",) * n_axes,
                                vmem_limit_bytes=need)


def _resident(block_shape, index_map):
    return pl.BlockSpec(block_shape, index_map, pipeline_mode=pl.Buffered(1))


def _modnorm(x, g, shift, scale):
    ms = jnp.mean(x * x, axis=-1, keepdims=True)
    return x * lax.rsqrt(ms + EPS) * (g * (1.0 + scale)) + shift


def _ffn_norm(x, mod_ref, row0, g):
    h = _modnorm(x, g, mod_ref[row0:row0 + 1, :], mod_ref[row0 + 1:row0 + 2, :])
    return h.astype(jnp.bfloat16)


def _per_tile(fn, x, mod_refs):
    return jnp.concatenate([fn(x[i * TM:(i + 1) * TM], m) for i, m in enumerate(mod_refs)],
                           axis=0)


def _swiglu(h, wgu_ref, wd_ref):
    gu = jnp.dot(h, wgu_ref[...], preferred_element_type=jnp.float32)
    gt, up = gu[:, :D_FF], gu[:, D_FF:]
    a = (gt * jax.nn.sigmoid(gt) * up).astype(jnp.bfloat16)
    return jnp.dot(a, wd_ref[...], preferred_element_type=jnp.float32)


def _ffn_half_step(x, mod_refs, row0, g, wgu_ref, wd_ref, h=None):
    if h is None:
        h = _per_tile(lambda t, m: _ffn_norm(t, m, row0, g), x, mod_refs)
    y = _swiglu(h, wgu_ref, wd_ref)
    return x + _per_tile(lambda t, m: 0.5 * m[row0 + 2:row0 + 3, :] * t, y, mod_refs)


def _mod_kernel(cc_ref, w_ref, b_ref, o_ref):
    s = cc_ref[...]
    s = s * jax.nn.sigmoid(s)
    s_hi = s.astype(jnp.bfloat16)
    s_lo = (s - s_hi.astype(jnp.float32)).astype(jnp.bfloat16)
    w = w_ref[...]
    w_hi = w.astype(jnp.bfloat16)
    w_lo = (w - w_hi.astype(jnp.float32)).astype(jnp.bfloat16)
    rows = s.shape[0]
    both = jnp.dot(jnp.concatenate([s_hi, s_lo], axis=0), w_hi,
                   preferred_element_type=jnp.float32)
    cross = jnp.dot(s_hi, w_lo, preferred_element_type=jnp.float32)
    o_ref[...] = both[:rows] + both[rows:] + cross + b_ref[...]


def _modulation(cc, w_mod, b_mod):
    depth = w_mod.shape[0]
    rows = cc.shape[0]
    tn = D
    return pl.pallas_call(
        _mod_kernel,
        grid=(depth, N_MOD * D // tn),
        in_specs=[
            pl.BlockSpec((rows, D), lambda l, j: (0, 0)),
            pl.BlockSpec((None, D, tn), lambda l, j: (l, 0, j)),
            pl.BlockSpec((None, 1, tn), lambda l, j: (l, 0, j)),
        ],
        out_specs=pl.BlockSpec((None, rows, tn), lambda l, j: (l, 0, j)),
        out_shape=jax.ShapeDtypeStruct((depth, rows, N_MOD * D), jnp.float32),
        compiler_params=_cparams(2, streamed=(D + 2 * rows) * tn * F32,
                                 temporaries=2 * D * tn * BF16 + 4 * rows * tn * F32),
        name="modulation",
    )(cc, w_mod, b_mod.reshape(depth, 1, N_MOD * D))


def _even_in_stage(p, ccs_ref, gv_ref, ws_ref, bs_ref, pq_ref, sgu_ref):
    a = p[:, :D_FOURIER].astype(jnp.bfloat16)
    uv = jax.nn.gelu(p[:, D_FOURIER:], approximate=True)
    u = uv[:, :D_SGU]
    v = uv[:, D_SGU:]
    ccs = ccs_ref[...]
    ps, qs, gated = [], [], []
    n_chunks = p.shape[0] // CHUNK
    for grp in range(N_GROUPS):
        lo, hi = grp * GROUP_CH, (grp + 1) * GROUP_CH
        t = jnp.dot(a[:, lo:hi], ccs, preferred_element_type=jnp.float32)
        ps.append(t[:, :GROUP_CH])
        qs.append(t[:, GROUP_CH:])
        vg = v[:, lo:hi]
        ms = jnp.mean(vg * vg, axis=-1, keepdims=True)
        vh = (vg * lax.rsqrt(ms + EPS) * gv_ref[:, lo:hi]).astype(jnp.bfloat16)
        rhs = jnp.concatenate([vh[c * CHUNK:(c + 1) * CHUNK, :] for c in range(n_chunks)], axis=1)
        mixed = jnp.dot(ws_ref[grp], rhs, preferred_element_type=jnp.float32)
        mixed = jnp.concatenate(
            [mixed[:, c * GROUP_CH:(c + 1) * GROUP_CH] for c in range(n_chunks)], axis=0)
        bias = jnp.concatenate([bs_ref[:, lo:hi]] * n_chunks, axis=0)
        gated.append(u[:, lo:hi] * (mixed + bias))
    pq_ref[...] = jnp.concatenate(ps + qs, axis=1).astype(jnp.bfloat16)
    sgu_ref[...] = jnp.concatenate(gated, axis=1).astype(jnp.bfloat16)


def _rope(t, cos, sin_signed):
    return t * cos + pltpu.roll(t, HEAD_DIM // 2, 1) * sin_signed


def _qkv_stage(qkv, gq_ref, gk_ref, cos, sin, q_ref, kt_ref, v_ref):
    q_scale = (HEAD_DIM ** -0.5) * math.log2(math.e)

    def head(col, gain):
        t = qkv[:, col:col + HEAD_DIM]
        ms = jnp.mean(t * t, axis=-1, keepdims=True)
        return _rope(t * lax.rsqrt(ms + EPS) * gain, cos, sin)

    qs = [head(hd * HEAD_DIM, gq_ref[...]) * q_scale for hd in range(N_HEADS)]
    q_ref[...] = jnp.concatenate(qs, axis=1).astype(jnp.bfloat16)
    for hd in range(N_KV):
        kt_ref[hd] = head((N_HEADS + hd) * HEAD_DIM, gk_ref[...]).T.astype(jnp.bfloat16)
    v_ref[...] = qkv[:, (N_HEADS + N_KV) * HEAD_DIM:].astype(jnp.bfloat16)


PAIR = 2 * TM
_STAGE_COLS = D_FOURIER + 2 * D_SGU
assert _STAGE_COLS == D + 2 * KV_W
_FFN_WEIGHT_BYTES = 3 * D * D_FF * BF16
_FFN_TEMP_BYTES = PAIR * (2 * D_FF * F32 + D_FF * BF16 + D * BF16 + 2 * D * F32)


def _layer_in_kernel(*refs, split_in, even, steps):
    refs = list(refs)
    h_ref = refs.pop()
    t = pl.program_id(0)

    @pl.when(t == 0)
    def _():
        h_ref[...] = jnp.zeros(h_ref.shape, h_ref.dtype)

    if split_in:
        tile0 = 2 * jnp.minimum(t, steps - 2)
        x = jnp.concatenate(
            [jnp.where((tile0 + i) % N_TILES == N_LAT_TILES, refs[2 * i + 1][...], refs[2 * i][...])
             for i in range(2)], axis=0)
        del refs[:4]
    else:
        x = refs.pop(0)[...]
    mod_refs = refs[:2]
    g1_ref, wgu_ref, wd_ref, gm_ref, w_stage_ref = refs[2:7]
    n_out = 3 if even else 4
    stage_in = refs[7:-n_out]
    xo_ref, *stage_out = refs[-n_out:]
    proj = jnp.dot(h_ref[...], w_stage_ref[...], preferred_element_type=jnp.float32)
    gm = gm_ref[...]
    if even:
        x = _ffn_half_step(x, mod_refs, 0, g1_ref[...], wgu_ref, wd_ref)
        xo_ref[...] = x
        h_ref[...] = _per_tile(lambda r, m: _ffn_norm(r, m, 3, gm), x, mod_refs)
        _even_in_stage(proj, *stage_in, *stage_out)
    else:
        gq_ref, gk_ref, cos0, sin0, cos1, sin1 = stage_in
        _qkv_stage(proj, gq_ref, gk_ref, jnp.concatenate([cos0[...], cos1[...]], axis=0),
                   jnp.concatenate([sin0[...], sin1[...]], axis=0), *stage_out)
        x = _ffn_half_step(x, mod_refs, 0, g1_ref[...], wgu_ref, wd_ref)
        xo_ref[...] = x
        h_ref[...] = _per_tile(lambda r, m: _ffn_norm(r, m, 3, gm), x, mod_refs)


def _layer_in(xs, mods, g1, wgu, wd, gm, layer, even, stage_args):
    batch = xs[0].shape[0]
    j = layer // 2
    split_in = len(xs) == 2
    n_rows = batch * ROWS
    assert (batch * N_TILES) % 2 == 0
    steps = batch * N_TILES // 2 + 1

    def cur(t):
        return jnp.minimum(t, steps - 2)

    def prev(t):
        return jnp.maximum(t - 1, 0)

    def tile(pair, i):
        n = 2 * pair + i
        return n // N_TILES, n % N_TILES

    def rows_spec(width, which):
        return pl.BlockSpec((PAIR, width), lambda t: (which(t), 0))

    def const(block_shape, *index):
        return _resident(block_shape, lambda t: index)

    def mod_spec(i):
        def index(t):
            b, r = tile(cur(t), i)
            return jnp.where(r == N_LAT_TILES, batch, b), 0, 0
        return pl.BlockSpec((None, N_MOD, D), index)

    if split_in:
        x_arrs, x_specs = [], []
        for i in range(2):
            def lat(t, i=i):
                b, r = tile(cur(t), i)
                return b, jnp.minimum(r, N_LAT_TILES - 1), 0
            x_arrs += list(xs)
            x_specs += [pl.BlockSpec((None, TM, D), lat),
                        pl.BlockSpec((None, CTX, D), lambda t, i=i: (tile(cur(t), i)[0], 0, 0))]
    else:
        x_arrs, x_specs = [xs[0].reshape(n_rows, D)], [rows_spec(D, cur)]
    in_specs = x_specs + [
        mod_spec(0), mod_spec(1),
        const((None, 1, D), layer, 0, 0),
        const((None, D, 2 * D_FF), layer, 0, 0),
        const((None, D_FF, D), layer, 0, 0),
        const((None, 1, D), layer, 0, 0),
    ]
    out_specs = [rows_spec(D, cur)]
    out_shape = [jax.ShapeDtypeStruct((n_rows, D), jnp.float32)]
    if even:
        in_specs += [
            const((None, D, D_FOURIER + 2 * D_SGU), j, 0, 0),
            const((GROUP_CH, 2 * GROUP_CH), 0, 0),
            const((None, 1, D_SGU), j, 0, 0),
            const((None, N_GROUPS, CHUNK, CHUNK), j, 0, 0, 0),
            const((None, CHUNK, D_SGU), j, 0, 0),
        ]
        out_specs += [rows_spec(2 * D_FOURIER, prev), rows_spec(D_SGU, prev)]
        out_shape += [jax.ShapeDtypeStruct((n_rows, 2 * D_FOURIER), jnp.bfloat16),
                      jax.ShapeDtypeStruct((n_rows, D_SGU), jnp.bfloat16)]
        stage_arrs = list(stage_args)
    else:
        wqkv, gq, gk, cos_t, sin_t = stage_args

        def rope_spec(i):
            return pl.BlockSpec((TM, HEAD_DIM), lambda t: (tile(prev(t), i)[1], 0))

        in_specs += [
            const((None, D, D + 2 * KV_W), j, 0, 0),
            const((None, 1, HEAD_DIM), j, 0, 0),
            const((None, 1, HEAD_DIM), j, 0, 0),
            rope_spec(0), rope_spec(0), rope_spec(1), rope_spec(1),
        ]
        stage_arrs = [wqkv, gq, gk, cos_t, sin_t, cos_t, sin_t]
        out_specs += [
            rows_spec(D, prev),
            pl.BlockSpec((N_KV, HEAD_DIM, PAIR), lambda t: (0, 0, prev(t))),
            rows_spec(KV_W, prev),
        ]
        out_shape += [jax.ShapeDtypeStruct((n_rows, D), jnp.bfloat16),
                      jax.ShapeDtypeStruct((N_KV, HEAD_DIM, n_rows), jnp.bfloat16),
                      jax.ShapeDtypeStruct((n_rows, KV_W), jnp.bfloat16)]
    outs = pl.pallas_call(
        functools.partial(_layer_in_kernel, split_in=split_in, even=even, steps=steps),
        grid=(steps,),
        in_specs=in_specs,
        out_specs=out_specs,
        out_shape=out_shape,
        scratch_shapes=[pltpu.VMEM((PAIR, D), jnp.bfloat16)],
        compiler_params=_cparams(
            1, resident=_FFN_WEIGHT_BYTES + D * _STAGE_COLS * BF16,
            streamed=PAIR * (2 * D * F32 + (_STAGE_COLS + (0 if even else KV_W)) * BF16),
            scratch=PAIR * D * BF16,
            temporaries=_FFN_TEMP_BYTES + 3 * PAIR * _STAGE_COLS * F32),
        name="layer_in_even" if even else "layer_in_odd",
    )(*x_arrs, mods, mods, g1, wgu, wd, gm, *stage_arrs)
    return [o if o.shape[0] == N_KV and o.ndim == 3 else o.reshape(batch, ROWS, o.shape[-1])
            for o in outs]


def _layer_out_kernel(x_ref, *refs, n_mod, n_y, split_y, final):
    mod_refs = refs[:n_mod] * (2 // n_mod)
    wo_ref = refs[n_mod]
    y_refs = refs[n_mod + 1:n_mod + 1 + n_y]
    g2_ref, wgu_ref, wd_ref = refs[n_mod + 1 + n_y:n_mod + 4 + n_y]
    o_ref = refs[-1]
    if split_y and n_y == 2:
        y = jnp.concatenate([r[...] for r in y_refs], axis=0)
    elif split_y:
        tile0 = 2 * pl.program_id(0)

        def pick(i):
            r = (tile0 + i) % N_TILES
            even, odd, ctx = (y_ref[...] for y_ref in y_refs[3 * i:3 * i + 3])
            return jnp.where(r == N_LAT_TILES, ctx, jnp.where(r % 2 == 1, odd, even))

        y = jnp.concatenate([pick(0), pick(1)], axis=0)
    else:
        y = jnp.concatenate([r[...] for r in y_refs], axis=1)
    proj = jnp.dot(y, wo_ref[...], preferred_element_type=jnp.float32)
    x = x_ref[...] + _per_tile(lambda p, m: m[5:6, :] * p, proj, mod_refs)
    x = _ffn_half_step(x, mod_refs, 6, g2_ref[...], wgu_ref, wd_ref)
    if final:
        gf_ref = refs[-2]
        ms = jnp.mean(x * x, axis=-1, keepdims=True)
        x = x * lax.rsqrt(ms + EPS) * gf_ref[...]
    o_ref[...] = x


def _layer_out(xc, mods, wo, ys, g2, wgu, wd, layer, n_tiles, g_final=None, split_y=False):
    batch = xc.shape[0]
    j = layer // 2
    final = g_final is not None

    def const(block_shape, *index):
        return _resident(block_shape, lambda *_: index)

    if n_tiles == N_LAT_TILES:
        grid = (batch, N_LAT_TILES // 2)
        x_arr, x_spec = xc, pl.BlockSpec((None, PAIR, D), lambda b, k: (b, k, 0))
        mod_specs = [pl.BlockSpec((None, N_MOD, D), lambda b, k: (b, 0, 0))]
        y_arrs = list(ys)
        y_rows = TM if split_y else PAIR
        y_specs = [pl.BlockSpec((None, y_rows, y.shape[-1]), lambda b, k: (b, k, 0)) for y in ys]
        out_spec = pl.BlockSpec((None, PAIR, D), lambda b, k: (b, k, 0))
        out_shape = jax.ShapeDtypeStruct((batch, SEQ, D), jnp.float32)
    else:
        assert (batch * N_TILES) % 2 == 0
        grid = (batch * N_TILES // 2,)
        x_arr, x_spec = xc.reshape(batch * ROWS, D), pl.BlockSpec((PAIR, D), lambda k: (k, 0))

        def tile(k, i):
            t = 2 * k + i
            return t // N_TILES, t % N_TILES

        def mod_spec(i):
            def index(k):
                b, r = tile(k, i)
                return jnp.where(r == N_LAT_TILES, batch, b), 0, 0
            return pl.BlockSpec((None, N_MOD, D), index)

        mod_specs = [mod_spec(0), mod_spec(1)]
        if split_y:
            y_arrs, y_specs = [], []
            for i in range(2):
                def lat(k, i=i):
                    b, r = tile(k, i)
                    return b, jnp.minimum(r, N_LAT_TILES - 1) // 2, 0
                y_arrs += list(ys)
                y_specs += [pl.BlockSpec((None, TM, D), lat), pl.BlockSpec((None, TM, D), lat),
                            pl.BlockSpec((None, CTX, D), lambda k, i=i: (tile(k, i)[0], 0, 0))]
        else:
            y_arrs = [y.reshape(batch * ROWS, y.shape[-1]) for y in ys]
            y_specs = [pl.BlockSpec((PAIR, y.shape[-1]), lambda k: (k, 0)) for y in y_arrs]
        out_spec = pl.BlockSpec((PAIR, D), lambda k: (k, 0))
        out_shape = jax.ShapeDtypeStruct((batch * ROWS, D), jnp.float32)
    in_specs = [x_spec] + mod_specs + [const((None, D, D), j, 0, 0)] + y_specs + [
        const((None, 1, D), layer, 0, 0),
        const((None, D, 2 * D_FF), layer, 0, 0),
        const((None, D_FF, D), layer, 0, 0),
    ]
    args = [x_arr] + [mods] * len(mod_specs) + [wo] + y_arrs + [g2, wgu, wd]
    if final:
        in_specs.append(const((1, D), 0, 0))
        args.append(g_final)
    out = pl.pallas_call(
        functools.partial(_layer_out_kernel, n_mod=len(mod_specs), n_y=len(y_arrs),
                          split_y=split_y, final=final),
        grid=grid,
        in_specs=in_specs,
        out_specs=out_spec,
        out_shape=out_shape,
        compiler_params=_cparams(
            len(grid), resident=_FFN_WEIGHT_BYTES + D * D * BF16,
            streamed=PAIR * (2 * D * F32 + D * BF16),
            temporaries=_FFN_TEMP_BYTES + 2 * PAIR * D * F32),
        name="layer_out_final" if final else "layer_out",
    )(*args)
    return out.reshape(batch, -1, D)


HALF = SEQ // 2
FT = MXU_DIM
FW = MXU_DIM
N_FOLD_TILES = HALF // FT
DFT_SCALE = SEQ ** -0.5


def _dft_kernel(ch_ref, sh_ref, rev_ref, cc_ref, sc_ref, p_ref, q_ref, o_ref):
    rev = rev_ref[...]
    row0 = lax.broadcasted_iota(jnp.int32, (FT, FW), 0) == 0
    alt = (1 - 2 * (lax.broadcasted_iota(jnp.int32, (FT, FW), 0) & 1)).astype(jnp.float32)

    def reversed_upper(x_ref, t):
        src = HALF + FT * (N_FOLD_TILES - 1 - t)
        r = jnp.dot(rev, x_ref[src:src + FT, :], preferred_element_type=jnp.float32)
        if t == 0:
            return r
        first = HALF + FT * (N_FOLD_TILES - t)
        return jnp.where(row0, x_ref[first:first + 1, :].astype(jnp.float32), r)

    pfs, qfs, alt_sum = [], [], None
    for t in range(N_FOLD_TILES):
        rows = slice(t * FT, (t + 1) * FT)
        pf = p_ref[rows, :].astype(jnp.float32) + reversed_upper(p_ref, t)
        qf = q_ref[rows, :].astype(jnp.float32) - reversed_upper(q_ref, t)
        alt_sum = pf if alt_sum is None else alt_sum + pf
        pfs.append(pf.astype(jnp.bfloat16))
        qfs.append(qf.astype(jnp.bfloat16))
    p_mid = p_ref[HALF:HALF + 1, :].astype(jnp.float32)
    sign = jnp.concatenate([alt] * N_FOLD_TILES, axis=0)
    e = jnp.dot(ch_ref[...], jnp.concatenate(pfs, axis=0), preferred_element_type=jnp.float32)
    e = e + sign * (DFT_SCALE * p_mid)
    o = jnp.dot(sh_ref[...], jnp.concatenate(qfs, axis=0), preferred_element_type=jnp.float32)
    o_ref[0:HALF, :] = (e + o).astype(jnp.bfloat16)
    g = (e - o).astype(jnp.bfloat16)
    mid = DFT_SCALE * (jnp.sum(alt_sum * alt, axis=0, keepdims=True) + p_mid)
    for t in range(N_FOLD_TILES):
        src = FT * (N_FOLD_TILES - 1 - t)
        up = jnp.dot(rev, g[src:src + FT, :], preferred_element_type=jnp.float32)
        first = mid if t == 0 else g[src + FT:src + FT + 1, :].astype(jnp.float32)
        o_ref[HALF + t * FT:HALF + (t + 1) * FT, :] = jnp.where(row0, first, up).astype(jnp.bfloat16)
    ctx = jnp.dot(cc_ref[...], p_ref[SEQ:, :], preferred_element_type=jnp.float32)
    ctx = ctx + jnp.dot(sc_ref[...], q_ref[SEQ:, :], preferred_element_type=jnp.float32)
    o_ref[SEQ:, :] = ctx.astype(jnp.bfloat16)


def _dft(tables, pq):
    ch, sh, rev, cc, sc = tables
    batch = pq.shape[0]
    n_col = D_FOURIER // FW
    return pl.pallas_call(
        _dft_kernel,
        grid=(batch, n_col),
        in_specs=[
            _resident((HALF, HALF), lambda b, c: (0, 0)),
            _resident((HALF, HALF), lambda b, c: (0, 0)),
            _resident((FT, FT), lambda b, c: (0, 0)),
            _resident((CTX, CTX), lambda b, c: (0, 0)),
            _resident((CTX, CTX), lambda b, c: (0, 0)),
            pl.BlockSpec((None, ROWS, FW), lambda b, c: (b, 0, c)),
            pl.BlockSpec((None, ROWS, FW), lambda b, c: (b, 0, n_col + c)),
        ],
        out_specs=pl.BlockSpec((None, ROWS, FW), lambda b, c: (b, 0, c)),
        out_shape=jax.ShapeDtypeStruct((batch, ROWS, D_FOURIER), jnp.bfloat16),
        compiler_params=_cparams(
            2, resident=2 * HALF * HALF * BF16, streamed=3 * ROWS * FW * BF16,
            temporaries=HALF * FW * (2 * BF16 + 3 * F32) + 2 * ROWS * FW * F32),
        name="dft",
    )(ch, sh, rev, cc, sc, pq, pq)


KEY_TILE = MXU_DIM
N_KEY_TILES = ROWS // KEY_TILE


def _with_row_sums(v):
    return jnp.concatenate([v, jnp.ones_like(v)], axis=1)


def _normalised(pv):
    return (pv[:, :HEAD_DIM] / pv[:, HEAD_DIM:]).astype(jnp.bfloat16)


def _attn_kernel(q_ref, kt_ref, v_prev_ref, v_ref, o_prev_ref, o_ref, s_ref, m_ref):
    @pl.when(pl.program_id(0) == 0)
    def _():
        s_ref[...] = jnp.zeros(s_ref.shape, s_ref.dtype)
        m_ref[...] = jnp.zeros(m_ref.shape, m_ref.dtype)

    for half, (vals_ref, out_ref) in enumerate(((v_prev_ref, o_prev_ref), (v_ref, o_ref))):
        outs = []
        for hd in range(GROUP):
            q = q_ref[half * TM:(half + 1) * TM, hd * HEAD_DIM:(hd + 1) * HEAD_DIM]
            m_prev = m_ref[hd]
            m_lane = None
            acc = None
            for j in range(N_KEY_TILES):
                keys = slice(j * KEY_TILE, (j + 1) * KEY_TILE)
                s_old = s_ref[hd, :, keys]
                p = jnp.exp2(s_old - jnp.concatenate([m_prev] * (KEY_TILE // HEAD_DIM), axis=1))
                pv = jnp.dot(p.astype(jnp.bfloat16), _with_row_sums(vals_ref[keys, :]),
                             preferred_element_type=jnp.float32)
                acc = pv if acc is None else acc + pv
                s_new = jnp.dot(q, kt_ref[:, keys], preferred_element_type=jnp.float32)
                s_ref[hd, :, keys] = s_new
                for c in range(KEY_TILE // HEAD_DIM):
                    part = s_new[:, c * HEAD_DIM:(c + 1) * HEAD_DIM]
                    m_lane = part if m_lane is None else jnp.maximum(m_lane, part)
            outs.append(_normalised(acc))
            m_ref[hd] = jnp.broadcast_to(jnp.max(m_lane, axis=-1, keepdims=True), m_lane.shape)
        out_ref[...] = jnp.concatenate(outs, axis=1)


def _attention_latent(q, kt, v):
    batch = q.shape[0]
    gw = GROUP * HEAD_DIM
    n_pairs = N_LAT_TILES // 2
    per_sample = N_KV * n_pairs
    steps = batch * per_sample + 1

    def unflatten(c):
        return c // per_sample, (c % per_sample) // n_pairs, c % n_pairs

    def cur(t):
        return unflatten(jnp.minimum(t, steps - 2))

    def prev(t):
        return unflatten(jnp.maximum(t - 1, 0))

    def tile_map(which):
        def index(t):
            b, g, i = which(t)
            return b, i, g
        return index

    def kt_map(t):
        b, g, _ = cur(t)
        return g, 0, b

    def v_map(which):
        def index(t):
            b, g, _ = which(t)
            return b, 0, g
        return index

    half_shape = jax.ShapeDtypeStruct((batch, SEQ // 2, D), jnp.bfloat16)
    odd, even = pl.pallas_call(
        _attn_kernel,
        grid=(steps,),
        in_specs=[
            pl.BlockSpec((None, PAIR, gw), tile_map(cur)),
            pl.BlockSpec((None, HEAD_DIM, ROWS), kt_map),
            pl.BlockSpec((None, ROWS, HEAD_DIM), v_map(prev)),
            pl.BlockSpec((None, ROWS, HEAD_DIM), v_map(cur)),
        ],
        out_specs=[pl.BlockSpec((None, TM, gw), tile_map(prev)),
                   pl.BlockSpec((None, TM, gw), tile_map(cur))],
        out_shape=[half_shape, half_shape],
        scratch_shapes=[pltpu.VMEM((GROUP, TM, ROWS), jnp.float32),
                        pltpu.VMEM((GROUP, TM, HEAD_DIM), jnp.float32)],
        compiler_params=_cparams(
            1, streamed=(3 * TM * gw + 3 * ROWS * HEAD_DIM) * BF16 + TM * gw * BF16,
            scratch=GROUP * TM * (ROWS + HEAD_DIM) * F32,
            temporaries=2 * GROUP * TM * 2 * HEAD_DIM * F32 + 4 * TM * KEY_TILE * F32),
        name="attention",
    )(q, kt, v, v)
    return even, odd


def _attn_ctx_kernel(q_ref, kt_ref, v_ref, o_ref):
    v1 = _with_row_sums(v_ref[...])
    kt = kt_ref[...]
    outs = []
    for hd in range(GROUP):
        s = jnp.dot(q_ref[:, hd * HEAD_DIM:(hd + 1) * HEAD_DIM], kt,
                    preferred_element_type=jnp.float32)
        p = jnp.exp2(s - jnp.max(s, axis=-1, keepdims=True)).astype(jnp.bfloat16)
        outs.append(_normalised(jnp.dot(p, v1, preferred_element_type=jnp.float32)))
    o_ref[...] = jnp.concatenate(outs, axis=1)


def _attention_context(q, kt, v):
    batch = q.shape[0]
    gw = GROUP * HEAD_DIM
    return pl.pallas_call(
        _attn_ctx_kernel,
        grid=(batch, N_KV),
        in_specs=[
            pl.BlockSpec((None, CTX, gw), lambda b, g: (b, SEQ // CTX, g)),
            pl.BlockSpec((None, HEAD_DIM, CTX), lambda b, g: (g, 0, b * N_TILES + N_LAT_TILES)),
            pl.BlockSpec((None, CTX, HEAD_DIM), lambda b, g: (b, SEQ // CTX, g)),
        ],
        out_specs=pl.BlockSpec((None, CTX, gw), lambda b, g: (b, 0, g)),
        out_shape=jax.ShapeDtypeStruct((batch, CTX, D), jnp.bfloat16),
        compiler_params=_cparams(
            2, streamed=2 * CTX * (gw + HEAD_DIM) * BF16,
            temporaries=GROUP * CTX * (2 * CTX + 2 * HEAD_DIM) * F32),
        name="attention_ctx",
    )(q, kt, v)


def _cos_sin_products(n, size):
    r = 1 << (int(math.log2(size)) // 2)
    hi = size // r
    assert r % SUBLANES == 0
    k = jnp.arange(size, dtype=jnp.int32)[None, :]
    ang_hi = ((jnp.arange(hi, dtype=jnp.int32)[:, None] * r * k) % n).astype(jnp.float32)
    ang_lo = ((jnp.arange(r, dtype=jnp.int32)[:, None] * k) % n).astype(jnp.float32)
    w = 2.0 * math.pi / n
    ca, sa = jnp.cos(ang_hi * w)[:, None, :], jnp.sin(ang_hi * w)[:, None, :]
    cb, sb = jnp.cos(ang_lo * w)[None, :, :], jnp.sin(ang_lo * w)[None, :, :]
    scale = n ** -0.5
    cos = ((ca * cb - sa * sb) * scale).reshape(size, size)
    sin = ((sa * cb + ca * sb) * scale).reshape(size, size)
    return cos, sin


def _dft_tables():
    ch, sh = _cos_sin_products(SEQ, HALF)
    cc, sc = _cos_sin_products(CTX, CTX)
    r = jnp.arange(FT, dtype=jnp.int32)
    rev = ((r[:, None] + r[None, :]) == FT).astype(jnp.bfloat16)
    return tuple(t.astype(jnp.bfloat16) for t in (ch, sh, rev, cc, sc))


def _channel_dft():
    c, s = _cos_sin_products(GROUP_CH, GROUP_CH)
    return jnp.concatenate([c, -s], axis=1).astype(jnp.bfloat16)


def _rope_tables():
    half = HEAD_DIM // 2
    rows = SEQ // GRID_W
    row = jnp.repeat(jnp.arange(rows, dtype=jnp.float32), GRID_W)
    col = jnp.tile(jnp.arange(GRID_W, dtype=jnp.float32), rows)
    inv_freq = ROPE_THETA ** (-jnp.arange(0, half, 2, dtype=jnp.float32) / half)
    ang_row = row[:, None] * inv_freq
    ang_col = col[:, None] * inv_freq
    cos = jnp.concatenate([jnp.cos(ang_row)] * 2 + [jnp.cos(ang_col)] * 2, axis=1)
    sin = jnp.concatenate([-jnp.sin(ang_row), jnp.sin(ang_row),
                           -jnp.sin(ang_col), jnp.sin(ang_col)], axis=1)
    cos = jnp.concatenate([cos, jnp.ones((CTX, HEAD_DIM), jnp.float32)], axis=0)
    sin = jnp.concatenate([sin, jnp.zeros((CTX, HEAD_DIM), jnp.float32)], axis=0)
    order = _rope_order()
    return cos[:, order], sin[:, order]


def _rope_order():
    quarter = HEAD_DIM // 4
    blocks = [0, 2, 1, 3]
    return jnp.concatenate([jnp.arange(b * quarter, (b + 1) * quarter) for b in blocks])


def _permute_qk_heads(w_qkv, g_q, g_k):
    order = _rope_order()
    n_cols = w_qkv.shape[-1]
    n_qk = (N_HEADS + N_KV) * HEAD_DIM
    heads = jnp.arange(n_qk, dtype=jnp.int32) // HEAD_DIM * HEAD_DIM
    src = jnp.concatenate([heads + jnp.tile(order, N_HEADS + N_KV),
                           jnp.arange(n_qk, n_cols, dtype=jnp.int32)])
    select = (jnp.arange(n_cols, dtype=jnp.int32)[:, None] == src[None, :]).astype(jnp.bfloat16)
    w = jnp.einsum("lik,kj->lij", w_qkv.astype(jnp.bfloat16), select,
                   preferred_element_type=jnp.float32).astype(jnp.bfloat16)
    return w, g_q[:, order], g_k[:, order]


def kernel(x, c, ctx, c_ctx, w_mod, b_mod, g_ffn1, w_ffn1_gu, w_ffn1_down, g_mix, g_ffn2, w_ffn2_gu, w_ffn2_down, g_final, w_in_ab, g_v, w_s, b_s, w_out_ab, w_qkv, g_q, g_k, w_o):
    batch = x.shape[0]
    depth = w_mod.shape[0]
    assert x.shape == (batch, SEQ, D) and ctx.shape == (batch, CTX, D)
    assert depth % 2 == 0

    mod_rows = -(-(batch + 1) // SUBLANES) * SUBLANES
    cc = jnp.concatenate([c, c_ctx[None], jnp.zeros((mod_rows - batch - 1, D), jnp.float32)], axis=0)
    m = _modulation(cc, w_mod, b_mod)
    m = m[:, :batch + 1].reshape(depth, batch + 1, N_MOD, D)

    wgu1, wd1 = w_ffn1_gu.astype(jnp.bfloat16), w_ffn1_down.astype(jnp.bfloat16)
    wgu2, wd2 = w_ffn2_gu.astype(jnp.bfloat16), w_ffn2_down.astype(jnp.bfloat16)
    g1 = g_ffn1.reshape(depth, 1, D)
    g2 = g_ffn2.reshape(depth, 1, D)
    gm = g_mix.reshape(depth, 1, D)
    w_in = w_in_ab.astype(jnp.bfloat16)
    w_out = w_out_ab.astype(jnp.bfloat16)
    ws = w_s.astype(jnp.bfloat16)
    gv = g_v.reshape(-1, 1, D_SGU)
    bs_full = jnp.repeat(jnp.swapaxes(b_s, 1, 2), GROUP_CH, axis=2)
    wqkv, g_q_p, g_k_p = _permute_qk_heads(w_qkv, g_q, g_k)
    wo = w_o.astype(jnp.bfloat16)
    gq = g_q_p.reshape(-1, 1, HEAD_DIM)
    gk = g_k_p.reshape(-1, 1, HEAD_DIM)
    dft_tables = _dft_tables()
    ccs = _channel_dft()
    cos_t, sin_t = _rope_tables()

    xs = (x, ctx)
    for l in range(depth):
        last = l == depth - 1
        even = l % 2 == 0
        mods = m[l]
        n_out = N_LAT_TILES if last else N_TILES
        if even:
            xc, pq, sgu = _layer_in(xs, mods, g1, wgu1, wd1, gm, l, True,
                                    (w_in, ccs, gv, ws, bs_full))
            ys, w_proj = [_dft(dft_tables, pq), sgu], w_out
        else:
            xc, q, kt, v = _layer_in(xs, mods, g1, wgu1, wd1, gm, l, False,
                                     (wqkv, gq, gk, cos_t, sin_t))
            ys, w_proj = list(_attention_latent(q, kt, v)), wo
            if not last:
                ys.append(_attention_context(q, kt, v))
        xc = _layer_out(xc, mods, w_proj, ys, g2, wgu2, wd2, l, n_out,
                        g_final.reshape(1, D) if last else None,
                        split_y=not even)
        xs = (xc,)
    return xc
```
